```python
import jax, jax.numpy as jnp
from jax import lax
import numpy as np

D_MODEL = 2048
BATCH = 1
SEQ = 8192
DEPTH = 4
DEC_BATCH = 32
DEC_SEQ = 32
PAST_LEN = 2048

CHUNK = 64
N_EVEN = (DEPTH + 1) // 2
N_ODD = DEPTH // 2
FFN_DIM = 5632
POOL_WINDOWS = (2, 4, 8, 16)
N_POOL_GROUPS = 4
POOL_DIM = 1024
POOL_GROUP_DIM = POOL_DIM // N_POOL_GROUPS
POOL_HIST = max(POOL_WINDOWS) - 1
MLA_HEADS = 8
Q_LORA = 512
KV_LORA = 512
QK_NOPE = 128
QK_ROPE = 64
V_HEAD = 128
QK_HEAD = QK_NOPE + QK_ROPE
ATTN_SCALE = QK_HEAD ** -0.5
ROPE_THETA = 10000.0
Q_BLOCK = 128
GMLP_CHUNK = 128
GATE_DIM = 2048
GMLP_GROUPS = 8
GMLP_GROUP_DIM = GATE_DIM // GMLP_GROUPS
EVEN_IN = POOL_DIM + Q_LORA + KV_LORA + QK_ROPE
EVEN_MIX = POOL_DIM + MLA_HEADS * V_HEAD
EPS = 1e-6

kernel_name = "pool_mla_gmlp_macaron_stream_step"


def rmsnorm(x, g):
    xf = x.astype(jnp.float32)
    y = xf * lax.rsqrt(jnp.mean(xf * xf, axis=-1, keepdims=True) + EPS)
    return (y * g.astype(jnp.float32)).astype(x.dtype)


def swiglu(h, w_gate, w_up, w_down):
    return (jax.nn.silu(h @ w_gate) * (h @ w_up)) @ w_down


def rope_cos_sin(pos):
    inv = ROPE_THETA ** (-jnp.arange(0, QK_ROPE, 2, dtype=jnp.float32) / QK_ROPE)
    ang = pos.astype(jnp.float32)[:, None] * inv[None, :]
    return jnp.cos(ang), jnp.sin(ang)


def apply_rope(x, cos, sin):
    x1, x2 = jnp.split(x.astype(jnp.float32), 2, axis=-1)
    out = jnp.concatenate([x1 * cos - x2 * sin, x1 * sin + x2 * cos], axis=-1)
    return out.astype(x.dtype)


def multi_pool(u, hist, pos, pool_w, pool_scale):
    B, L, _ = u.shape
    ext = jnp.concatenate([hist, u], axis=1)
    cs = jnp.pad(jnp.cumsum(ext.astype(jnp.float32), axis=1), ((0, 0), (1, 0), (0, 0)))
    posf = pos.astype(jnp.float32)
    outs = []
    for g, w in enumerate(POOL_WINDOWS):
        sl = slice(g * POOL_GROUP_DIM, (g + 1) * POOL_GROUP_DIM)
        s = cs[:, POOL_HIST + 1:, sl] - cs[:, POOL_HIST + 1 - w:POOL_HIST + 1 - w + L, sl]
        cnt = jnp.minimum(jnp.float32(w), posf + 1.0)
        outs.append(s / cnt[None, :, None])
    pooled = jnp.stack(outs, axis=2) - u.reshape(B, L, N_POOL_GROUPS, POOL_GROUP_DIM).astype(jnp.float32)
    y = jnp.einsum('blgc,gcd->blgd', pooled.astype(u.dtype), pool_w).reshape(B, L, POOL_DIM)
    return y * pool_scale, ext[:, -POOL_HIST:]


def mla_block(q_nope, q_pe, k_nope, k_pe, v, q_pos, k_pos):
    s = (jnp.einsum('bqhd,bkhd->bhqk', q_nope, k_nope)
         + jnp.einsum('bqhr,bkr->bhqk', q_pe, k_pe)).astype(jnp.float32) * ATTN_SCALE
    mask = (k_pos[None, :] // CHUNK) <= (q_pos[:, None] // CHUNK)
    s = jnp.where(mask[None, None], s, jnp.finfo(jnp.float32).min)
    p = jax.nn.softmax(s, axis=-1).astype(v.dtype)
    return jnp.einsum('bhqk,bkhd->bqhd', p, v)


def mla_attention(q_nope, q_pe, k_nope, k_pe, v, q_pos, k_pos):
    B, Lq = q_nope.shape[0], q_nope.shape[1]
    if Lq <= Q_BLOCK:
        return mla_block(q_nope, q_pe, k_nope, k_pe, v, q_pos, k_pos)
    nb = Lq // Q_BLOCK
    qn = q_nope.reshape(B, nb, Q_BLOCK, MLA_HEADS, QK_NOPE).swapaxes(0, 1)
    qp = q_pe.reshape(B, nb, Q_BLOCK, MLA_HEADS, QK_ROPE).swapaxes(0, 1)
    qpos = q_pos.reshape(nb, Q_BLOCK)
    out = lax.map(lambda a: mla_block(a[0], a[1], k_nope, k_pe, v, a[2], k_pos), (qn, qp, qpos))
    return out.swapaxes(0, 1).reshape(B, Lq, MLA_HEADS, V_HEAD)


def even_mixer(h, pos, ckv_hist, kpe_hist, pool_hist, w_in, q_a_norm, kv_a_norm, w_qb, w_kvb,
               q_nope_norm, q_pe_norm, k_nope_norm, k_pe_norm, pool_w, pool_scale, w_out):
    B, L, _ = h.shape
    z = h @ w_in
    u_pool, q_a, kv_a, kpe_raw = jnp.split(z, [POOL_DIM, POOL_DIM + Q_LORA, POOL_DIM + Q_LORA + KV_LORA], axis=-1)
    pool_out, pool_state = multi_pool(u_pool, pool_hist, pos, pool_w, pool_scale)
    cos, sin = rope_cos_sin(pos)
    q = (rmsnorm(q_a, q_a_norm) @ w_qb).reshape(B, L, MLA_HEADS, QK_HEAD)
    q_nope = rmsnorm(q[..., :QK_NOPE], q_nope_norm)
    q_pe = apply_rope(rmsnorm(q[..., QK_NOPE:], q_pe_norm), cos[:, None, :], sin[:, None, :])
    ckv = rmsnorm(kv_a, kv_a_norm)
    kpe = apply_rope(rmsnorm(kpe_raw, k_pe_norm), cos, sin)
    if ckv_hist is None:
        ckv_all, kpe_all, k_pos = ckv, kpe, pos
    else:
        ckv_all = jnp.concatenate([ckv_hist, ckv], axis=1)
        kpe_all = jnp.concatenate([kpe_hist, kpe], axis=1)
        k_pos = jnp.concatenate([jnp.arange(ckv_hist.shape[1], dtype=jnp.int32), pos])
    Lk = ckv_all.shape[1]
    kv = (ckv_all @ w_kvb).reshape(B, Lk, MLA_HEADS, QK_NOPE + V_HEAD)
    k_nope = rmsnorm(kv[..., :QK_NOPE], k_nope_norm)
    v = kv[..., QK_NOPE:]
    attn = mla_attention(q_nope, q_pe, k_nope, kpe_all, v, pos, k_pos)
    mix = jnp.concatenate([pool_out, attn.reshape(B, L, MLA_HEADS * V_HEAD)], axis=-1)
    return mix @ w_out, ckv, kpe, pool_state


def odd_mixer(h, w_in, v_norm, w_s, b_s, w_out):
    B, L, _ = h.shape
    z = jax.nn.gelu(h @ w_in)
    u, v = jnp.split(z, 2, axis=-1)
    v = rmsnorm(v, v_norm)
    tri = jnp.tril(jnp.ones((GMLP_CHUNK, GMLP_CHUNK), dtype=w_s.dtype))
    ws = w_s * tri[None]
    vg = v.reshape(B, L, GMLP_GROUPS, GMLP_GROUP_DIM)
    if L >= GMLP_CHUNK:
        nc = L // GMLP_CHUNK
        vc = vg.reshape(B, nc, GMLP_CHUNK, GMLP_GROUPS, GMLP_GROUP_DIM)
        s = jnp.einsum('gij,bnjgc->bnigc', ws, vc) + b_s.T[None, None, :, :, None]
        s = s.reshape(B, L, GATE_DIM)
    else:
        s = jnp.einsum('gij,bjgc->bigc', ws[:, :L, :L], vg) + b_s[:, :L].T[None, :, :, None]
        s = s.reshape(B, L, GATE_DIM)
    return (u * s) @ w_out, v


def trunk(x, start_pos, ckv_cache, kpe_cache, pool_cache, w):
    B, L, _ = x.shape
    pos = start_pos + jnp.arange(L, dtype=jnp.int32)
    new_ckv, new_kpe, new_pool, new_v = [], [], [], []
    for layer in range(DEPTH):
        x = x + 0.5 * swiglu(rmsnorm(x, w['norm_ffn1'][layer]), w['ffn1_w_gate'][layer],
                             w['ffn1_w_up'][layer], w['ffn1_w_down'][layer])
        h = rmsnorm(x, w['norm_mix'][layer])
        if layer % 2 == 0:
            e = layer // 2
            if ckv_cache is None:
                ckv_h, kpe_h = None, None
                pool_h = jnp.zeros((B, POOL_HIST, POOL_DIM), x.dtype)
            else:
                ckv_h, kpe_h, pool_h = ckv_cache[e], kpe_cache[e], pool_cache[e]
            m, ckv, kpe, pst = even_mixer(
                h, pos, ckv_h, kpe_h, pool_h, w['ev_w_in'][e], w['ev_q_a_norm'][e], w['ev_kv_a_norm'][e],
                w['ev_w_qb'][e], w['ev_w_kvb'][e], w['ev_q_nope_norm'][e], w['ev_q_pe_norm'][e],
                w['ev_k_nope_norm'][e], w['ev_k_pe_norm'][e], w['ev_pool_w'][e], w['ev_pool_scale'][e],
                w['ev_w_out'][e])
            new_ckv.append(ckv)
            new_kpe.append(kpe)
            new_pool.append(pst)
        else:
            o = layer // 2
            m, v = odd_mixer(h, w['od_w_in'][o], w['od_v_norm'][o], w['od_w_s'][o], w['od_b_s'][o], w['od_w_out'][o])
            new_v.append(v)
        x = x + m
        x = x + 0.5 * swiglu(rmsnorm(x, w['norm_ffn2'][layer]), w['ffn2_w_gate'][layer],
                             w['ffn2_w_up'][layer], w['ffn2_w_down'][layer])
    return x, jnp.stack(new_ckv), jnp.stack(new_kpe), jnp.stack(new_pool), jnp.stack(new_v)


def setup_inputs(seed: int = 0) -> dict:
    key = jax.random.key(seed)
    ks = jax.random.split(key, 32)
    f32 = jnp.float32

    def nrm(k, shape, scale=1.0):
        return jax.random.normal(k, shape, f32) * scale

    def gain(k, shape):
        return 1.0 + 0.02 * jax.random.normal(k, shape, f32)

    return {
        "x_prompt": nrm(ks[0], (BATCH, SEQ, D_MODEL)),
        "x_sample": nrm(ks[1], (DEC_BATCH, DEC_SEQ, D_MODEL)),
        "cache_mla_ckv": nrm(ks[2], (N_EVEN, DEC_BATCH, PAST_LEN, KV_LORA)),
        "cache_mla_kpe": nrm(ks[3], (N_EVEN, DEC_BATCH, PAST_LEN, QK_ROPE)),
        "state_pool": nrm(ks[4], (N_EVEN, DEC_BATCH, POOL_HIST, POOL_DIM)),
        "norm_ffn1": gain(ks[5], (DEPTH, D_MODEL)),
        "norm_mix": gain(ks[6], (DEPTH, D_MODEL)),
        "norm_ffn2": gain(ks[7], (DEPTH, D_MODEL)),
        "ffn1_w_gate": nrm(ks[8], (DEPTH, D_MODEL, FFN_DIM), D_MODEL ** -0.5),
        "ffn1_w_up": nrm(ks[9], (DEPTH, D_MODEL, FFN_DIM), D_MODEL ** -0.5),
        "ffn1_w_down": nrm(ks[10], (DEPTH, FFN_DIM, D_MODEL), FFN_DIM ** -0.5),
        "ffn2_w_gate": nrm(ks[11], (DEPTH, D_MODEL, FFN_DIM), D_MODEL ** -0.5),
        "ffn2_w_up": nrm(ks[12], (DEPTH, D_MODEL, FFN_DIM), D_MODEL ** -0.5),
        "ffn2_w_down": nrm(ks[13], (DEPTH, FFN_DIM, D_MODEL), FFN_DIM ** -0.5),
        "ev_w_in": nrm(ks[14], (N_EVEN, D_MODEL, EVEN_IN), D_MODEL ** -0.5),
        "ev_q_a_norm": gain(ks[15], (N_EVEN, Q_LORA)),
        "ev_kv_a_norm": gain(ks[16], (N_EVEN, KV_LORA)),
        "ev_w_qb": nrm(ks[17], (N_EVEN, Q_LORA, MLA_HEADS * QK_HEAD), Q_LORA ** -0.5),
        "ev_w_kvb": nrm(ks[18], (N_EVEN, KV_LORA, MLA_HEADS * (QK_NOPE + V_HEAD)), KV_LORA ** -0.5),
        "ev_q_nope_norm": gain(ks[19], (N_EVEN, QK_NOPE)),
        "ev_q_pe_norm": gain(ks[20], (N_EVEN, QK_ROPE)),
        "ev_k_nope_norm": gain(ks[21], (N_EVEN, QK_NOPE)),
        "ev_k_pe_norm": gain(ks[22], (N_EVEN, QK_ROPE)),
        "ev_pool_w": nrm(ks[23], (N_EVEN, N_POOL_GROUPS, POOL_GROUP_DIM, POOL_GROUP_DIM), POOL_GROUP_DIM ** -0.5),
        "ev_pool_scale": gain(ks[24], (N_EVEN, POOL_DIM)),
        "ev_w_out": nrm(ks[25], (N_EVEN, EVEN_MIX, D_MODEL), EVEN_MIX ** -0.5),
        "od_w_in": nrm(ks[26], (N_ODD, D_MODEL, 2 * GATE_DIM), D_MODEL ** -0.5),
        "od_v_norm": gain(ks[27], (N_ODD, GATE_DIM)),
        "od_w_s": nrm(ks[28], (N_ODD, GMLP_GROUPS, GMLP_CHUNK, GMLP_CHUNK), GMLP_CHUNK ** -0.5),
        "od_b_s": 1.0 + 0.1 * jax.random.normal(ks[29], (N_ODD, GMLP_GROUPS, GMLP_CHUNK), f32),
        "od_w_out": nrm(ks[30], (N_ODD, GATE_DIM, D_MODEL), GATE_DIM ** -0.5),
    }


def reference(x_prompt, x_sample, cache_mla_ckv, cache_mla_kpe, state_pool,
              norm_ffn1, norm_mix, norm_ffn2,
              ffn1_w_gate, ffn1_w_up, ffn1_w_down, ffn2_w_gate, ffn2_w_up, ffn2_w_down,
              ev_w_in, ev_q_a_norm, ev_kv_a_norm, ev_w_qb, ev_w_kvb,
              ev_q_nope_norm, ev_q_pe_norm, ev_k_nope_norm, ev_k_pe_norm,
              ev_pool_w, ev_pool_scale, ev_w_out,
              od_w_in, od_v_norm, od_w_s, od_b_s, od_w_out):
    w = {
        'norm_ffn1': norm_ffn1, 'norm_mix': norm_mix, 'norm_ffn2': norm_ffn2,
        'ffn1_w_gate': ffn1_w_gate, 'ffn1_w_up': ffn1_w_up, 'ffn1_w_down': ffn1_w_down,
        'ffn2_w_gate': ffn2_w_gate, 'ffn2_w_up': ffn2_w_up, 'ffn2_w_down': ffn2_w_down,
        'ev_w_in': ev_w_in, 'ev_q_a_norm': ev_q_a_norm, 'ev_kv_a_norm': ev_kv_a_norm,
        'ev_w_qb': ev_w_qb, 'ev_w_kvb': ev_w_kvb,
        'ev_q_nope_norm': ev_q_nope_norm, 'ev_q_pe_norm': ev_q_pe_norm,
        'ev_k_nope_norm': ev_k_nope_norm, 'ev_k_pe_norm': ev_k_pe_norm,
        'ev_pool_w': ev_pool_w, 'ev_pool_scale': ev_pool_scale, 'ev_w_out': ev_w_out,
        'od_w_in': od_w_in, 'od_v_norm': od_v_norm, 'od_w_s': od_w_s, 'od_b_s': od_b_s, 'od_w_out': od_w_out,
    }
    y_prompt, new_ckv_prompt, new_kpe_prompt, new_pool_prompt, _ = trunk(x_prompt, 0, None, None, None, w)
    y_sample, new_ckv_sample, new_kpe_sample, new_pool_sample, new_v_sample = trunk(
        x_sample, PAST_LEN, cache_mla_ckv, cache_mla_kpe, state_pool, w)
    return (y_prompt, y_sample, new_ckv_prompt, new_kpe_prompt, new_pool_prompt,
            new_ckv_sample, new_kpe_sample, new_pool_sample, new_v_sample)
```

```python
import functools
import math

import jax
import jax.numpy as jnp
from jax import lax
from jax.experimental import pallas as pl
from jax.experimental.pallas import tpu as pltpu

F32 = jnp.float32
BF16 = jnp.bfloat16

EPS = 1e-6
CHUNK = 64
POOL_WINDOWS = (2, 4, 8, 16)
POOL_HIST = max(POOL_WINDOWS) - 1
HIST_ROWS = 16
MLA_HEADS = 8
QK_NOPE = 128
QK_ROPE = 64
V_HEAD = 128
QK_HEAD = QK_NOPE + QK_ROPE
ATTN_SCALE = QK_HEAD ** -0.5
ROPE_THETA = 10000.0
GMLP_CHUNK = 128
GMLP_GROUPS = 8
LANES = 128
NEG_BIG = -1e30
VMEM_LIMIT = 56 * 1024 * 1024


def _params(*sem):
    return pltpu.CompilerParams(dimension_semantics=sem, vmem_limit_bytes=VMEM_LIMIT)


def _tile(n, pref, mult=8):
    if n <= pref:
        return n
    for t in range(pref - pref % mult, 0, -mult):
        if n % t == 0:
            return t
    raise ValueError(f"no tile for {n}")


def _const_spec(shape):
    nd = len(shape)
    return pl.BlockSpec(shape, lambda *_: (0,) * nd, pipeline_mode=pl.Buffered(1))


def _rms(x, g):
    ms = jnp.mean(x * x, axis=-1, keepdims=True)
    return x * lax.rsqrt(ms + EPS) * g


def _dot(a, b):
    return jnp.dot(a, b, preferred_element_type=F32)


def _dot_nt(a, b):
    return lax.dot_general(a, b, (((1,), (1,)), ((), ())), preferred_element_type=F32)


def _ffn_body(x_ref, g_ref, wg_ref, wu_ref, wd_ref, o_ref, h_ref):
    @pl.when(pl.program_id(1) == 0)
    def _():
        x = x_ref[...]
        h_ref[...] = _rms(x, g_ref[...]).astype(BF16)
        o_ref[...] = x

    h = h_ref[...]
    a = _dot(h, wg_ref[...])
    b = _dot(h, wu_ref[...])
    act = (a * jax.nn.sigmoid(a) * b).astype(BF16)
    o_ref[...] += 0.5 * _dot(act, wd_ref[...])


def _ffn(x, g, wg, wu, wd, layer):
    m, d = x.shape
    f = wg.shape[-1]
    tm, tf = _tile(m, 512), _tile(f, 512, LANES)
    return pl.pallas_call(
        _ffn_body,
        grid=(m // tm, f // tf),
        in_specs=[
            pl.BlockSpec((tm, d), lambda i, j: (i, 0)),
            pl.BlockSpec((None, 1, d), lambda i, j: (layer, 0, 0)),
            pl.BlockSpec((None, d, tf), lambda i, j: (layer, 0, j)),
            pl.BlockSpec((None, d, tf), lambda i, j: (layer, 0, j)),
            pl.BlockSpec((None, tf, d), lambda i, j: (layer, j, 0)),
        ],
        out_specs=pl.BlockSpec((tm, d), lambda i, j: (i, 0)),
        out_shape=jax.ShapeDtypeStruct((m, d), F32),
        scratch_shapes=[pltpu.VMEM((tm, d), BF16)],
        compiler_params=_params("parallel", "arbitrary"),
        name="ffn",
    )(x, g, wg, wu, wd)


def _rope_tab_body(inv_ref, sgn_ref, cos_ref, sin_ref, *, tm, lp, past, s):
    r = lax.broadcasted_iota(jnp.int32, (tm, LANES), 0) + pl.program_id(0) * tm
    t = r - lp
    t = (t & (s - 1)) if s & (s - 1) == 0 else lax.rem(t, s)
    pos = jnp.where(r >= lp, past + t, r).astype(F32)
    ang = pos * inv_ref[...]
    cos_ref[...] = jnp.cos(ang)
    sin_ref[...] = jnp.sin(ang) * sgn_ref[...]


def _rope_tables(m, lp, past, s):
    half = QK_ROPE // 2
    inv = ROPE_THETA ** (-jnp.arange(0, QK_ROPE, 2, dtype=F32) / QK_ROPE)
    inv = jnp.tile(inv, LANES // half)[None, :]
    sgn = jnp.tile(jnp.concatenate([-jnp.ones((half,), F32), jnp.ones((half,), F32)]), LANES // QK_ROPE)[None, :]
    tm = _tile(m, 512)
    return pl.pallas_call(
        functools.partial(_rope_tab_body, tm=tm, lp=lp, past=past, s=s),
        grid=(m // tm,),
        in_specs=[_const_spec((1, LANES)), _const_spec((1, LANES))],
        out_specs=[pl.BlockSpec((tm, LANES), lambda i: (i, 0))] * 2,
        out_shape=[jax.ShapeDtypeStruct((m, LANES), F32)] * 2,
        compiler_params=_params("parallel"),
        name="rope_tables",
    )(inv, sgn)


def _swap_halves(x):
    lane = lax.broadcasted_iota(jnp.int32, x.shape, 1)
    left = pltpu.roll(x, LANES - QK_ROPE // 2, 1)
    right = pltpu.roll(x, QK_ROPE // 2, 1)
    return jnp.where((lane & (QK_ROPE - 1)) < QK_ROPE // 2, left, right)


def _even_in_body(x_ref, g_ref, win_ref, qan_ref, kvan_ref, wqb_ref, qnn_ref, qpn_ref, kpn_ref, cos_ref, sin_ref,
                  u_ref, q_ref, ckv_ref, kpe_ref, *, pool_dim, q_lora, kv_lora):
    h = _rms(x_ref[...], g_ref[...]).astype(BF16)
    z = _dot(h, win_ref[...])
    u_ref[...] = z[:, :pool_dim]
    o1, o2 = pool_dim + q_lora, pool_dim + q_lora + kv_lora
    ckv_ref[...] = _rms(z[:, o1:o2], kvan_ref[...])
    cos, sin = cos_ref[...], sin_ref[...]
    lane = lax.broadcasted_iota(jnp.int32, cos.shape, 1)
    lo = lane < QK_ROPE

    def rope(t):
        return t * cos + _swap_halves(t) * sin

    kp = z[:, o2:o2 + LANES]
    kp = kp * lax.rsqrt(jnp.sum(kp * kp, axis=-1, keepdims=True) / QK_ROPE + EPS) * kpn_ref[...]
    kpe_ref[...] = rope(kp)

    qn = _rms(z[:, pool_dim:o1], qan_ref[...]).astype(BF16)
    q = _dot(qn, wqb_ref[...])
    nope_w = MLA_HEADS * QK_NOPE
    for hd in range(MLA_HEADS):
        qh = _rms(q[:, hd * QK_NOPE:(hd + 1) * QK_NOPE], qnn_ref[...]) * ATTN_SCALE
        q_ref[hd, :, :QK_NOPE] = qh.astype(BF16)
    for j in range(MLA_HEADS // 2):
        t = q[:, nope_w + j * LANES:nope_w + (j + 1) * LANES]
        tt = t * t
        s_lo = jnp.sum(jnp.where(lo, tt, 0.0), axis=-1, keepdims=True)
        s_hi = jnp.sum(jnp.where(lo, 0.0, tt), axis=-1, keepdims=True)
        inv = jnp.where(lo, lax.rsqrt(s_lo / QK_ROPE + EPS), lax.rsqrt(s_hi / QK_ROPE + EPS))
        t = (rope(t * inv * qpn_ref[...]) * ATTN_SCALE).astype(BF16)
        q_ref[2 * j, :, QK_NOPE:] = t
        q_ref[2 * j + 1, :, QK_NOPE:] = t


def _even_in(x, g, layer, win, qan, kvan, wqb, qnn, qpn, kpn, cos, sin, e, pool_dim, q_lora, kv_lora):
    m, d = x.shape
    tm = _tile(m, 256)
    n_in = win.shape[-1]
    n_q = wqb.shape[-1]
    row = lambda w: pl.BlockSpec((tm, w), lambda i: (i, 0))
    vec = lambda w, idx: pl.BlockSpec((None, 1, w), lambda i: (idx, 0, 0))
    return pl.pallas_call(
        functools.partial(_even_in_body, pool_dim=pool_dim, q_lora=q_lora, kv_lora=kv_lora),
        grid=(m // tm,),
        in_specs=[
            row(d), vec(d, layer),
            pl.BlockSpec((None, d, n_in), lambda i: (e, 0, 0), pipeline_mode=pl.Buffered(1)),
            vec(q_lora, e), vec(kv_lora, e),
            pl.BlockSpec((None, q_lora, n_q), lambda i: (e, 0, 0), pipeline_mode=pl.Buffered(1)),
            vec(QK_NOPE, e), vec(LANES, e), vec(LANES, e),
            row(LANES), row(LANES),
        ],
        out_specs=[
            row(pool_dim),
            pl.BlockSpec((MLA_HEADS, tm, 2 * LANES), lambda i: (0, i, 0)),
            row(kv_lora), row(LANES),
        ],
        out_shape=[
            jax.ShapeDtypeStruct((m, pool_dim), F32),
            jax.ShapeDtypeStruct((MLA_HEADS, m, 2 * LANES), BF16),
            jax.ShapeDtypeStruct((m, kv_lora), F32),
            jax.ShapeDtypeStruct((m, LANES), F32),
        ],
        compiler_params=_params("parallel"),
        name="even_in",
    )(x, g, win, qan, kvan, wqb, qnn, qpn, kpn, cos, sin)


def _kv_body(ckv_ref, kpe_ref, w_ref, g_ref, k_ref, v_ref):
    c = ckv_ref[...].astype(BF16)
    kp = kpe_ref[...]
    kp_par = (kp.astype(BF16), pltpu.roll(kp, QK_ROPE, 1).astype(BF16))
    hw = QK_NOPE + V_HEAD
    for hd in range(MLA_HEADS):
        kv = _dot(c, w_ref[:, hd * hw:(hd + 1) * hw])
        k_ref[hd, :, :QK_NOPE] = _rms(kv[:, :QK_NOPE], g_ref[...]).astype(BF16)
        k_ref[hd, :, QK_NOPE:] = kp_par[hd % 2]
        v_ref[hd] = kv[:, QK_NOPE:].astype(BF16)


def _kv_prompt(ckv, kpe, wkvb, knn, e, lp):
    kv_lora = ckv.shape[-1]
    tm = _tile(lp, 512)
    return pl.pallas_call(
        _kv_body,
        grid=(lp // tm,),
        in_specs=[
            pl.BlockSpec((tm, kv_lora), lambda i: (i, 0)),
            pl.BlockSpec((tm, LANES), lambda i: (i, 0)),
            pl.BlockSpec((None,) + wkvb.shape[1:], lambda i: (e, 0, 0), pipeline_mode=pl.Buffered(1)),
            pl.BlockSpec((None, 1, QK_NOPE), lambda i: (e, 0, 0)),
        ],
        out_specs=[
            pl.BlockSpec((MLA_HEADS, tm, 2 * LANES), lambda i: (0, i, 0)),
            pl.BlockSpec((MLA_HEADS, tm, V_HEAD), lambda i: (0, i, 0)),
        ],
        out_shape=[
            jax.ShapeDtypeStruct((MLA_HEADS, lp, 2 * LANES), BF16),
            jax.ShapeDtypeStruct((MLA_HEADS, lp, V_HEAD), BF16),
        ],
        compiler_params=_params("parallel"),
        name="kv_prompt",
    )(ckv, kpe, wkvb, knn)


def _flash_body(q_ref, k_ref, v_ref, o_ref, m_ref, l_ref, acc_ref, *, t):
    i = pl.program_id(1)
    q = q_ref[...]
    m_ref[...] = jnp.full(m_ref.shape, NEG_BIG, F32)
    l_ref[...] = jnp.zeros(l_ref.shape, F32)
    acc_ref[...] = jnp.zeros(acc_ref.shape, F32)

    def block(j, masked):
        start = pl.multiple_of(j * t, t)
        s = _dot_nt(q, k_ref[pl.ds(start, t), :])
        if masked:
            qc = lax.broadcasted_iota(jnp.int32, s.shape, 0) // CHUNK
            kc = lax.broadcasted_iota(jnp.int32, s.shape, 1) // CHUNK
            s = jnp.where(kc <= qc, s, NEG_BIG)
        m_old = m_ref[...]
        m_new = jnp.maximum(m_old, jnp.max(s, axis=-1, keepdims=True))
        alpha = jnp.exp(m_old - m_new)
        p = jnp.exp(s - m_new)
        l_ref[...] = alpha * l_ref[...] + jnp.sum(p, axis=-1, keepdims=True)
        acc_ref[...] = alpha * acc_ref[...] + _dot(p.astype(BF16), v_ref[pl.ds(start, t), :])
        m_ref[...] = m_new

    def body(j, carry):
        block(j, False)
        return carry

    lax.fori_loop(0, i, body, 0)
    block(i, True)
    o_ref[...] = (acc_ref[...] / l_ref[...]).astype(BF16)


def _attn_prompt(q, k, v, lp):
    t = _tile(lp, 512, CHUNK)
    return pl.pallas_call(
        functools.partial(_flash_body, t=t),
        grid=(MLA_HEADS, lp // t),
        in_specs=[
            pl.BlockSpec((None, t, 2 * LANES), lambda h, i: (h, i, 0)),
            pl.BlockSpec((None, lp, 2 * LANES), lambda h, i: (h, 0, 0)),
            pl.BlockSpec((None, lp, V_HEAD), lambda h, i: (h, 0, 0)),
        ],
        out_specs=pl.BlockSpec((t, V_HEAD), lambda h, i: (i, h)),
        out_shape=jax.ShapeDtypeStruct((lp, MLA_HEADS * V_HEAD), BF16),
        scratch_shapes=[pltpu.VMEM((t, 1), F32), pltpu.VMEM((t, 1), F32), pltpu.VMEM((t, V_HEAD), F32)],
        compiler_params=_params("parallel", "arbitrary"),
        name="attn_prompt",
    )(q, k, v)


def _attn_sample_body(q_ref, cc_ref, cn_ref, pc_ref, pn_ref, w_ref, g_ref, o_ref, c_sc, kp_sc, *, past, s, lk):
    c_sc[:past] = cc_ref[...].astype(BF16)
    c_sc[past:past + s] = cn_ref[...].astype(BF16)
    kp_sc[0, :past] = pc_ref[...].astype(BF16)
    kp_sc[0, past:past + s] = pn_ref[...].astype(BF16)
    kp_sc[1, :past] = pltpu.roll(pc_ref[...], QK_ROPE, 1).astype(BF16)
    kp_sc[1, past:past + s] = pltpu.roll(pn_ref[...], QK_ROPE, 1).astype(BF16)
    if lk > past + s:
        c_sc[past + s:] = jnp.zeros((lk - past - s, c_sc.shape[1]), BF16)
        kp_sc[0, past + s:] = jnp.zeros((lk - past - s, LANES), BF16)
        kp_sc[1, past + s:] = jnp.zeros((lk - past - s, LANES), BF16)
    c = c_sc[...]
    valid = lax.broadcasted_iota(jnp.int32, (s, lk), 1) < past + s
    hw = QK_NOPE + V_HEAD
    for hd in range(MLA_HEADS):
        kv = _dot(c, w_ref[:, hd * hw:(hd + 1) * hw])
        kn = _rms(kv[:, :QK_NOPE], g_ref[...]).astype(BF16)
        q = q_ref[hd]
        sc = _dot_nt(q[:, :QK_NOPE], kn) + _dot_nt(q[:, QK_NOPE:], kp_sc[hd % 2])
        sc = jnp.where(valid, sc, NEG_BIG)
        p = jnp.exp(sc - jnp.max(sc, axis=-1, keepdims=True))
        o = _dot(p.astype(BF16), kv[:, QK_NOPE:].astype(BF16)) / jnp.sum(p, axis=-1, keepdims=True)
        o_ref[:, hd * V_HEAD:(hd + 1) * V_HEAD] = o.astype(BF16)


def _attn_sample(q, ckv_cache, ckv, kpe_cache, kpe, wkvb, knn, e, lp, b, s):
    past, kv_lora = ckv_cache.shape[2], ckv_cache.shape[3]
    lk = past + -(-s // LANES) * LANES
    row0 = lp // s
    return pl.pallas_call(
        functools.partial(_attn_sample_body, past=past, s=s, lk=lk),
        grid=(b,),
        in_specs=[
            pl.BlockSpec((MLA_HEADS, s, 2 * LANES), lambda i: (0, row0 + i, 0)),
            pl.BlockSpec((None, None, past, kv_lora), lambda i: (e, i, 0, 0)),
            pl.BlockSpec((s, kv_lora), lambda i: (row0 + i, 0)),
            pl.BlockSpec((None, None, past, LANES), lambda i: (e, i, 0, 0)),
            pl.BlockSpec((s, LANES), lambda i: (row0 + i, 0)),
            pl.BlockSpec((None,) + wkvb.shape[1:], lambda i: (e, 0, 0), pipeline_mode=pl.Buffered(1)),
            pl.BlockSpec((None, 1, QK_NOPE), lambda i: (e, 0, 0)),
        ],
        out_specs=pl.BlockSpec((s, MLA_HEADS * V_HEAD), lambda i: (i, 0)),
        out_shape=jax.ShapeDtypeStruct((b * s, MLA_HEADS * V_HEAD), BF16),
        scratch_shapes=[pltpu.VMEM((lk, kv_lora), BF16), pltpu.VMEM((2, lk, LANES), BF16)],
        compiler_params=_params("parallel"),
        name="attn_sample",
    )(q, ckv_cache, ckv, kpe_cache, kpe, wkvb, knn)


def _pool_segment(ext_ref, pooled_ref, row0, n, pos0, gdim):
    pos = (lax.broadcasted_iota(jnp.int32, (n, 1), 0) + pos0).astype(F32)
    for gi, w in enumerate(POOL_WINDOWS):
        cols = slice(gi * gdim, (gi + 1) * gdim)
        u = ext_ref[HIST_ROWS:HIST_ROWS + n, cols]
        acc = u
        for k in range(1, w):
            acc = acc + ext_ref[HIST_ROWS - k:HIST_ROWS - k + n, cols]
        cnt = jnp.minimum(jnp.float32(w), pos + 1.0)
        pooled_ref[row0:row0 + n, cols] = (acc / cnt - u).astype(BF16)


def _pool_project(pooled_ref, pw_ref, sc_ref, o_ref, gdim):
    for gi in range(len(POOL_WINDOWS)):
        cols = slice(gi * gdim, (gi + 1) * gdim)
        o_ref[:, cols] = (_dot(pooled_ref[:, cols], pw_ref[gi]) * sc_ref[:, cols]).astype(BF16)


def _pool_prompt_body(u_ref, pw_ref, sc_ref, o_ref, ext_ref, pooled_ref, *, tm, gdim):
    i = pl.program_id(0)

    @pl.when(i == 0)
    def _():
        ext_ref[:HIST_ROWS] = jnp.zeros((HIST_ROWS, ext_ref.shape[1]), F32)

    ext_ref[HIST_ROWS:] = u_ref[...]
    _pool_segment(ext_ref, pooled_ref, 0, tm, i * tm, gdim)
    _pool_project(pooled_ref, pw_ref, sc_ref, o_ref, gdim)
    ext_ref[:HIST_ROWS] = ext_ref[tm:tm + HIST_ROWS]


def _pool_sample_body(u_ref, hist_ref, pw_ref, sc_ref, o_ref, ext_ref, pooled_ref, *, nb, s, past, gdim):
    for bi in range(nb):
        ext = ext_ref.at[bi]
        ext[0:1] = jnp.zeros((1, ext_ref.shape[2]), F32)
        ext[1:HIST_ROWS] = hist_ref[bi]
        ext[HIST_ROWS:] = u_ref[bi * s:(bi + 1) * s]
        _pool_segment(ext, pooled_ref, bi * s, s, past, gdim)
    _pool_project(pooled_ref, pw_ref, sc_ref, o_ref, gdim)


def _pool(u, state_pool, pw, scale, e, lp, b, s, past):
    m, pool_dim = u.shape
    ng = len(POOL_WINDOWS)
    gdim = pool_dim // ng
    w_spec = pl.BlockSpec((None, ng, gdim, gdim), lambda i: (e, 0, 0, 0))
    s_spec = pl.BlockSpec((None, 1, pool_dim), lambda i: (e, 0, 0))
    tm = _tile(lp, 512, HIST_ROWS)
    out_p = pl.pallas_call(
        functools.partial(_pool_prompt_body, tm=tm, gdim=gdim),
        grid=(lp // tm,),
        in_specs=[pl.BlockSpec((tm, pool_dim), lambda i: (i, 0)), w_spec, s_spec],
        out_specs=pl.BlockSpec((tm, pool_dim), lambda i: (i, 0)),
        out_shape=jax.ShapeDtypeStruct((lp, pool_dim), BF16),
        scratch_shapes=[pltpu.VMEM((HIST_ROWS + tm, pool_dim), F32), pltpu.VMEM((tm, pool_dim), BF16)],
        compiler_params=_params("arbitrary"),
        name="pool_prompt",
    )(u, pw, scale)
    nb = math.gcd(b, max(1, 256 // s))
    row0 = lp // (nb * s)
    out_s = pl.pallas_call(
        functools.partial(_pool_sample_body, nb=nb, s=s, past=past, gdim=gdim),
        grid=(b // nb,),
        in_specs=[
            pl.BlockSpec((nb * s, pool_dim), lambda i: (row0 + i, 0)),
            pl.BlockSpec((None, nb, POOL_HIST, pool_dim), lambda i: (e, i, 0, 0)),
            w_spec, s_spec,
        ],
        out_specs=pl.BlockSpec((nb * s, pool_dim), lambda i: (i, 0)),
        out_shape=jax.ShapeDtypeStruct((b * s, pool_dim), BF16),
        scratch_shapes=[pltpu.VMEM((nb, HIST_ROWS + s, pool_dim), F32), pltpu.VMEM((nb * s, pool_dim), BF16)],
        compiler_params=_params("parallel"),
        name="pool_sample",
    )(u, state_pool, pw, scale)
    return out_p, out_s


def _even_out_body(x_ref, p_ref, a_ref, w_ref, o_ref, *, pool_dim):
    o_ref[...] = x_ref[...] + _dot(p_ref[...], w_ref[:pool_dim]) + _dot(a_ref[...], w_ref[pool_dim:])


def _even_out(x, pool_out, attn, wout, e):
    m, d = x.shape
    pool_dim, attn_dim = pool_out.shape[1], attn.shape[1]
    tm = _tile(m, 512)
    return pl.pallas_call(
        functools.partial(_even_out_body, pool_dim=pool_dim),
        grid=(m // tm,),
        in_specs=[
            pl.BlockSpec((tm, d), lambda i: (i, 0)),
            pl.BlockSpec((tm, pool_dim), lambda i: (i, 0)),
            pl.BlockSpec((tm, attn_dim), lambda i: (i, 0)),
            pl.BlockSpec((None,) + wout.shape[1:], lambda i: (e, 0, 0), pipeline_mode=pl.Buffered(1)),
        ],
        out_specs=pl.BlockSpec((tm, d), lambda i: (i, 0)),
        out_shape=jax.ShapeDtypeStruct((m, d), F32),
        compiler_params=_params("parallel"),
        name="even_out",
    )(x, pool_out, attn, wout)


def _odd_in_body(x_ref, g_ref, w_ref, vn_ref, u_ref, v_ref, vf_ref, *, gate):
    h = _rms(x_ref[...], g_ref[...]).astype(BF16)
    u_ref[...] = jax.nn.gelu(_dot(h, w_ref[:, :gate])).astype(BF16)
    v = _rms(jax.nn.gelu(_dot(h, w_ref[:, gate:])), vn_ref[...])
    v_ref[...] = v.astype(BF16)
    vf_ref[...] = v


def _odd_in(x, g, layer, win, vn, o, lp):
    m, d = x.shape
    gate = win.shape[-1] // 2
    tm = _tile(math.gcd(lp, m - lp), 256)
    first = lp // tm
    return pl.pallas_call(
        functools.partial(_odd_in_body, gate=gate),
        grid=(m // tm,),
        in_specs=[
            pl.BlockSpec((tm, d), lambda i: (i, 0)),
            pl.BlockSpec((None, 1, d), lambda i: (layer, 0, 0)),
            pl.BlockSpec((None, d, 2 * gate), lambda i: (o, 0, 0), pipeline_mode=pl.Buffered(1)),
            pl.BlockSpec((None, 1, gate), lambda i: (o, 0, 0)),
        ],
        out_specs=[
            pl.BlockSpec((tm, gate), lambda i: (i, 0)),
            pl.BlockSpec((tm, gate), lambda i: (i, 0)),
            pl.BlockSpec((tm, gate), lambda i: (jnp.maximum(i - first, 0), 0)),
        ],
        out_shape=[
            jax.ShapeDtypeStruct((m, gate), BF16),
            jax.ShapeDtypeStruct((m, gate), BF16),
            jax.ShapeDtypeStruct((m - lp, gate), F32),
        ],
        compiler_params=_params("arbitrary"),
        name="odd_in",
    )(x, g, win, vn)


def _odd_out_body(x_ref, u_ref, v_ref, ws_ref, b_ref, w_ref, o_ref, us_ref, *, tm, n_prompt_tiles, s):
    is_prompt = pl.program_id(0) < n_prompt_tiles
    ii = lax.broadcasted_iota(jnp.int32, (GMLP_CHUNK, GMLP_CHUNK), 0)
    jj = lax.broadcasted_iota(jnp.int32, (GMLP_CHUNK, GMLP_CHUNK), 1)
    same_stream = jnp.where((ii // s) == (jj // s), 1, 0) + jnp.where(is_prompt, 1, 0)
    keep = (jj <= ii) & (same_stream > 0)
    gdim = u_ref.shape[1] // GMLP_GROUPS
    for gi in range(GMLP_GROUPS):
        cols = slice(gi * gdim, (gi + 1) * gdim)
        wmat = jnp.where(keep, ws_ref[gi], 0.0).astype(BF16)
        bias = b_ref[:, gi:gi + 1]
        for ci in range(tm // GMLP_CHUNK):
            rows = slice(ci * GMLP_CHUNK, (ci + 1) * GMLP_CHUNK)
            sg = _dot(wmat, v_ref[rows, cols]) + bias
            us_ref[rows, cols] = (u_ref[rows, cols].astype(F32) * sg).astype(BF16)
    o_ref[...] = x_ref[...] + _dot(us_ref[...], w_ref[...])


def _odd_out(x, u, v, ws_sel, b_sel, wout, o, lp, s):
    m, d = x.shape
    gate = u.shape[1]
    tm = _tile(math.gcd(lp, m - lp), 512, GMLP_CHUNK)
    npt = lp // tm
    sel = lambda i: jnp.where(i >= npt, 1, 0)
    return pl.pallas_call(
        functools.partial(_odd_out_body, tm=tm, n_prompt_tiles=npt, s=s),
        grid=(m // tm,),
        in_specs=[
            pl.BlockSpec((tm, d), lambda i: (i, 0)),
            pl.BlockSpec((tm, gate), lambda i: (i, 0)),
            pl.BlockSpec((tm, gate), lambda i: (i, 0)),
            pl.BlockSpec((None, None, GMLP_GROUPS, GMLP_CHUNK, GMLP_CHUNK), lambda i: (sel(i), o, 0, 0, 0)),
            pl.BlockSpec((None, None, GMLP_CHUNK, GMLP_GROUPS), lambda i: (sel(i), o, 0, 0)),
            pl.BlockSpec((None,) + wout.shape[1:], lambda i: (o, 0, 0), pipeline_mode=pl.Buffered(1)),
        ],
        out_specs=pl.BlockSpec((tm, d), lambda i: (i, 0)),
        out_shape=jax.ShapeDtypeStruct((m, d), F32),
        scratch_shapes=[pltpu.VMEM((tm, gate), BF16)],
        compiler_params=_params("parallel"),
        name="odd_out",
    )(x, u, v, ws_sel, b_sel, wout)


def kernel(x_prompt, x_sample, cache_mla_ckv, cache_mla_kpe, state_pool, norm_ffn1, norm_mix, norm_ffn2, ffn1_w_gate, ffn1_w_up, ffn1_w_down, ffn2_w_gate, ffn2_w_up, ffn2_w_down, ev_w_in, ev_q_a_norm, ev_kv_a_norm, ev_w_qb, ev_w_kvb, ev_q_nope_norm, ev_q_pe_norm, ev_k_nope_norm, ev_k_pe_norm, ev_pool_w, ev_pool_scale, ev_w_out, od_w_in, od_v_norm, od_w_s, od_b_s, od_w_out):
    bp, lp, d = x_prompt.shape
    b, s, _ = x_sample.shape
    depth = norm_mix.shape[0]
    n_even, n_odd = ev_w_in.shape[0], od_w_in.shape[0]
    past = cache_mla_ckv.shape[2]
    q_lora, kv_lora = ev_q_a_norm.shape[1], ev_kv_a_norm.shape[1]
    pool_dim = ev_pool_scale.shape[1]
    gate = od_v_norm.shape[1]
    m = lp + b * s
    assert bp == 1 and lp % GMLP_CHUNK == 0 and (b * s) % GMLP_CHUNK == 0 and GMLP_CHUNK % s == 0
    assert past % CHUNK == 0 and s <= CHUNK, "every cached and new key must be visible to every sample query"
    assert s >= POOL_HIST and lp >= POOL_HIST and lp % s == 0
    assert ev_w_in.shape[2] == pool_dim + q_lora + kv_lora + QK_ROPE

    vec = lambda a: a[:, None, :]
    bf = lambda a: a.astype(BF16)
    pad_lanes = lambda a: jnp.pad(a, [(0, 0)] * (a.ndim - 1) + [(0, (-a.shape[-1]) % LANES)])

    w_ffn = [tuple(bf(w) for w in ws) for ws in ((ffn1_w_gate, ffn1_w_up, ffn1_w_down),
                                                  (ffn2_w_gate, ffn2_w_up, ffn2_w_down))]
    g_ffn = (vec(norm_ffn1), vec(norm_ffn2))
    g_mix = vec(norm_mix)
    w_in_e = bf(pad_lanes(ev_w_in))
    wq = ev_w_qb.reshape(n_even, q_lora, MLA_HEADS, QK_HEAD)
    w_qb = bf(jnp.concatenate([wq[..., :QK_NOPE].reshape(n_even, q_lora, -1),
                               wq[..., QK_NOPE:].reshape(n_even, q_lora, -1)], axis=-1))
    w_kvb, w_out_e, pool_w = bf(ev_w_kvb), bf(ev_w_out), bf(ev_pool_w)
    qpn = vec(jnp.tile(ev_q_pe_norm, (1, LANES // QK_ROPE)))
    kpn = vec(pad_lanes(ev_k_pe_norm))
    kpe_cache = pad_lanes(cache_mla_kpe)
    w_in_o, w_out_o = bf(od_w_in), bf(od_w_out)
    rep = GMLP_CHUNK // s
    ws_sel = jnp.stack([od_w_s, jnp.tile(od_w_s[:, :, :s, :s], (1, 1, rep, rep))])
    b_sel = jnp.stack([od_b_s, jnp.tile(od_b_s[:, :, :s], (1, 1, rep))]).transpose(0, 1, 3, 2)

    cos, sin = _rope_tables(m, lp, past, s)
    x = jnp.concatenate([x_prompt.reshape(lp, d), x_sample.reshape(b * s, d)], axis=0)
    new_ckv, new_kpe, new_u, new_v = [], [], [], []
    for layer in range(depth):
        x = _ffn(x, g_ffn[0], *w_ffn[0], layer)
        if layer % 2 == 0:
            e = layer // 2
            u, q, ckv, kpe = _even_in(x, g_mix, layer, w_in_e, vec(ev_q_a_norm), vec(ev_kv_a_norm), w_qb,
                                      vec(ev_q_nope_norm), qpn, kpn, cos, sin, e, pool_dim, q_lora, kv_lora)
            k_p, v_p = _kv_prompt(ckv, kpe, w_kvb, vec(ev_k_nope_norm), e, lp)
            attn_p = _attn_prompt(q, k_p, v_p, lp)
            attn_s = _attn_sample(q, cache_mla_ckv, ckv, kpe_cache, kpe, w_kvb, vec(ev_k_nope_norm), e, lp, b, s)
            pool_p, pool_s = _pool(u, state_pool, pool_w, vec(ev_pool_scale), e, lp, b, s, past)
            x = _even_out(x, jnp.concatenate([pool_p, pool_s]), jnp.concatenate([attn_p, attn_s]), w_out_e, e)
            new_ckv.append((ckv[:lp], ckv[lp:]))
            new_kpe.append((kpe[:lp, :QK_ROPE], kpe[lp:, :QK_ROPE]))
            new_u.append((u[lp - POOL_HIST:lp], u[lp:].reshape(b, s, pool_dim)[:, s - POOL_HIST:]))
        else:
            o = layer // 2
            u, v, v_f32 = _odd_in(x, g_mix, layer, w_in_o, vec(od_v_norm), o, lp)
            x = _odd_out(x, u, v, ws_sel, b_sel, w_out_o, o, lp, s)
            new_v.append(v_f32)
        x = _ffn(x, g_ffn[1], *w_ffn[1], layer)

    stack = lambda pairs, k, shape: jnp.stack([p[k] for p in pairs]).reshape((n_even,) + shape)
    return (
        x[:lp].reshape(1, lp, d),
        x[lp:].reshape(b, s, d),
        stack(new_ckv, 0, (1, lp, kv_lora)),
        stack(new_kpe, 0, (1, lp, QK_ROPE)),
        stack(new_u, 0, (1, POOL_HIST, pool_dim)),
        stack(new_ckv, 1, (b, s, kv_lora)),
        stack(new_kpe, 1, (b, s, QK_ROPE)),
        stack(new_u, 1, (b, POOL_HIST, pool_dim)),
        jnp.stack(new_v).reshape(n_odd, b, s, gate),
    )
```

```python
import functools
import math

import jax
import jax.numpy as jnp
from jax import lax
from jax.experimental import pallas as pl
from jax.experimental.pallas import tpu as pltpu

F32 = jnp.float32
BF16 = jnp.bfloat16

EPS = 1e-6
CHUNK = 64
POOL_WINDOWS = (2, 4, 8, 16)
POOL_HIST = max(POOL_WINDOWS) - 1
HIST_ROWS = 16
MLA_HEADS = 8
QK_NOPE = 128
QK_ROPE = 64
V_HEAD = 128
QK_HEAD = QK_NOPE + QK_ROPE
ATTN_SCALE = QK_HEAD ** -0.5
Q_SCALE = ATTN_SCALE * math.log2(math.e)
ROPE_THETA = 10000.0
GMLP_CHUNK = 128
GMLP_GROUPS = 8
LANES = 128
NEG_BIG = -1e30
VMEM_LIMIT = 56 * 1024 * 1024


def _params(*sem):
    return pltpu.CompilerParams(dimension_semantics=sem, vmem_limit_bytes=VMEM_LIMIT)


def _tile(n, pref, mult=8):
    if n <= pref:
        return n
    for t in range(pref - pref % mult, 0, -mult):
        if n % t == 0:
            return t
    raise ValueError(f"no tile for {n}")


def _const_spec(shape):
    nd = len(shape)
    return pl.BlockSpec(shape, lambda *_: (0,) * nd, pipeline_mode=pl.Buffered(1))


def _rms(x, g):
    ms = jnp.mean(x * x, axis=-1, keepdims=True)
    return x * lax.rsqrt(ms + EPS) * g


def _dot(a, b):
    return jnp.dot(a, b, preferred_element_type=F32)


def _dot_nt(a, b):
    return lax.dot_general(a, b, (((1,), (1,)), ((), ())), preferred_element_type=F32)


def _ffn_body(x_ref, g_ref, wg_ref, wu_ref, wd_ref, o_ref, h_ref):
    @pl.when(pl.program_id(1) == 0)
    def _():
        x = x_ref[...]
        h_ref[...] = _rms(x, g_ref[...]).astype(BF16)
        o_ref[...] = x

    h = h_ref[...]
    a = _dot(h, wg_ref[...])
    b = _dot(h, wu_ref[...])
    act = (a * jax.nn.sigmoid(a) * b).astype(BF16)
    o_ref[...] += 0.5 * _dot(act, wd_ref[...])


def _ffn(x, g, wg, wu, wd, layer):
    m, d = x.shape
    f = wg.shape[-1]
    tm, tf = _tile(m, 1024), _tile(f, 512, LANES)
    return pl.pallas_call(
        _ffn_body,
        grid=(m // tm, f // tf),
        in_specs=[
            pl.BlockSpec((tm, d), lambda i, j: (i, 0)),
            pl.BlockSpec((None, 1, d), lambda i, j: (layer, 0, 0)),
            pl.BlockSpec((None, d, tf), lambda i, j: (layer, 0, j)),
            pl.BlockSpec((None, d, tf), lambda i, j: (layer, 0, j)),
            pl.BlockSpec((None, tf, d), lambda i, j: (layer, j, 0)),
        ],
        out_specs=pl.BlockSpec((tm, d), lambda i, j: (i, 0)),
        out_shape=jax.ShapeDtypeStruct((m, d), F32),
        scratch_shapes=[pltpu.VMEM((tm, d), BF16)],
        compiler_params=_params("parallel", "arbitrary"),
        name="ffn",
    )(x, g, wg, wu, wd)


def _rope_tab_body(inv_ref, sgn_ref, cos_ref, sin_ref, *, tm, lp, past, s):
    r = lax.broadcasted_iota(jnp.int32, (tm, LANES), 0) + pl.program_id(0) * tm
    t = r - lp
    t = (t & (s - 1)) if s & (s - 1) == 0 else lax.rem(t, s)
    pos = jnp.where(r >= lp, past + t, r).astype(F32)
    ang = pos * inv_ref[...]
    cos_ref[...] = jnp.cos(ang)
    sin_ref[...] = jnp.sin(ang) * sgn_ref[...]


def _rope_tables(m, lp, past, s):
    half = QK_ROPE // 2
    inv = ROPE_THETA ** (-jnp.arange(0, QK_ROPE, 2, dtype=F32) / QK_ROPE)
    inv = jnp.tile(inv, LANES // half)[None, :]
    sgn = jnp.tile(jnp.concatenate([-jnp.ones((half,), F32), jnp.ones((half,), F32)]), LANES // QK_ROPE)[None, :]
    tm = _tile(m, 512)
    return pl.pallas_call(
        functools.partial(_rope_tab_body, tm=tm, lp=lp, past=past, s=s),
        grid=(m // tm,),
        in_specs=[_const_spec((1, LANES)), _const_spec((1, LANES))],
        out_specs=[pl.BlockSpec((tm, LANES), lambda i: (i, 0))] * 2,
        out_shape=[jax.ShapeDtypeStruct((m, LANES), F32)] * 2,
        compiler_params=_params("parallel"),
        name="rope_tables",
    )(inv, sgn)


def _swap_halves(x):
    lane = lax.broadcasted_iota(jnp.int32, x.shape, 1)
    left = pltpu.roll(x, LANES - QK_ROPE // 2, 1)
    right = pltpu.roll(x, QK_ROPE // 2, 1)
    return jnp.where((lane & (QK_ROPE - 1)) < QK_ROPE // 2, left, right)


def _even_in_body(x_ref, g_ref, win_ref, qan_ref, kvan_ref, wqb_ref, qnn_ref, knn_ref, qpn_ref, kpn_ref,
                  cos_ref, sin_ref, u_ref, q_ref, ckv_ref, kpe_ref, *, pool_dim, q_lora, kv_lora):
    h = _rms(x_ref[...], g_ref[...]).astype(BF16)
    z = _dot(h, win_ref[...])
    u_ref[...] = z[:, :pool_dim]
    o1, o2 = pool_dim + q_lora, pool_dim + q_lora + kv_lora
    ckv_ref[...] = _rms(z[:, o1:o2], kvan_ref[...])
    cos, sin = cos_ref[...], sin_ref[...]
    lane = lax.broadcasted_iota(jnp.int32, cos.shape, 1)
    lo = lane < QK_ROPE

    def rope(t):
        return t * cos + _swap_halves(t) * sin

    kp = z[:, o2:o2 + LANES]
    kp = kp * lax.rsqrt(jnp.sum(kp * kp, axis=-1, keepdims=True) / QK_ROPE + EPS) * kpn_ref[...]
    kpe_ref[...] = rope(kp)

    qn = _rms(z[:, pool_dim:o1], qan_ref[...]).astype(BF16)
    q = _dot(qn, wqb_ref[...])
    nope_w = MLA_HEADS * QK_NOPE
    k_gain = knn_ref[...] * Q_SCALE
    for hd in range(MLA_HEADS):
        qh = _rms(q[:, hd * QK_NOPE:(hd + 1) * QK_NOPE], qnn_ref[...]) * k_gain
        q_ref[hd, :, :QK_NOPE] = qh.astype(BF16)
    for j in range(MLA_HEADS // 2):
        t = q[:, nope_w + j * LANES:nope_w + (j + 1) * LANES]
        tt = t * t
        s_lo = jnp.sum(jnp.where(lo, tt, 0.0), axis=-1, keepdims=True)
        s_hi = jnp.sum(jnp.where(lo, 0.0, tt), axis=-1, keepdims=True)
        inv = jnp.where(lo, lax.rsqrt(s_lo / QK_ROPE + EPS), lax.rsqrt(s_hi / QK_ROPE + EPS))
        t = (rope(t * inv * qpn_ref[...]) * Q_SCALE).astype(BF16)
        q_ref[2 * j, :, QK_NOPE:] = t
        q_ref[2 * j + 1, :, QK_NOPE:] = t


def _even_in(x, g, layer, win, qan, kvan, wqb, qnn, knn, qpn, kpn, cos, sin, e, pool_dim, q_lora, kv_lora):
    m, d = x.shape
    tm = _tile(m, 256)
    n_in = win.shape[-1]
    n_q = wqb.shape[-1]
    row = lambda w: pl.BlockSpec((tm, w), lambda i: (i, 0))
    vec = lambda w, idx: pl.BlockSpec((None, 1, w), lambda i: (idx, 0, 0))
    return pl.pallas_call(
        functools.partial(_even_in_body, pool_dim=pool_dim, q_lora=q_lora, kv_lora=kv_lora),
        grid=(m // tm,),
        in_specs=[
            row(d), vec(d, layer),
            pl.BlockSpec((None, d, n_in), lambda i: (e, 0, 0), pipeline_mode=pl.Buffered(1)),
            vec(q_lora, e), vec(kv_lora, e),
            pl.BlockSpec((None, q_lora, n_q), lambda i: (e, 0, 0), pipeline_mode=pl.Buffered(1)),
            vec(QK_NOPE, e), vec(QK_NOPE, e), vec(LANES, e), vec(LANES, e),
            row(LANES), row(LANES),
        ],
        out_specs=[
            row(pool_dim),
            pl.BlockSpec((MLA_HEADS, tm, 2 * LANES), lambda i: (0, i, 0)),
            row(kv_lora), row(LANES),
        ],
        out_shape=[
            jax.ShapeDtypeStruct((m, pool_dim), F32),
            jax.ShapeDtypeStruct((MLA_HEADS, m, 2 * LANES), BF16),
            jax.ShapeDtypeStruct((m, kv_lora), F32),
            jax.ShapeDtypeStruct((m, LANES), F32),
        ],
        compiler_params=_params("parallel"),
        name="even_in",
    )(x, g, win, qan, kvan, wqb, qnn, knn, qpn, kpn, cos, sin)


def _kv_body(ckv_ref, kpe_ref, w_ref, k_ref, v_ref):
    c = ckv_ref[...].astype(BF16)
    kp = kpe_ref[...]
    kp_par = (kp.astype(BF16), pltpu.roll(kp, QK_ROPE, 1).astype(BF16))
    hw = QK_NOPE + V_HEAD
    for hd in range(MLA_HEADS):
        kv = _dot(c, w_ref[:, hd * hw:(hd + 1) * hw])
        k_ref[hd, :, :QK_NOPE] = _rms(kv[:, :QK_NOPE], 1.0).astype(BF16)
        k_ref[hd, :, QK_NOPE:] = kp_par[hd % 2]
        v_ref[hd, :, :V_HEAD] = kv[:, QK_NOPE:].astype(BF16)
        v_ref[hd, :, V_HEAD:] = jnp.ones((c.shape[0], LANES), BF16)


def _kv_prompt(ckv, kpe, wkvb, e, lp):
    kv_lora = ckv.shape[-1]
    tm = _tile(lp, 512)
    return pl.pallas_call(
        _kv_body,
        grid=(lp // tm,),
        in_specs=[
            pl.BlockSpec((tm, kv_lora), lambda i: (i, 0)),
            pl.BlockSpec((tm, LANES), lambda i: (i, 0)),
            pl.BlockSpec((None,) + wkvb.shape[1:], lambda i: (e, 0, 0), pipeline_mode=pl.Buffered(1)),
        ],
        out_specs=[
            pl.BlockSpec((MLA_HEADS, tm, 2 * LANES), lambda i: (0, i, 0)),
            pl.BlockSpec((MLA_HEADS, tm, V_HEAD + LANES), lambda i: (0, i, 0)),
        ],
        out_shape=[
            jax.ShapeDtypeStruct((MLA_HEADS, lp, 2 * LANES), BF16),
            jax.ShapeDtypeStruct((MLA_HEADS, lp, V_HEAD + LANES), BF16),
        ],
        compiler_params=_params("parallel"),
        name="kv_prompt",
    )(ckv, kpe, wkvb)


def _flash_body(q_ref, k_ref, v_ref, o_ref, m_ref, acc_ref, *, t):
    i = pl.program_id(1)
    q = q_ref[...]
    m_ref[...] = jnp.full(m_ref.shape, NEG_BIG, F32)
    acc_ref[...] = jnp.zeros(acc_ref.shape, F32)

    def scores(j):
        return _dot_nt(q, k_ref[pl.ds(pl.multiple_of(j * t, t), t), :])

    def update(s, j):
        m_old = m_ref[...]
        m_new = jnp.maximum(m_old, jnp.max(s, axis=-1, keepdims=True))
        p = jnp.exp2(s - jnp.tile(m_new, (1, t // LANES)))
        alpha = jnp.tile(jnp.exp2(m_old - m_new), (1, acc_ref.shape[1] // LANES))
        pv = _dot(p.astype(BF16), v_ref[pl.ds(pl.multiple_of(j * t, t), t), :])
        acc_ref[...] = alpha * acc_ref[...] + pv
        m_ref[...] = m_new

    def body(j, s_cur):
        s_next = scores(j + 1)
        update(s_cur, j)
        return s_next

    s = lax.fori_loop(0, i, body, scores(0))
    qc = lax.broadcasted_iota(jnp.int32, s.shape, 0) // CHUNK
    kc = lax.broadcasted_iota(jnp.int32, s.shape, 1) // CHUNK
    update(jnp.where(kc <= qc, s, NEG_BIG), i)
    acc = acc_ref[...]
    o_ref[...] = (acc[:, :V_HEAD] / acc[:, V_HEAD:]).astype(BF16)


def _attn_prompt(q, k, v, lp):
    t = _tile(lp, 512, CHUNK)
    return pl.pallas_call(
        functools.partial(_flash_body, t=t),
        grid=(MLA_HEADS, lp // t),
        in_specs=[
            pl.BlockSpec((None, t, 2 * LANES), lambda h, i: (h, i, 0)),
            pl.BlockSpec((None, lp, 2 * LANES), lambda h, i: (h, 0, 0)),
            pl.BlockSpec((None, lp, V_HEAD + LANES), lambda h, i: (h, 0, 0)),
        ],
        out_specs=pl.BlockSpec((t, V_HEAD), lambda h, i: (i, h)),
        out_shape=jax.ShapeDtypeStruct((q.shape[1], MLA_HEADS * V_HEAD), BF16),
        scratch_shapes=[pltpu.VMEM((t, LANES), F32), pltpu.VMEM((t, V_HEAD + LANES), F32)],
        compiler_params=_params("parallel", "arbitrary"),
        name="attn_prompt",
    )(q, k, v)


def _attn_sample_body(q_ref, cc_ref, cn_ref, pc_ref, pn_ref, w_ref, _, o_ref, c_sc, kp_sc, *, past, s, lk):
    c_sc[:past] = cc_ref[...].astype(BF16)
    c_sc[past:past + s] = cn_ref[...].astype(BF16)
    kp_sc[0, :past] = pc_ref[...].astype(BF16)
    kp_sc[0, past:past + s] = pn_ref[...].astype(BF16)
    kp_sc[1, :past] = pltpu.roll(pc_ref[...], QK_ROPE, 1).astype(BF16)
    kp_sc[1, past:past + s] = pltpu.roll(pn_ref[...], QK_ROPE, 1).astype(BF16)
    if lk > past + s:
        c_sc[past + s:] = jnp.zeros((lk - past - s, c_sc.shape[1]), BF16)
        kp_sc[0, past + s:] = jnp.zeros((lk - past - s, LANES), BF16)
        kp_sc[1, past + s:] = jnp.zeros((lk - past - s, LANES), BF16)
    c = c_sc[...]
    valid = lax.broadcasted_iota(jnp.int32, (s, lk), 1) < past + s
    hw = QK_NOPE + V_HEAD
    ones = jnp.ones((8, QK_NOPE), BF16)
    n_ch = 4 if lk % 64 == 0 else 1
    ch = lk // n_ch
    for hd in range(MLA_HEADS):
        kv = jnp.concatenate([_dot(c[r * ch:(r + 1) * ch], w_ref[:, hd * hw:(hd + 1) * hw]) for r in range(n_ch)])
        kn = kv[:, :QK_NOPE]
        ms = _dot_nt(ones, (kn * kn).astype(BF16))[0:1] / QK_NOPE
        q = q_ref[hd]
        sc = _dot_nt(q[:, :QK_NOPE], kn.astype(BF16)) * lax.rsqrt(ms + EPS) + _dot_nt(q[:, QK_NOPE:], kp_sc[hd % 2])
        sc = jnp.where(valid, sc, NEG_BIG)
        p = jnp.exp2(sc - jnp.max(sc, axis=-1, keepdims=True))
        o = _dot(p.astype(BF16), kv[:, QK_NOPE:].astype(BF16)) / jnp.sum(p, axis=-1, keepdims=True)
        o_ref[:, hd * V_HEAD:(hd + 1) * V_HEAD] = o.astype(BF16)


def _attn_sample(q, ckv_cache, ckv, kpe_cache, kpe, wkvb, attn, e, lp, b, s):
    past, kv_lora = ckv_cache.shape[2], ckv_cache.shape[3]
    lk = past + -(-s // LANES) * LANES
    row0 = lp // s
    return pl.pallas_call(
        functools.partial(_attn_sample_body, past=past, s=s, lk=lk),
        grid=(b,),
        in_specs=[
            pl.BlockSpec((MLA_HEADS, s, 2 * LANES), lambda i: (0, row0 + i, 0)),
            pl.BlockSpec((None, None, past, kv_lora), lambda i: (e, i, 0, 0)),
            pl.BlockSpec((s, kv_lora), lambda i: (row0 + i, 0)),
            pl.BlockSpec((None, None, past, LANES), lambda i: (e, i, 0, 0)),
            pl.BlockSpec((s, LANES), lambda i: (row0 + i, 0)),
            pl.BlockSpec((None,) + wkvb.shape[1:], lambda i: (e, 0, 0), pipeline_mode=pl.Buffered(1)),
            pl.BlockSpec(memory_space=pl.ANY),
        ],
        out_specs=pl.BlockSpec((s, MLA_HEADS * V_HEAD), lambda i: (row0 + i, 0)),
        out_shape=jax.ShapeDtypeStruct(attn.shape, attn.dtype),
        input_output_aliases={6: 0},
        scratch_shapes=[pltpu.VMEM((lk, kv_lora), BF16), pltpu.VMEM((2, lk, LANES), BF16)],
        compiler_params=_params("parallel"),
        name="attn_sample",
    )(q, ckv_cache, ckv, kpe_cache, kpe, wkvb, attn)


def _pool_segment(ext_ref, pooled_ref, row0, n, pos0, gdim):
    pos = (lax.broadcasted_iota(jnp.int32, (n, 1), 0) + pos0).astype(F32)
    for gi, w in enumerate(POOL_WINDOWS):
        cols = slice(gi * gdim, (gi + 1) * gdim)
        u = ext_ref[HIST_ROWS:HIST_ROWS + n, cols]
        acc = u
        for k in range(1, w):
            acc = acc + ext_ref[HIST_ROWS - k:HIST_ROWS - k + n, cols]
        cnt = jnp.minimum(jnp.float32(w), pos + 1.0)
        pooled_ref[row0:row0 + n, cols] = (acc / cnt - u).astype(BF16)


def _pool_project(pooled_ref, pw_ref, sc_ref, o_ref, gdim):
    for gi in range(len(POOL_WINDOWS)):
        cols = slice(gi * gdim, (gi + 1) * gdim)
        o_ref[:, cols] = (_dot(pooled_ref[:, cols], pw_ref[gi]) * sc_ref[:, cols]).astype(BF16)


def _pool_prompt_body(u_ref, pw_ref, sc_ref, o_ref, ext_ref, pooled_ref, *, tm, gdim):
    i = pl.program_id(0)

    @pl.when(i == 0)
    def _():
        ext_ref[:HIST_ROWS] = jnp.zeros((HIST_ROWS, ext_ref.shape[1]), F32)

    ext_ref[HIST_ROWS:] = u_ref[...]
    _pool_segment(ext_ref, pooled_ref, 0, tm, i * tm, gdim)
    _pool_project(pooled_ref, pw_ref, sc_ref, o_ref, gdim)
    ext_ref[:HIST_ROWS] = ext_ref[tm:tm + HIST_ROWS]


def _pool_sample_body(u_ref, hist_ref, pw_ref, sc_ref, _, o_ref, ext_ref, pooled_ref, *, nb, s, past, gdim):
    for bi in range(nb):
        ext = ext_ref.at[bi]
        ext[0:1] = jnp.zeros((1, ext_ref.shape[2]), F32)
        ext[1:HIST_ROWS] = hist_ref[bi]
        ext[HIST_ROWS:] = u_ref[bi * s:(bi + 1) * s]
        _pool_segment(ext, pooled_ref, bi * s, s, past, gdim)
    _pool_project(pooled_ref, pw_ref, sc_ref, o_ref, gdim)


def _pool(u, state_pool, pw, scale, e, lp, b, s, past):
    m, pool_dim = u.shape
    ng = len(POOL_WINDOWS)
    gdim = pool_dim // ng
    w_spec = pl.BlockSpec((None, ng, gdim, gdim), lambda i: (e, 0, 0, 0))
    s_spec = pl.BlockSpec((None, 1, pool_dim), lambda i: (e, 0, 0))
    tm = _tile(lp, 512, HIST_ROWS)
    out_p = pl.pallas_call(
        functools.partial(_pool_prompt_body, tm=tm, gdim=gdim),
        grid=(lp // tm,),
        in_specs=[pl.BlockSpec((tm, pool_dim), lambda i: (i, 0)), w_spec, s_spec],
        out_specs=pl.BlockSpec((tm, pool_dim), lambda i: (i, 0)),
        out_shape=jax.ShapeDtypeStruct((m, pool_dim), BF16),
        scratch_shapes=[pltpu.VMEM((HIST_ROWS + tm, pool_dim), F32), pltpu.VMEM((tm, pool_dim), BF16)],
        compiler_params=_params("arbitrary"),
        name="pool_prompt",
    )(u, pw, scale)
    nb = math.gcd(b, max(1, 256 // s))
    row0 = lp // (nb * s)
    return pl.pallas_call(
        functools.partial(_pool_sample_body, nb=nb, s=s, past=past, gdim=gdim),
        grid=(b // nb,),
        in_specs=[
            pl.BlockSpec((nb * s, pool_dim), lambda i: (row0 + i, 0)),
            pl.BlockSpec((None, nb, POOL_HIST, pool_dim), lambda i: (e, i, 0, 0)),
            w_spec, s_spec,
            pl.BlockSpec(memory_space=pl.ANY),
        ],
        out_specs=pl.BlockSpec((nb * s, pool_dim), lambda i: (row0 + i, 0)),
        out_shape=jax.ShapeDtypeStruct((m, pool_dim), BF16),
        input_output_aliases={4: 0},
        scratch_shapes=[pltpu.VMEM((nb, HIST_ROWS + s, pool_dim), F32), pltpu.VMEM((nb * s, pool_dim), BF16)],
        compiler_params=_params("parallel"),
        name="pool_sample",
    )(u, state_pool, pw, scale, out_p)


def _even_out_body(x_ref, p_ref, a_ref, w_ref, o_ref, *, pool_dim):
    o_ref[...] = x_ref[...] + _dot(p_ref[...], w_ref[:pool_dim]) + _dot(a_ref[...], w_ref[pool_dim:])


def _even_out(x, pool_out, attn, wout, e):
    m, d = x.shape
    pool_dim, attn_dim = pool_out.shape[1], attn.shape[1]
    tm = _tile(m, 512)
    return pl.pallas_call(
        functools.partial(_even_out_body, pool_dim=pool_dim),
        grid=(m // tm,),
        in_specs=[
            pl.BlockSpec((tm, d), lambda i: (i, 0)),
            pl.BlockSpec((tm, pool_dim), lambda i: (i, 0)),
            pl.BlockSpec((tm, attn_dim), lambda i: (i, 0)),
            pl.BlockSpec((None,) + wout.shape[1:], lambda i: (e, 0, 0), pipeline_mode=pl.Buffered(1)),
        ],
        out_specs=pl.BlockSpec((tm, d), lambda i: (i, 0)),
        out_shape=jax.ShapeDtypeStruct((m, d), F32),
        compiler_params=_params("parallel"),
        name="even_out",
    )(x, pool_out, attn, wout)


def _odd_in_body(x_ref, g_ref, w_ref, vn_ref, u_ref, v_ref, vf_ref, *, gate):
    h = _rms(x_ref[...], g_ref[...]).astype(BF16)
    u_ref[...] = jax.nn.gelu(_dot(h, w_ref[:, :gate])).astype(BF16)
    v = _rms(jax.nn.gelu(_dot(h, w_ref[:, gate:])), vn_ref[...])
    v_ref[...] = v.astype(BF16)
    vf_ref[...] = v


def _odd_in(x, g, layer, win, vn, o, lp):
    m, d = x.shape
    gate = win.shape[-1] // 2
    tm = _tile(math.gcd(lp, m - lp), 256)
    first = lp // tm
    return pl.pallas_call(
        functools.partial(_odd_in_body, gate=gate),
        grid=(m // tm,),
        in_specs=[
            pl.BlockSpec((tm, d), lambda i: (i, 0)),
            pl.BlockSpec((None, 1, d), lambda i: (layer, 0, 0)),
            pl.BlockSpec((None, d, 2 * gate), lambda i: (o, 0, 0), pipeline_mode=pl.Buffered(1)),
            pl.BlockSpec((None, 1, gate), lambda i: (o, 0, 0)),
        ],
        out_specs=[
            pl.BlockSpec((tm, gate), lambda i: (i, 0)),
            pl.BlockSpec((tm, gate), lambda i: (i, 0)),
            pl.BlockSpec((tm, gate), lambda i: (jnp.maximum(i - first, 0), 0)),
        ],
        out_shape=[
            jax.ShapeDtypeStruct((m, gate), BF16),
            jax.ShapeDtypeStruct((m, gate), BF16),
            jax.ShapeDtypeStruct((m - lp, gate), F32),
        ],
        compiler_params=_params("arbitrary"),
        name="odd_in",
    )(x, g, win, vn)


def _odd_out_body(x_ref, u_ref, v_ref, ws_ref, b_ref, w_ref, o_ref, us_ref, *, tm, n_prompt_tiles, s):
    is_prompt = pl.program_id(0) < n_prompt_tiles
    ii = lax.broadcasted_iota(jnp.int32, (GMLP_CHUNK, GMLP_CHUNK), 0)
    jj = lax.broadcasted_iota(jnp.int32, (GMLP_CHUNK, GMLP_CHUNK), 1)
    same_stream = jnp.where((ii // s) == (jj // s), 1, 0) + jnp.where(is_prompt, 1, 0)
    keep = (jj <= ii) & (same_stream > 0)
    gdim = u_ref.shape[1] // GMLP_GROUPS
    for gi in range(GMLP_GROUPS):
        cols = slice(gi * gdim, (gi + 1) * gdim)
        wmat = jnp.where(keep, ws_ref[gi], 0.0).astype(BF16)
        bias = b_ref[:, gi:gi + 1]
        for ci in range(tm // GMLP_CHUNK):
            rows = slice(ci * GMLP_CHUNK, (ci + 1) * GMLP_CHUNK)
            sg = _dot(wmat, v_ref[rows, cols]) + bias
            us_ref[rows, cols] = (u_ref[rows, cols].astype(F32) * sg).astype(BF16)
    o_ref[...] = x_ref[...] + _dot(us_ref[...], w_ref[...])


def _odd_out(x, u, v, ws_sel, b_sel, wout, o, lp, s):
    m, d = x.shape
    gate = u.shape[1]
    tm = _tile(math.gcd(lp, m - lp), 512, GMLP_CHUNK)
    npt = lp // tm
    sel = lambda i: jnp.where(i >= npt, 1, 0)
    return pl.pallas_call(
        functools.partial(_odd_out_body, tm=tm, n_prompt_tiles=npt, s=s),
        grid=(m // tm,),
        in_specs=[
            pl.BlockSpec((tm, d), lambda i: (i, 0)),
            pl.BlockSpec((tm, gate), lambda i: (i, 0)),
            pl.BlockSpec((tm, gate), lambda i: (i, 0)),
            pl.BlockSpec((None, None, GMLP_GROUPS, GMLP_CHUNK, GMLP_CHUNK), lambda i: (sel(i), o, 0, 0, 0)),
            pl.BlockSpec((None, None, GMLP_CHUNK, GMLP_GROUPS), lambda i: (sel(i), o, 0, 0)),
            pl.BlockSpec((None,) + wout.shape[1:], lambda i: (o, 0, 0), pipeline_mode=pl.Buffered(1)),
        ],
        out_specs=pl.BlockSpec((tm, d), lambda i: (i, 0)),
        out_shape=jax.ShapeDtypeStruct((m, d), F32),
        scratch_shapes=[pltpu.VMEM((tm, gate), BF16)],
        compiler_params=_params("parallel"),
        name="odd_out",
    )(x, u, v, ws_sel, b_sel, wout)


def kernel(x_prompt, x_sample, cache_mla_ckv, cache_mla_kpe, state_pool, norm_ffn1, norm_mix, norm_ffn2, ffn1_w_gate, ffn1_w_up, ffn1_w_down, ffn2_w_gate, ffn2_w_up, ffn2_w_down, ev_w_in, ev_q_a_norm, ev_kv_a_norm, ev_w_qb, ev_w_kvb, ev_q_nope_norm, ev_q_pe_norm, ev_k_nope_norm, ev_k_pe_norm, ev_pool_w, ev_pool_scale, ev_w_out, od_w_in, od_v_norm, od_w_s, od_b_s, od_w_out):
    bp, lp, d = x_prompt.shape
    b, s, _ = x_sample.shape
    depth = norm_mix.shape[0]
    n_even, n_odd = ev_w_in.shape[0], od_w_in.shape[0]
    past = cache_mla_ckv.shape[2]
    q_lora, kv_lora = ev_q_a_norm.shape[1], ev_kv_a_norm.shape[1]
    pool_dim = ev_pool_scale.shape[1]
    gate = od_v_norm.shape[1]
    m = lp + b * s
    assert bp == 1 and lp % GMLP_CHUNK == 0 and (b * s) % GMLP_CHUNK == 0 and GMLP_CHUNK % s == 0
    assert past % CHUNK == 0 and s <= CHUNK, "every cached and new key must be visible to every sample query"
    assert s >= POOL_HIST and lp >= POOL_HIST and lp % s == 0
    assert ev_w_in.shape[2] == pool_dim + q_lora + kv_lora + QK_ROPE

    vec = lambda a: a[:, None, :]
    bf = lambda a: a.astype(BF16)
    pad_lanes = lambda a: jnp.pad(a, [(0, 0)] * (a.ndim - 1) + [(0, (-a.shape[-1]) % LANES)])

    w_ffn = [tuple(bf(w) for w in ws) for ws in ((ffn1_w_gate, ffn1_w_up, ffn1_w_down),
                                                  (ffn2_w_gate, ffn2_w_up, ffn2_w_down))]
    g_ffn = (vec(norm_ffn1), vec(norm_ffn2))
    g_mix = vec(norm_mix)
    w_in_e = bf(pad_lanes(ev_w_in))
    wq = ev_w_qb.reshape(n_even, q_lora, MLA_HEADS, QK_HEAD)
    w_qb = bf(jnp.concatenate([wq[..., :QK_NOPE].reshape(n_even, q_lora, -1),
                               wq[..., QK_NOPE:].reshape(n_even, q_lora, -1)], axis=-1))
    w_kvb, w_out_e, pool_w = bf(ev_w_kvb), bf(ev_w_out), bf(ev_pool_w)
    qpn = vec(jnp.tile(ev_q_pe_norm, (1, LANES // QK_ROPE)))
    kpn = vec(pad_lanes(ev_k_pe_norm))
    kpe_cache = pad_lanes(cache_mla_kpe)
    w_in_o, w_out_o = bf(od_w_in), bf(od_w_out)
    rep = GMLP_CHUNK // s
    ws_sel = jnp.stack([od_w_s, jnp.tile(od_w_s[:, :, :s, :s], (1, 1, rep, rep))])
    b_sel = jnp.stack([od_b_s, jnp.tile(od_b_s[:, :, :s], (1, 1, rep))]).transpose(0, 1, 3, 2)

    cos, sin = _rope_tables(m, lp, past, s)
    x = jnp.concatenate([x_prompt.reshape(lp, d), x_sample.reshape(b * s, d)], axis=0)
    new_ckv, new_kpe, new_u, new_v = [], [], [], []
    for layer in range(depth):
        x = _ffn(x, g_ffn[0], *w_ffn[0], layer)
        if layer % 2 == 0:
            e = layer // 2
            u, q, ckv, kpe = _even_in(x, g_mix, layer, w_in_e, vec(ev_q_a_norm), vec(ev_kv_a_norm), w_qb,
                                      vec(ev_q_nope_norm), vec(ev_k_nope_norm), qpn, kpn, cos, sin, e,
                                      pool_dim, q_lora, kv_lora)
            k_p, v_p = _kv_prompt(ckv, kpe, w_kvb, e, lp)
            attn = _attn_prompt(q, k_p, v_p, lp)
            attn = _attn_sample(q, cache_mla_ckv, ckv, kpe_cache, kpe, w_kvb, attn, e, lp, b, s)
            pool_out = _pool(u, state_pool, pool_w, vec(ev_pool_scale), e, lp, b, s, past)
            x = _even_out(x, pool_out, attn, w_out_e, e)
            new_ckv.append((ckv[:lp], ckv[lp:]))
            new_kpe.append((kpe[:lp, :QK_ROPE], kpe[lp:, :QK_ROPE]))
            new_u.append((u[lp - POOL_HIST:lp], u[lp:].reshape(b, s, pool_dim)[:, s - POOL_HIST:]))
        else:
            o = layer // 2
            u, v, v_f32 = _odd_in(x, g_mix, layer, w_in_o, vec(od_v_norm), o, lp)
            x = _odd_out(x, u, v, ws_sel, b_sel, w_out_o, o, lp, s)
            new_v.append(v_f32)
        x = _ffn(x, g_ffn[1], *w_ffn[1], layer)

    stack = lambda pairs, k, shape: jnp.stack([p[k] for p in pairs]).reshape((n_even,) + shape)
    return (
        x[:lp].reshape(1, lp, d),
        x[lp:].reshape(b, s, d),
        stack(new_ckv, 0, (1, lp, kv_lora)),
        stack(new_kpe, 0, (1, lp, QK_ROPE)),
        stack(new_u, 0, (1, POOL_HIST, pool_dim)),
        stack(new_ckv, 1, (b, s, kv_lora)),
        stack(new_kpe, 1, (b, s, QK_ROPE)),
        stack(new_u, 1, (b, POOL_HIST, pool_dim)),
        jnp.stack(new_v).reshape(n_odd, b, s, gate),
    )
```

```python
import functools
import math

import jax
import jax.numpy as jnp
from jax import lax
from jax.experimental import pallas as pl
from jax.experimental.pallas import tpu as pltpu

F32 = jnp.float32
BF16 = jnp.bfloat16

EPS = 1e-6
CHUNK = 64
POOL_WINDOWS = (2, 4, 8, 16)
POOL_HIST = max(POOL_WINDOWS) - 1
HIST_ROWS = 16
MLA_HEADS = 8
QK_NOPE = 128
QK_ROPE = 64
V_HEAD = 128
QK_HEAD = QK_NOPE + QK_ROPE
ATTN_SCALE = QK_HEAD ** -0.5
Q_SCALE = ATTN_SCALE * math.log2(math.e)
ROPE_THETA = 10000.0
GMLP_CHUNK = 128
GMLP_GROUPS = 8
LANES = 128
NEG_BIG = -1e30
VMEM_LIMIT = 56 * 1024 * 1024


def _params(*sem):
    return pltpu.CompilerParams(dimension_semantics=sem, vmem_limit_bytes=VMEM_LIMIT)


def _tile(n, pref, mult=8):
    if n <= pref:
        return n
    for t in range(pref - pref % mult, 0, -mult):
        if n % t == 0:
            return t
    raise ValueError(f"no tile for {n}")


def _const_spec(shape):
    nd = len(shape)
    return pl.BlockSpec(shape, lambda *_: (0,) * nd, pipeline_mode=pl.Buffered(1))


def _rms(x, g):
    ms = jnp.mean(x * x, axis=-1, keepdims=True)
    return x * lax.rsqrt(ms + EPS) * g


def _dot(a, b):
    return jnp.dot(a, b, preferred_element_type=F32)


def _dot_nt(a, b):
    return lax.dot_general(a, b, (((1,), (1,)), ((), ())), preferred_element_type=F32)


def _ffn_body(x_ref, g_ref, wg_ref, wu_ref, wd_ref, o_ref, h_ref):
    @pl.when(pl.program_id(1) == 0)
    def _():
        x = x_ref[...]
        h_ref[...] = _rms(x, g_ref[...]).astype(BF16)
        o_ref[...] = x

    h = h_ref[...]
    a = _dot(h, wg_ref[...].astype(BF16))
    b = _dot(h, wu_ref[...].astype(BF16))
    act = (a * jax.nn.sigmoid(a) * b).astype(BF16)
    o_ref[...] += 0.5 * _dot(act, wd_ref[...].astype(BF16))


def _ffn(x, g, wg, wu, wd, layer):
    m, d = x.shape
    f = wg.shape[-1]
    tm, tf = _tile(m, 1024), _tile(f, 256, LANES)
    return pl.pallas_call(
        _ffn_body,
        grid=(m // tm, f // tf),
        in_specs=[
            pl.BlockSpec((tm, d), lambda i, j: (i, 0)),
            pl.BlockSpec((None, 1, d), lambda i, j: (layer, 0, 0)),
            pl.BlockSpec((None, d, tf), lambda i, j: (layer, 0, j)),
            pl.BlockSpec((None, d, tf), lambda i, j: (layer, 0, j)),
            pl.BlockSpec((None, tf, d), lambda i, j: (layer, j, 0)),
        ],
        out_specs=pl.BlockSpec((tm, d), lambda i, j: (i, 0)),
        out_shape=jax.ShapeDtypeStruct((m, d), F32),
        scratch_shapes=[pltpu.VMEM((tm, d), BF16)],
        compiler_params=_params("parallel", "arbitrary"),
        name="ffn",
    )(x, g, wg, wu, wd)


def _rope_tab_body(inv_ref, sgn_ref, cos_ref, sin_ref, *, tm, lp, past, s):
    r = lax.broadcasted_iota(jnp.int32, (tm, LANES), 0) + pl.program_id(0) * tm
    t = r - lp
    t = (t & (s - 1)) if s & (s - 1) == 0 else lax.rem(t, s)
    pos = jnp.where(r >= lp, past + t, r).astype(F32)
    ang = pos * inv_ref[...]
    cos_ref[...] = jnp.cos(ang)
    sin_ref[...] = jnp.sin(ang) * sgn_ref[...]


def _rope_tables(m, lp, past, s):
    half = QK_ROPE // 2
    inv = ROPE_THETA ** (-jnp.arange(0, QK_ROPE, 2, dtype=F32) / QK_ROPE)
    inv = jnp.tile(inv, LANES // half)[None, :]
    sgn = jnp.tile(jnp.concatenate([-jnp.ones((half,), F32), jnp.ones((half,), F32)]), LANES // QK_ROPE)[None, :]
    tm = _tile(m, 512)
    return pl.pallas_call(
        functools.partial(_rope_tab_body, tm=tm, lp=lp, past=past, s=s),
        grid=(m // tm,),
        in_specs=[_const_spec((1, LANES)), _const_spec((1, LANES))],
        out_specs=[pl.BlockSpec((tm, LANES), lambda i: (i, 0))] * 2,
        out_shape=[jax.ShapeDtypeStruct((m, LANES), F32)] * 2,
        compiler_params=_params("parallel"),
        name="rope_tables",
    )(inv, sgn)


def _swap_halves(x):
    lane = lax.broadcasted_iota(jnp.int32, x.shape, 1)
    left = pltpu.roll(x, LANES - QK_ROPE // 2, 1)
    right = pltpu.roll(x, QK_ROPE // 2, 1)
    return jnp.where((lane & (QK_ROPE - 1)) < QK_ROPE // 2, left, right)


def _even_in_body(x_ref, g_ref, win_ref, qan_ref, kvan_ref, wqb_ref, qnn_ref, knn_ref, qpn_ref, kpn_ref,
                  cos_ref, sin_ref, u_ref, q_ref, ckv_ref, kpe_ref, *, pool_dim, q_lora, kv_lora):
    h = _rms(x_ref[...], g_ref[...]).astype(BF16)
    z = _dot(h, win_ref[...])
    u_ref[...] = z[:, :pool_dim]
    o1, o2 = pool_dim + q_lora, pool_dim + q_lora + kv_lora
    ckv_ref[...] = _rms(z[:, o1:o2], kvan_ref[...])
    cos, sin = cos_ref[...], sin_ref[...]
    lane = lax.broadcasted_iota(jnp.int32, cos.shape, 1)
    lo = lane < QK_ROPE

    def rope(t):
        return t * cos + _swap_halves(t) * sin

    kp = z[:, o2:o2 + LANES]
    kp = kp * lax.rsqrt(jnp.sum(kp * kp, axis=-1, keepdims=True) / QK_ROPE + EPS) * kpn_ref[...]
    kpe_ref[...] = rope(kp)

    qn = _rms(z[:, pool_dim:o1], qan_ref[...]).astype(BF16)
    q = _dot(qn, wqb_ref[...])
    nope_w = MLA_HEADS * QK_NOPE
    k_gain = knn_ref[...] * Q_SCALE
    for hd in range(MLA_HEADS):
        qh = _rms(q[:, hd * QK_NOPE:(hd + 1) * QK_NOPE], qnn_ref[...]) * k_gain
        q_ref[hd, :, :QK_NOPE] = qh.astype(BF16)
    for j in range(MLA_HEADS // 2):
        t = q[:, nope_w + j * LANES:nope_w + (j + 1) * LANES]
        tt = t * t
        s_lo = jnp.sum(jnp.where(lo, tt, 0.0), axis=-1, keepdims=True)
        s_hi = jnp.sum(jnp.where(lo, 0.0, tt), axis=-1, keepdims=True)
        inv = jnp.where(lo, lax.rsqrt(s_lo / QK_ROPE + EPS), lax.rsqrt(s_hi / QK_ROPE + EPS))
        t = (rope(t * inv * qpn_ref[...]) * Q_SCALE).astype(BF16)
        q_ref[2 * j, :, QK_NOPE:] = t
        q_ref[2 * j + 1, :, QK_NOPE:] = t


def _even_in(x, g, layer, win, qan, kvan, wqb, qnn, knn, qpn, kpn, cos, sin, e, pool_dim, q_lora, kv_lora):
    m, d = x.shape
    tm = _tile(m, 256)
    n_in = win.shape[-1]
    n_q = wqb.shape[-1]
    row = lambda w: pl.BlockSpec((tm, w), lambda i: (i, 0))
    vec = lambda w, idx: pl.BlockSpec((None, 1, w), lambda i: (idx, 0, 0))
    return pl.pallas_call(
        functools.partial(_even_in_body, pool_dim=pool_dim, q_lora=q_lora, kv_lora=kv_lora),
        grid=(m // tm,),
        in_specs=[
            row(d), vec(d, layer),
            pl.BlockSpec((None, d, n_in), lambda i: (e, 0, 0), pipeline_mode=pl.Buffered(1)),
            vec(q_lora, e), vec(kv_lora, e),
            pl.BlockSpec((None, q_lora, n_q), lambda i: (e, 0, 0), pipeline_mode=pl.Buffered(1)),
            vec(QK_NOPE, e), vec(QK_NOPE, e), vec(LANES, e), vec(LANES, e),
            row(LANES), row(LANES),
        ],
        out_specs=[
            row(pool_dim),
            pl.BlockSpec((MLA_HEADS, tm, 2 * LANES), lambda i: (0, i, 0)),
            row(kv_lora), row(LANES),
        ],
        out_shape=[
            jax.ShapeDtypeStruct((m, pool_dim), F32),
            jax.ShapeDtypeStruct((MLA_HEADS, m, 2 * LANES), BF16),
            jax.ShapeDtypeStruct((m, kv_lora), F32),
            jax.ShapeDtypeStruct((m, LANES), F32),
        ],
        compiler_params=_params("parallel"),
        name="even_in",
    )(x, g, win, qan, kvan, wqb, qnn, knn, qpn, kpn, cos, sin)


def _kv_body(ckv_ref, kpe_ref, w_ref, k_ref, v_ref):
    c = ckv_ref[...].astype(BF16)
    kp = kpe_ref[...]
    kp_par = (kp.astype(BF16), pltpu.roll(kp, QK_ROPE, 1).astype(BF16))
    hw = QK_NOPE + V_HEAD
    for hd in range(MLA_HEADS):
        kv = _dot(c, w_ref[:, hd * hw:(hd + 1) * hw])
        k_ref[hd, :, :QK_NOPE] = _rms(kv[:, :QK_NOPE], 1.0).astype(BF16)
        k_ref[hd, :, QK_NOPE:] = kp_par[hd % 2]
        v_ref[hd, :, :V_HEAD] = kv[:, QK_NOPE:].astype(BF16)
        v_ref[hd, :, V_HEAD:] = jnp.ones((c.shape[0], LANES), BF16)


def _kv_prompt(ckv, kpe, wkvb, e, lp):
    kv_lora = ckv.shape[-1]
    tm = _tile(lp, 512)
    return pl.pallas_call(
        _kv_body,
        grid=(lp // tm,),
        in_specs=[
            pl.BlockSpec((tm, kv_lora), lambda i: (i, 0)),
            pl.BlockSpec((tm, LANES), lambda i: (i, 0)),
            pl.BlockSpec((None,) + wkvb.shape[1:], lambda i: (e, 0, 0), pipeline_mode=pl.Buffered(1)),
        ],
        out_specs=[
            pl.BlockSpec((MLA_HEADS, tm, 2 * LANES), lambda i: (0, i, 0)),
            pl.BlockSpec((MLA_HEADS, tm, V_HEAD + LANES), lambda i: (0, i, 0)),
        ],
        out_shape=[
            jax.ShapeDtypeStruct((MLA_HEADS, lp, 2 * LANES), BF16),
            jax.ShapeDtypeStruct((MLA_HEADS, lp, V_HEAD + LANES), BF16),
        ],
        compiler_params=_params("parallel"),
        name="kv_prompt",
    )(ckv, kpe, wkvb)


def _flash_body(q_ref, k_ref, v_ref, o_ref, sa_ref, sb_ref, m_ref, acc_ref, *, t):
    i = pl.program_id(1)
    q = q_ref[...]
    m_ref[...] = jnp.full(m_ref.shape, NEG_BIG, F32)
    acc_ref[...] = jnp.zeros(acc_ref.shape, F32)

    def scores(j):
        return _dot_nt(q, k_ref[pl.ds(pl.multiple_of(j * t, t), t), :])

    def update(s_ref, j, diagonal=False):
        s = s_ref[...]
        if diagonal:
            qc = lax.broadcasted_iota(jnp.int32, s.shape, 0) // CHUNK
            kc = lax.broadcasted_iota(jnp.int32, s.shape, 1) // CHUNK
            s = jnp.where(kc <= qc, s, NEG_BIG)
        m_old = m_ref[...]
        m_new = jnp.maximum(m_old, jnp.max(s, axis=-1, keepdims=True))
        p = jnp.exp2(s - jnp.tile(m_new, (1, t // LANES)))
        alpha = jnp.tile(jnp.exp2(m_old - m_new), (1, acc_ref.shape[1] // LANES))
        pv = _dot(p.astype(BF16), v_ref[pl.ds(pl.multiple_of(j * t, t), t), :])
        acc_ref[...] = alpha * acc_ref[...] + pv
        m_ref[...] = m_new

    sa_ref[...] = scores(0)

    def pair(jj, carry):
        j = 2 * jj
        sb_ref[...] = scores(j + 1)
        update(sa_ref, j)
        sa_ref[...] = scores(j + 2)
        update(sb_ref, j + 1)
        return carry

    lax.fori_loop(0, i // 2, pair, 0)

    @pl.when(i % 2 == 0)
    def _():
        update(sa_ref, i, diagonal=True)

    @pl.when(i % 2 == 1)
    def _():
        sb_ref[...] = scores(i)
        update(sa_ref, i - 1)
        update(sb_ref, i, diagonal=True)

    acc = acc_ref[...]
    o_ref[...] = (acc[:, :V_HEAD] / acc[:, V_HEAD:]).astype(BF16)


def _attn_prompt(q, k, v, lp):
    t = _tile(lp, 1024, LANES)
    return pl.pallas_call(
        functools.partial(_flash_body, t=t),
        grid=(MLA_HEADS, lp // t),
        in_specs=[
            pl.BlockSpec((None, t, 2 * LANES), lambda h, i: (h, i, 0)),
            pl.BlockSpec((None, lp, 2 * LANES), lambda h, i: (h, 0, 0)),
            pl.BlockSpec((None, lp, V_HEAD + LANES), lambda h, i: (h, 0, 0)),
        ],
        out_specs=pl.BlockSpec((t, V_HEAD), lambda h, i: (i, h)),
        out_shape=jax.ShapeDtypeStruct((q.shape[1], MLA_HEADS * V_HEAD), BF16),
        scratch_shapes=[pltpu.VMEM((t, t), F32), pltpu.VMEM((t, t), F32),
                        pltpu.VMEM((t, LANES), F32), pltpu.VMEM((t, V_HEAD + LANES), F32)],
        compiler_params=_params("parallel", "arbitrary"),
        name="attn_prompt",
    )(q, k, v)


def _attn_sample_body(q_ref, cc_ref, cn_ref, pc_ref, pn_ref, w_ref, _, o_ref, c_sc, kp_sc, *, past, s, lk):
    c_sc[:past] = cc_ref[...].astype(BF16)
    c_sc[past:past + s] = cn_ref[...].astype(BF16)
    kp_sc[0, :past] = pc_ref[...].astype(BF16)
    kp_sc[0, past:past + s] = pn_ref[...].astype(BF16)
    kp_sc[1, :past] = pltpu.roll(pc_ref[...], QK_ROPE, 1).astype(BF16)
    kp_sc[1, past:past + s] = pltpu.roll(pn_ref[...], QK_ROPE, 1).astype(BF16)
    if lk > past + s:
        c_sc[past + s:] = jnp.zeros((lk - past - s, c_sc.shape[1]), BF16)
        kp_sc[0, past + s:] = jnp.zeros((lk - past - s, LANES), BF16)
        kp_sc[1, past + s:] = jnp.zeros((lk - past - s, LANES), BF16)
    c = c_sc[...]
    valid = lax.broadcasted_iota(jnp.int32, (s, lk), 1) < past + s
    hw = QK_NOPE + V_HEAD
    ones = jnp.ones((8, QK_NOPE), BF16)
    n_ch = 4 if lk % 64 == 0 else 1
    ch = lk // n_ch
    for hd in range(MLA_HEADS):
        kv = jnp.concatenate([_dot(c[r * ch:(r + 1) * ch], w_ref[:, hd * hw:(hd + 1) * hw]) for r in range(n_ch)])
        kn = kv[:, :QK_NOPE]
        ms = _dot_nt(ones, (kn * kn).astype(BF16))[0:1] / QK_NOPE
        q = q_ref[hd]
        sc = _dot_nt(q[:, :QK_NOPE], kn.astype(BF16)) * lax.rsqrt(ms + EPS) + _dot_nt(q[:, QK_NOPE:], kp_sc[hd % 2])
        sc = jnp.where(valid, sc, NEG_BIG)
        p = jnp.exp2(sc - jnp.max(sc, axis=-1, keepdims=True))
        o = _dot(p.astype(BF16), kv[:, QK_NOPE:].astype(BF16)) / jnp.sum(p, axis=-1, keepdims=True)
        o_ref[:, hd * V_HEAD:(hd + 1) * V_HEAD] = o.astype(BF16)


def _attn_sample(q, ckv_cache, ckv, kpe_cache, kpe, wkvb, attn, e, lp, b, s):
    past, kv_lora = ckv_cache.shape[2], ckv_cache.shape[3]
    lk = past + -(-s // LANES) * LANES
    row0 = lp // s
    return pl.pallas_call(
        functools.partial(_attn_sample_body, past=past, s=s, lk=lk),
        grid=(b,),
        in_specs=[
            pl.BlockSpec((MLA_HEADS, s, 2 * LANES), lambda i: (0, row0 + i, 0)),
            pl.BlockSpec((None, None, past, kv_lora), lambda i: (e, i, 0, 0)),
            pl.BlockSpec((s, kv_lora), lambda i: (row0 + i, 0)),
            pl.BlockSpec((None, None, past, LANES), lambda i: (e, i, 0, 0)),
            pl.BlockSpec((s, LANES), lambda i: (row0 + i, 0)),
            pl.BlockSpec((None,) + wkvb.shape[1:], lambda i: (e, 0, 0), pipeline_mode=pl.Buffered(1)),
            pl.BlockSpec(memory_space=pl.ANY),
        ],
        out_specs=pl.BlockSpec((s, MLA_HEADS * V_HEAD), lambda i: (row0 + i, 0)),
        out_shape=jax.ShapeDtypeStruct(attn.shape, attn.dtype),
        input_output_aliases={6: 0},
        scratch_shapes=[pltpu.VMEM((lk, kv_lora), BF16), pltpu.VMEM((2, lk, LANES), BF16)],
        compiler_params=_params("parallel"),
        name="attn_sample",
    )(q, ckv_cache, ckv, kpe_cache, kpe, wkvb, attn)


def _pool_segment(ext_ref, pooled_ref, row0, n, pos0, gdim):
    pos = (lax.broadcasted_iota(jnp.int32, (n, 1), 0) + pos0).astype(F32)
    for gi, w in enumerate(POOL_WINDOWS):
        cols = slice(gi * gdim, (gi + 1) * gdim)
        u = ext_ref[HIST_ROWS:HIST_ROWS + n, cols]
        acc = u
        for k in range(1, w):
            acc = acc + ext_ref[HIST_ROWS - k:HIST_ROWS - k + n, cols]
        cnt = jnp.minimum(jnp.float32(w), pos + 1.0)
        pooled_ref[row0:row0 + n, cols] = (acc / cnt - u).astype(BF16)


def _pool_project(pooled_ref, pw_ref, sc_ref, o_ref, gdim):
    for gi in range(len(POOL_WINDOWS)):
        cols = slice(gi * gdim, (gi + 1) * gdim)
        o_ref[:, cols] = (_dot(pooled_ref[:, cols], pw_ref[gi]) * sc_ref[:, cols]).astype(BF16)


def _pool_prompt_body(u_ref, pw_ref, sc_ref, o_ref, ext_ref, pooled_ref, *, tm, gdim):
    i = pl.program_id(0)

    @pl.when(i == 0)
    def _():
        ext_ref[:HIST_ROWS] = jnp.zeros((HIST_ROWS, ext_ref.shape[1]), F32)

    ext_ref[HIST_ROWS:] = u_ref[...]
    _pool_segment(ext_ref, pooled_ref, 0, tm, i * tm, gdim)
    _pool_project(pooled_ref, pw_ref, sc_ref, o_ref, gdim)
    ext_ref[:HIST_ROWS] = ext_ref[tm:tm + HIST_ROWS]


def _pool_sample_body(u_ref, hist_ref, pw_ref, sc_ref, _, o_ref, ext_ref, pooled_ref, *, nb, s, past, gdim):
    for bi in range(nb):
        ext = ext_ref.at[bi]
        ext[0:1] = jnp.zeros((1, ext_ref.shape[2]), F32)
        ext[1:HIST_ROWS] = hist_ref[bi]
        ext[HIST_ROWS:] = u_ref[bi * s:(bi + 1) * s]
        _pool_segment(ext, pooled_ref, bi * s, s, past, gdim)
    _pool_project(pooled_ref, pw_ref, sc_ref, o_ref, gdim)


def _pool(u, state_pool, pw, scale, e, lp, b, s, past):
    m, pool_dim = u.shape
    ng = len(POOL_WINDOWS)
    gdim = pool_dim // ng
    w_spec = pl.BlockSpec((None, ng, gdim, gdim), lambda i: (e, 0, 0, 0))
    s_spec = pl.BlockSpec((None, 1, pool_dim), lambda i: (e, 0, 0))
    tm = _tile(lp, 512, HIST_ROWS)
    out_p = pl.pallas_call(
        functools.partial(_pool_prompt_body, tm=tm, gdim=gdim),
        grid=(lp // tm,),
        in_specs=[pl.BlockSpec((tm, pool_dim), lambda i: (i, 0)), w_spec, s_spec],
        out_specs=pl.BlockSpec((tm, pool_dim), lambda i: (i, 0)),
        out_shape=jax.ShapeDtypeStruct((m, pool_dim), BF16),
        scratch_shapes=[pltpu.VMEM((HIST_ROWS + tm, pool_dim), F32), pltpu.VMEM((tm, pool_dim), BF16)],
        compiler_params=_params("arbitrary"),
        name="pool_prompt",
    )(u, pw, scale)
    nb = math.gcd(b, max(1, 256 // s))
    row0 = lp // (nb * s)
    return pl.pallas_call(
        functools.partial(_pool_sample_body, nb=nb, s=s, past=past, gdim=gdim),
        grid=(b // nb,),
        in_specs=[
            pl.BlockSpec((nb * s, pool_dim), lambda i: (row0 + i, 0)),
            pl.BlockSpec((None, nb, POOL_HIST, pool_dim), lambda i: (e, i, 0, 0)),
            w_spec, s_spec,
            pl.BlockSpec(memory_space=pl.ANY),
        ],
        out_specs=pl.BlockSpec((nb * s, pool_dim), lambda i: (row0 + i, 0)),
        out_shape=jax.ShapeDtypeStruct((m, pool_dim), BF16),
        input_output_aliases={4: 0},
        scratch_shapes=[pltpu.VMEM((nb, HIST_ROWS + s, pool_dim), F32), pltpu.VMEM((nb * s, pool_dim), BF16)],
        compiler_params=_params("parallel"),
        name="pool_sample",
    )(u, state_pool, pw, scale, out_p)


def _even_out_body(x_ref, p_ref, a_ref, w_ref, o_ref, *, pool_dim):
    o_ref[...] = x_ref[...] + _dot(p_ref[...], w_ref[:pool_dim]) + _dot(a_ref[...], w_ref[pool_dim:])


def _even_out(x, pool_out, attn, wout, e):
    m, d = x.shape
    pool_dim, attn_dim = pool_out.shape[1], attn.shape[1]
    tm = _tile(m, 512)
    return pl.pallas_call(
        functools.partial(_even_out_body, pool_dim=pool_dim),
        grid=(m // tm,),
        in_specs=[
            pl.BlockSpec((tm, d), lambda i: (i, 0)),
            pl.BlockSpec((tm, pool_dim), lambda i: (i, 0)),
            pl.BlockSpec((tm, attn_dim), lambda i: (i, 0)),
            pl.BlockSpec((None,) + wout.shape[1:], lambda i: (e, 0, 0), pipeline_mode=pl.Buffered(1)),
        ],
        out_specs=pl.BlockSpec((tm, d), lambda i: (i, 0)),
        out_shape=jax.ShapeDtypeStruct((m, d), F32),
        compiler_params=_params("parallel"),
        name="even_out",
    )(x, pool_out, attn, wout)


def _odd_in_body(x_ref, g_ref, w_ref, vn_ref, u_ref, v_ref, vf_ref, *, gate):
    h = _rms(x_ref[...], g_ref[...]).astype(BF16)
    u_ref[...] = jax.nn.gelu(_dot(h, w_ref[:, :gate])).astype(BF16)
    v = _rms(jax.nn.gelu(_dot(h, w_ref[:, gate:])), vn_ref[...])
    v_ref[...] = v.astype(BF16)
    vf_ref[...] = v


def _odd_in(x, g, layer, win, vn, o, lp):
    m, d = x.shape
    gate = win.shape[-1] // 2
    tm = _tile(math.gcd(lp, m - lp), 256)
    first = lp // tm
    return pl.pallas_call(
        functools.partial(_odd_in_body, gate=gate),
        grid=(m // tm,),
        in_specs=[
            pl.BlockSpec((tm, d), lambda i: (i, 0)),
            pl.BlockSpec((None, 1, d), lambda i: (layer, 0, 0)),
            pl.BlockSpec((None, d, 2 * gate), lambda i: (o, 0, 0), pipeline_mode=pl.Buffered(1)),
            pl.BlockSpec((None, 1, gate), lambda i: (o, 0, 0)),
        ],
        out_specs=[
            pl.BlockSpec((tm, gate), lambda i: (i, 0)),
            pl.BlockSpec((tm, gate), lambda i: (i, 0)),
            pl.BlockSpec((tm, gate), lambda i: (jnp.maximum(i - first, 0), 0)),
        ],
        out_shape=[
            jax.ShapeDtypeStruct((m, gate), BF16),
            jax.ShapeDtypeStruct((m, gate), BF16),
            jax.ShapeDtypeStruct((m - lp, gate), F32),
        ],
        compiler_params=_params("arbitrary"),
        name="odd_in",
    )(x, g, win, vn)


def _odd_out_body(x_ref, u_ref, v_ref, ws_ref, b_ref, w_ref, o_ref, us_ref, *, tm, n_prompt_tiles, s):
    is_prompt = pl.program_id(0) < n_prompt_tiles
    ii = lax.broadcasted_iota(jnp.int32, (GMLP_CHUNK, GMLP_CHUNK), 0)
    jj = lax.broadcasted_iota(jnp.int32, (GMLP_CHUNK, GMLP_CHUNK), 1)
    same_stream = jnp.where((ii // s) == (jj // s), 1, 0) + jnp.where(is_prompt, 1, 0)
    keep = (jj <= ii) & (same_stream > 0)
    gdim = u_ref.shape[1] // GMLP_GROUPS
    for gi in range(GMLP_GROUPS):
        cols = slice(gi * gdim, (gi + 1) * gdim)
        wmat = jnp.where(keep, ws_ref[gi], 0.0).astype(BF16)
        bias = b_ref[:, gi:gi + 1]
        for ci in range(tm // GMLP_CHUNK):
            rows = slice(ci * GMLP_CHUNK, (ci + 1) * GMLP_CHUNK)
            sg = _dot(wmat, v_ref[rows, cols]) + bias
            us_ref[rows, cols] = (u_ref[rows, cols].astype(F32) * sg).astype(BF16)
    o_ref[...] = x_ref[...] + _dot(us_ref[...], w_ref[...])


def _odd_out(x, u, v, ws_sel, b_sel, wout, o, lp, s):
    m, d = x.shape
    gate = u.shape[1]
    tm = _tile(math.gcd(lp, m - lp), 512, GMLP_CHUNK)
    npt = lp // tm
    sel = lambda i: jnp.where(i >= npt, 1, 0)
    return pl.pallas_call(
        functools.partial(_odd_out_body, tm=tm, n_prompt_tiles=npt, s=s),
        grid=(m // tm,),
        in_specs=[
            pl.BlockSpec((tm, d), lambda i: (i, 0)),
            pl.BlockSpec((tm, gate), lambda i: (i, 0)),
            pl.BlockSpec((tm, gate), lambda i: (i, 0)),
            pl.BlockSpec((None, None, GMLP_GROUPS, GMLP_CHUNK, GMLP_CHUNK), lambda i: (sel(i), o, 0, 0, 0)),
            pl.BlockSpec((None, None, GMLP_CHUNK, GMLP_GROUPS), lambda i: (sel(i), o, 0, 0)),
            pl.BlockSpec((None,) + wout.shape[1:], lambda i: (o, 0, 0), pipeline_mode=pl.Buffered(1)),
        ],
        out_specs=pl.BlockSpec((tm, d), lambda i: (i, 0)),
        out_shape=jax.ShapeDtypeStruct((m, d), F32),
        scratch_shapes=[pltpu.VMEM((tm, gate), BF16)],
        compiler_params=_params("parallel"),
        name="odd_out",
    )(x, u, v, ws_sel, b_sel, wout)


def kernel(x_prompt, x_sample, cache_mla_ckv, cache_mla_kpe, state_pool, norm_ffn1, norm_mix, norm_ffn2, ffn1_w_gate, ffn1_w_up, ffn1_w_down, ffn2_w_gate, ffn2_w_up, ffn2_w_down, ev_w_in, ev_q_a_norm, ev_kv_a_norm, ev_w_qb, ev_w_kvb, ev_q_nope_norm, ev_q_pe_norm, ev_k_nope_norm, ev_k_pe_norm, ev_pool_w, ev_pool_scale, ev_w_out, od_w_in, od_v_norm, od_w_s, od_b_s, od_w_out):
    bp, lp, d = x_prompt.shape
    b, s, _ = x_sample.shape
    depth = norm_mix.shape[0]
    n_even, n_odd = ev_w_in.shape[0], od_w_in.shape[0]
    past = cache_mla_ckv.shape[2]
    q_lora, kv_lora = ev_q_a_norm.shape[1], ev_kv_a_norm.shape[1]
    pool_dim = ev_pool_scale.shape[1]
    gate = od_v_norm.shape[1]
    m = lp + b * s
    assert bp == 1 and lp % GMLP_CHUNK == 0 and (b * s) % GMLP_CHUNK == 0 and GMLP_CHUNK % s == 0
    assert past % CHUNK == 0 and s <= CHUNK, "every cached and new key must be visible to every sample query"
    assert s >= POOL_HIST and lp >= POOL_HIST and lp % s == 0
    assert ev_w_in.shape[2] == pool_dim + q_lora + kv_lora + QK_ROPE

    vec = lambda a: a[:, None, :]
    bf = lambda a: a.astype(BF16)
    pad_lanes = lambda a: jnp.pad(a, [(0, 0)] * (a.ndim - 1) + [(0, (-a.shape[-1]) % LANES)])

    w_ffn = ((ffn1_w_gate, ffn1_w_up, ffn1_w_down), (ffn2_w_gate, ffn2_w_up, ffn2_w_down))
    g_ffn = (vec(norm_ffn1), vec(norm_ffn2))
    g_mix = vec(norm_mix)
    w_in_e = bf(pad_lanes(ev_w_in))
    wq = ev_w_qb.reshape(n_even, q_lora, MLA_HEADS, QK_HEAD)
    w_qb = bf(jnp.concatenate([wq[..., :QK_NOPE].reshape(n_even, q_lora, -1),
                               wq[..., QK_NOPE:].reshape(n_even, q_lora, -1)], axis=-1))
    w_kvb, w_out_e, pool_w = bf(ev_w_kvb), bf(ev_w_out), bf(ev_pool_w)
    qpn = vec(jnp.tile(ev_q_pe_norm, (1, LANES // QK_ROPE)))
    kpn = vec(pad_lanes(ev_k_pe_norm))
    kpe_cache = pad_lanes(cache_mla_kpe)
    w_in_o, w_out_o = bf(od_w_in), bf(od_w_out)
    rep = GMLP_CHUNK // s
    ws_sel = jnp.stack([od_w_s, jnp.tile(od_w_s[:, :, :s, :s], (1, 1, rep, rep))])
    b_sel = jnp.stack([od_b_s, jnp.tile(od_b_s[:, :, :s], (1, 1, rep))]).transpose(0, 1, 3, 2)

    cos, sin = _rope_tables(m, lp, past, s)
    x = jnp.concatenate([x_prompt.reshape(lp, d), x_sample.reshape(b * s, d)], axis=0)
    new_ckv, new_kpe, new_u, new_v = [], [], [], []
    for layer in range(depth):
        x = _ffn(x, g_ffn[0], *w_ffn[0], layer)
        if layer % 2 == 0:
            e = layer // 2
            u, q, ckv, kpe = _even_in(x, g_mix, layer, w_in_e, vec(ev_q_a_norm), vec(ev_kv_a_norm), w_qb,
                                      vec(ev_q_nope_norm), vec(ev_k_nope_norm), qpn, kpn, cos, sin, e,
                                      pool_dim, q_lora, kv_lora)
            k_p, v_p = _kv_prompt(ckv, kpe, w_kvb, e, lp)
            attn = _attn_prompt(q, k_p, v_p, lp)
            attn = _attn_sample(q, cache_mla_ckv, ckv, kpe_cache, kpe, w_kvb, attn, e, lp, b, s)
            pool_out = _pool(u, state_pool, pool_w, vec(ev_pool_scale), e, lp, b, s, past)
            x = _even_out(x, pool_out, attn, w_out_e, e)
            new_ckv.append((ckv[:lp], ckv[lp:]))
            new_kpe.append((kpe[:lp, :QK_ROPE], kpe[lp:, :QK_ROPE]))
            new_u.append((u[lp - POOL_HIST:lp], u[lp:].reshape(b, s, pool_dim)[:, s - POOL_HIST:]))
        else:
            o = layer // 2
            u, v, v_f32 = _odd_in(x, g_mix, layer, w_in_o, vec(od_v_norm), o, lp)
            x = _odd_out(x, u, v, ws_sel, b_sel, w_out_o, o, lp, s)
            new_v.append(v_f32)
        x = _ffn(x, g_ffn[1], *w_ffn[1], layer)

    stack = lambda pairs, k, shape: jnp.stack([p[k] for p in pairs]).reshape((n_even,) + shape)
    return (
        x[:lp].reshape(1, lp, d),
        x[lp:].reshape(b, s, d),
        stack(new_ckv, 0, (1, lp, kv_lora)),
        stack(new_kpe, 0, (1, lp, QK_ROPE)),
        stack(new_u, 0, (1, POOL_HIST, pool_dim)),
        stack(new_ckv, 1, (b, s, kv_lora)),
        stack(new_kpe, 1, (b, s, QK_ROPE)),
        stack(new_u, 1, (b, POOL_HIST, pool_dim)),
        jnp.stack(new_v).reshape(n_odd, b, s, gate),
    )
```

```python
import functools
import math

import jax
import jax.numpy as jnp
from jax import lax
from jax.experimental import pallas as pl
from jax.experimental.pallas import tpu as pltpu

F32 = jnp.float32
BF16 = jnp.bfloat16

EPS = 1e-6
CHUNK = 64
POOL_WINDOWS = (2, 4, 8, 16)
POOL_HIST = max(POOL_WINDOWS) - 1
HIST_ROWS = 16
MLA_HEADS = 8
QK_NOPE = 128
QK_ROPE = 64
V_HEAD = 128
QK_HEAD = QK_NOPE + QK_ROPE
ATTN_SCALE = QK_HEAD ** -0.5
Q_SCALE = ATTN_SCALE * math.log2(math.e)
ROPE_THETA = 10000.0
GMLP_CHUNK = 128
GMLP_GROUPS = 8
LANES = 128
NEG_BIG = -1e30
VMEM_LIMIT = 56 * 1024 * 1024


def _params(*sem):
    return pltpu.CompilerParams(dimension_semantics=sem, vmem_limit_bytes=VMEM_LIMIT)


def _tile(n, pref, mult=8):
    if n <= pref:
        return n
    for t in range(pref - pref % mult, 0, -mult):
        if n % t == 0:
            return t
    raise ValueError(f"no tile for {n}")


def _const_spec(shape):
    nd = len(shape)
    return pl.BlockSpec(shape, lambda *_: (0,) * nd, pipeline_mode=pl.Buffered(1))


def _rms(x, g):
    ms = jnp.mean(x * x, axis=-1, keepdims=True)
    return x * lax.rsqrt(ms + EPS) * g


def _dot(a, b):
    return jnp.dot(a, b, preferred_element_type=F32)


def _dot_nt(a, b):
    return lax.dot_general(a, b, (((1,), (1,)), ((), ())), preferred_element_type=F32)


def _ffn_body(x_ref, g_ref, wg_ref, wu_ref, wd_ref, o_ref, h_ref):
    @pl.when(pl.program_id(1) == 0)
    def _():
        x = x_ref[...]
        h_ref[...] = _rms(x, g_ref[...]).astype(BF16)
        o_ref[...] = x

    h = h_ref[...]
    a = _dot(h, wg_ref[...].astype(BF16))
    b = _dot(h, wu_ref[...].astype(BF16))
    act = (a * jax.nn.sigmoid(a) * b).astype(BF16)
    o_ref[...] += 0.5 * _dot(act, wd_ref[...].astype(BF16))


def _ffn(x, g, wg, wu, wd, layer):
    m, d = x.shape
    f = wg.shape[-1]
    tm, tf = _tile(m, 1024), _tile(f, 256, LANES)
    return pl.pallas_call(
        _ffn_body,
        grid=(m // tm, f // tf),
        in_specs=[
            pl.BlockSpec((tm, d), lambda i, j: (i, 0)),
            pl.BlockSpec((None, 1, d), lambda i, j: (layer, 0, 0)),
            pl.BlockSpec((None, d, tf), lambda i, j: (layer, 0, j)),
            pl.BlockSpec((None, d, tf), lambda i, j: (layer, 0, j)),
            pl.BlockSpec((None, tf, d), lambda i, j: (layer, j, 0)),
        ],
        out_specs=pl.BlockSpec((tm, d), lambda i, j: (i, 0)),
        out_shape=jax.ShapeDtypeStruct((m, d), F32),
        scratch_shapes=[pltpu.VMEM((tm, d), BF16)],
        compiler_params=_params("parallel", "arbitrary"),
        name="ffn",
    )(x, g, wg, wu, wd)


def _rope_tab_body(inv_ref, sgn_ref, cos_ref, sin_ref, *, tm, lp, past, s):
    r = lax.broadcasted_iota(jnp.int32, (tm, LANES), 0) + pl.program_id(0) * tm
    t = r - lp
    t = (t & (s - 1)) if s & (s - 1) == 0 else lax.rem(t, s)
    pos = jnp.where(r >= lp, past + t, r).astype(F32)
    ang = pos * inv_ref[...]
    cos_ref[...] = jnp.cos(ang)
    sin_ref[...] = jnp.sin(ang) * sgn_ref[...]


def _rope_tables(m, lp, past, s):
    half = QK_ROPE // 2
    inv = ROPE_THETA ** (-jnp.arange(0, QK_ROPE, 2, dtype=F32) / QK_ROPE)
    inv = jnp.tile(inv, LANES // half)[None, :]
    sgn = jnp.tile(jnp.concatenate([-jnp.ones((half,), F32), jnp.ones((half,), F32)]), LANES // QK_ROPE)[None, :]
    tm = _tile(m, 512)
    return pl.pallas_call(
        functools.partial(_rope_tab_body, tm=tm, lp=lp, past=past, s=s),
        grid=(m // tm,),
        in_specs=[_const_spec((1, LANES)), _const_spec((1, LANES))],
        out_specs=[pl.BlockSpec((tm, LANES), lambda i: (i, 0))] * 2,
        out_shape=[jax.ShapeDtypeStruct((m, LANES), F32)] * 2,
        compiler_params=_params("parallel"),
        name="rope_tables",
    )(inv, sgn)


def _swap_halves(x):
    lane = lax.broadcasted_iota(jnp.int32, x.shape, 1)
    left = pltpu.roll(x, LANES - QK_ROPE // 2, 1)
    right = pltpu.roll(x, QK_ROPE // 2, 1)
    return jnp.where((lane & (QK_ROPE - 1)) < QK_ROPE // 2, left, right)


def _even_in_body(x_ref, g_ref, win_ref, qan_ref, kvan_ref, wqb_ref, qnn_ref, knn_ref, qpn_ref, kpn_ref,
                  cos_ref, sin_ref, u_ref, q_ref, ckv_ref, kpe_ref, *, pool_dim, q_lora, kv_lora):
    h = _rms(x_ref[...], g_ref[...]).astype(BF16)
    z = _dot(h, win_ref[...])
    u_ref[...] = z[:, :pool_dim]
    o1, o2 = pool_dim + q_lora, pool_dim + q_lora + kv_lora
    ckv_ref[...] = _rms(z[:, o1:o2], kvan_ref[...])
    cos, sin = cos_ref[...], sin_ref[...]
    lane = lax.broadcasted_iota(jnp.int32, cos.shape, 1)
    lo = lane < QK_ROPE

    def rope(t):
        return t * cos + _swap_halves(t) * sin

    kp = z[:, o2:o2 + LANES]
    kp = kp * lax.rsqrt(jnp.sum(kp * kp, axis=-1, keepdims=True) / QK_ROPE + EPS) * kpn_ref[...]
    kpe_ref[...] = rope(kp)

    qn = _rms(z[:, pool_dim:o1], qan_ref[...]).astype(BF16)
    q = _dot(qn, wqb_ref[...])
    nope_w = MLA_HEADS * QK_NOPE
    k_gain = knn_ref[...] * Q_SCALE
    for hd in range(MLA_HEADS):
        qh = _rms(q[:, hd * QK_NOPE:(hd + 1) * QK_NOPE], qnn_ref[...]) * k_gain
        q_ref[hd, :, :QK_NOPE] = qh.astype(BF16)
    for j in range(MLA_HEADS // 2):
        t = q[:, nope_w + j * LANES:nope_w + (j + 1) * LANES]
        tt = t * t
        s_lo = jnp.sum(jnp.where(lo, tt, 0.0), axis=-1, keepdims=True)
        s_hi = jnp.sum(jnp.where(lo, 0.0, tt), axis=-1, keepdims=True)
        inv = jnp.where(lo, lax.rsqrt(s_lo / QK_ROPE + EPS), lax.rsqrt(s_hi / QK_ROPE + EPS))
        t = rope(t * inv * qpn_ref[...]) * Q_SCALE
        q_ref[2 * j, :, QK_NOPE:] = jnp.where(lo, t, 0.0).astype(BF16)
        q_ref[2 * j + 1, :, QK_NOPE:] = jnp.where(lo, 0.0, t).astype(BF16)


def _even_in(x, g, layer, win, qan, kvan, wqb, qnn, knn, qpn, kpn, cos, sin, e, pool_dim, q_lora, kv_lora):
    m, d = x.shape
    tm = _tile(m, 512)
    n_in = win.shape[-1]
    n_q = wqb.shape[-1]
    row = lambda w: pl.BlockSpec((tm, w), lambda i: (i, 0))
    vec = lambda w, idx: pl.BlockSpec((None, 1, w), lambda i: (idx, 0, 0))
    return pl.pallas_call(
        functools.partial(_even_in_body, pool_dim=pool_dim, q_lora=q_lora, kv_lora=kv_lora),
        grid=(m // tm,),
        in_specs=[
            row(d), vec(d, layer),
            pl.BlockSpec((None, d, n_in), lambda i: (e, 0, 0), pipeline_mode=pl.Buffered(1)),
            vec(q_lora, e), vec(kv_lora, e),
            pl.BlockSpec((None, q_lora, n_q), lambda i: (e, 0, 0), pipeline_mode=pl.Buffered(1)),
            vec(QK_NOPE, e), vec(QK_NOPE, e), vec(LANES, e), vec(LANES, e),
            row(LANES), row(LANES),
        ],
        out_specs=[
            row(pool_dim),
            pl.BlockSpec((MLA_HEADS, tm, 2 * LANES), lambda i: (0, i, 0)),
            row(kv_lora), row(LANES),
        ],
        out_shape=[
            jax.ShapeDtypeStruct((m, pool_dim), F32),
            jax.ShapeDtypeStruct((MLA_HEADS, m, 2 * LANES), BF16),
            jax.ShapeDtypeStruct((m, kv_lora), F32),
            jax.ShapeDtypeStruct((m, LANES), F32),
        ],
        compiler_params=_params("parallel"),
        name="even_in",
    )(x, g, win, qan, kvan, wqb, qnn, knn, qpn, kpn, cos, sin)


def _kv_body(ckv_ref, kpe_ref, w_ref, k_ref, v_ref):
    c = ckv_ref[...].astype(BF16)
    kp = kpe_ref[...]
    kp_both = (kp + pltpu.roll(kp, QK_ROPE, 1)).astype(BF16)
    hw = QK_NOPE + V_HEAD
    for hd in range(MLA_HEADS):
        kv = _dot(c, w_ref[:, hd * hw:(hd + 1) * hw])
        k_ref[hd, :, :QK_NOPE] = _rms(kv[:, :QK_NOPE], 1.0).astype(BF16)
        k_ref[hd, :, QK_NOPE:] = kp_both
        v_ref[hd, :, :V_HEAD] = kv[:, QK_NOPE:].astype(BF16)
        v_ref[hd, :, V_HEAD:] = jnp.ones((c.shape[0], LANES), BF16)


def _kv_prompt(ckv, kpe, wkvb, e, lp):
    kv_lora = ckv.shape[-1]
    tm = _tile(lp, 512)
    return pl.pallas_call(
        _kv_body,
        grid=(lp // tm,),
        in_specs=[
            pl.BlockSpec((tm, kv_lora), lambda i: (i, 0)),
            pl.BlockSpec((tm, LANES), lambda i: (i, 0)),
            pl.BlockSpec((None,) + wkvb.shape[1:], lambda i: (e, 0, 0), pipeline_mode=pl.Buffered(1)),
        ],
        out_specs=[
            pl.BlockSpec((MLA_HEADS, tm, 2 * LANES), lambda i: (0, i, 0)),
            pl.BlockSpec((MLA_HEADS, tm, V_HEAD + LANES), lambda i: (0, i, 0)),
        ],
        out_shape=[
            jax.ShapeDtypeStruct((MLA_HEADS, lp, 2 * LANES), BF16),
            jax.ShapeDtypeStruct((MLA_HEADS, lp, V_HEAD + LANES), BF16),
        ],
        compiler_params=_params("parallel"),
        name="kv_prompt",
    )(ckv, kpe, wkvb)


def _flash_body(q_ref, k_ref, v_ref, o_ref, sa_ref, sb_ref, m_ref, acc_ref, *, t):
    i = pl.program_id(1)
    q = q_ref[...]
    m_ref[...] = jnp.full(m_ref.shape, NEG_BIG, F32)
    acc_ref[...] = jnp.zeros(acc_ref.shape, F32)

    def scores(j):
        return _dot_nt(q, k_ref[pl.ds(pl.multiple_of(j * t, t), t), :])

    def update(s_ref, j, diagonal=False):
        s = s_ref[...]
        if diagonal:
            qc = lax.broadcasted_iota(jnp.int32, s.shape, 0) // CHUNK
            kc = lax.broadcasted_iota(jnp.int32, s.shape, 1) // CHUNK
            s = jnp.where(kc <= qc, s, NEG_BIG)
        m_old = m_ref[...]
        m_new = jnp.maximum(m_old, jnp.max(s, axis=-1, keepdims=True))
        p = jnp.exp2(s - jnp.tile(m_new, (1, t // LANES)))
        alpha = jnp.tile(jnp.exp2(m_old - m_new), (1, acc_ref.shape[1] // LANES))
        pv = _dot(p.astype(BF16), v_ref[pl.ds(pl.multiple_of(j * t, t), t), :])
        acc_ref[...] = alpha * acc_ref[...] + pv
        m_ref[...] = m_new

    sa_ref[...] = scores(0)

    def pair(jj, carry):
        j = 2 * jj
        sb_ref[...] = scores(j + 1)
        update(sa_ref, j)
        sa_ref[...] = scores(j + 2)
        update(sb_ref, j + 1)
        return carry

    lax.fori_loop(0, i // 2, pair, 0)

    @pl.when(i % 2 == 0)
    def _():
        update(sa_ref, i, diagonal=True)

    @pl.when(i % 2 == 1)
    def _():
        sb_ref[...] = scores(i)
        update(sa_ref, i - 1)
        update(sb_ref, i, diagonal=True)

    acc = acc_ref[...]
    o_ref[...] = (acc[:, :V_HEAD] / acc[:, V_HEAD:]).astype(BF16)


def _attn_prompt(q, k, v, lp):
    t = _tile(lp, 1024, LANES)
    return pl.pallas_call(
        functools.partial(_flash_body, t=t),
        grid=(MLA_HEADS, lp // t),
        in_specs=[
            pl.BlockSpec((None, t, 2 * LANES), lambda h, i: (h, i, 0)),
            pl.BlockSpec((None, lp, 2 * LANES), lambda h, i: (h, 0, 0)),
            pl.BlockSpec((None, lp, V_HEAD + LANES), lambda h, i: (h, 0, 0)),
        ],
        out_specs=pl.BlockSpec((t, V_HEAD), lambda h, i: (i, h)),
        out_shape=jax.ShapeDtypeStruct((lp, MLA_HEADS * V_HEAD), BF16),
        scratch_shapes=[pltpu.VMEM((t, t), F32), pltpu.VMEM((t, t), F32),
                        pltpu.VMEM((t, LANES), F32), pltpu.VMEM((t, V_HEAD + LANES), F32)],
        compiler_params=_params("parallel", "arbitrary"),
        name="attn_prompt",
    )(q, k, v)


def _attn_sample_body(q_ref, cc_ref, cn_ref, pc_ref, pn_ref, wk_ref, wv_ref, o_ref, c_sc, kp_sc, p_sc,
                      *, past, s, lk):
    c_sc[:past] = cc_ref[...].astype(BF16)
    c_sc[past:past + s] = cn_ref[...].astype(BF16)
    pc, pn = pc_ref[...], pn_ref[...]
    kp_sc[:past] = jnp.concatenate([pc, pc], axis=1).astype(BF16)
    kp_sc[past:past + s] = (pn + pltpu.roll(pn, QK_ROPE, 1)).astype(BF16)
    if lk > past + s:
        c_sc[past + s:] = jnp.zeros((lk - past - s, c_sc.shape[1]), BF16)
        kp_sc[past + s:] = jnp.zeros((lk - past - s, LANES), BF16)
    c = c_sc[...]
    valid = lax.broadcasted_iota(jnp.int32, (s, lk), 1) < past + s
    ones = jnp.ones((8, QK_NOPE), BF16)
    n_ch = 4 if lk % 64 == 0 else 1
    ch = lk // n_ch
    pe = _dot_nt(jnp.concatenate([q_ref[hd][:, QK_NOPE:] for hd in range(MLA_HEADS)], axis=0), kp_sc[...])
    inv_l = []
    for pair in range(MLA_HEADS // 2):
        wk = wk_ref[:, pair * 2 * QK_NOPE:(pair + 1) * 2 * QK_NOPE]
        kk = jnp.concatenate([_dot(c[r * ch:(r + 1) * ch], wk) for r in range(n_ch)])
        for hd in (2 * pair, 2 * pair + 1):
            kn = kk[:, (hd % 2) * QK_NOPE:(hd % 2 + 1) * QK_NOPE]
            ms = _dot_nt(ones, (kn * kn).astype(BF16))[0:1] / QK_NOPE
            sc = _dot_nt(q_ref[hd][:, :QK_NOPE], kn.astype(BF16)) * lax.rsqrt(ms + EPS) + pe[hd * s:(hd + 1) * s]
            sc = jnp.where(valid, sc, NEG_BIG)
            p = jnp.exp2(sc - jnp.max(sc, axis=-1, keepdims=True))
            inv_l.append(1.0 / jnp.sum(p, axis=-1, keepdims=True))
            p_sc[hd * s:(hd + 1) * s] = p.astype(BF16)
    ctx = _dot(p_sc[...], c)
    for hd in range(MLA_HEADS):
        ctx_h = (ctx[hd * s:(hd + 1) * s] * inv_l[hd]).astype(BF16)
        o_ref[:, hd * V_HEAD:(hd + 1) * V_HEAD] = _dot(ctx_h, wv_ref[:, hd * V_HEAD:(hd + 1) * V_HEAD]).astype(BF16)


def _attn_sample(q, ckv_cache, ckv, kpe_cache, kpe, wk, wv, e, lp, b, s):
    past, kv_lora = ckv_cache.shape[2], ckv_cache.shape[3]
    lk = past + -(-s // LANES) * LANES
    row0 = lp // s
    return pl.pallas_call(
        functools.partial(_attn_sample_body, past=past, s=s, lk=lk),
        grid=(b,),
        in_specs=[
            pl.BlockSpec((MLA_HEADS, s, 2 * LANES), lambda i: (0, row0 + i, 0)),
            pl.BlockSpec((None, None, past, kv_lora), lambda i: (e, i, 0, 0)),
            pl.BlockSpec((s, kv_lora), lambda i: (row0 + i, 0)),
            pl.BlockSpec((None, None, past, QK_ROPE), lambda i: (e, i, 0, 0)),
            pl.BlockSpec((s, LANES), lambda i: (row0 + i, 0)),
            pl.BlockSpec((None,) + wk.shape[1:], lambda i: (e, 0, 0), pipeline_mode=pl.Buffered(1)),
            pl.BlockSpec((None,) + wv.shape[1:], lambda i: (e, 0, 0), pipeline_mode=pl.Buffered(1)),
        ],
        out_specs=pl.BlockSpec((s, MLA_HEADS * V_HEAD), lambda i: (i, 0)),
        out_shape=jax.ShapeDtypeStruct((b * s, MLA_HEADS * V_HEAD), BF16),
        scratch_shapes=[pltpu.VMEM((lk, kv_lora), BF16), pltpu.VMEM((lk, LANES), BF16),
                        pltpu.VMEM((MLA_HEADS * s, lk), BF16)],
        compiler_params=_params("parallel"),
        name="attn_sample",
    )(q, ckv_cache, ckv, kpe_cache, kpe, wk, wv)


def _pool_segment(ext_ref, pooled_ref, row0, n, pos0, gdim):
    pos = (lax.broadcasted_iota(jnp.int32, (n, 1), 0) + pos0).astype(F32)
    for gi, w in enumerate(POOL_WINDOWS):
        cols = slice(gi * gdim, (gi + 1) * gdim)
        u = ext_ref[HIST_ROWS:HIST_ROWS + n, cols]
        acc = u
        for k in range(1, w):
            acc = acc + ext_ref[HIST_ROWS - k:HIST_ROWS - k + n, cols]
        cnt = jnp.minimum(jnp.float32(w), pos + 1.0)
        pooled_ref[row0:row0 + n, cols] = (acc / cnt - u).astype(BF16)


def _pool_project(pooled_ref, pw_ref, sc_ref, o_ref, gdim):
    for gi in range(len(POOL_WINDOWS)):
        cols = slice(gi * gdim, (gi + 1) * gdim)
        o_ref[:, cols] = (_dot(pooled_ref[:, cols], pw_ref[gi]) * sc_ref[:, cols]).astype(BF16)


def _pool_prompt_body(u_ref, pw_ref, sc_ref, o_ref, ext_ref, pooled_ref, *, tm, gdim):
    i = pl.program_id(0)

    @pl.when(i == 0)
    def _():
        ext_ref[:HIST_ROWS] = jnp.zeros((HIST_ROWS, ext_ref.shape[1]), F32)

    ext_ref[HIST_ROWS:] = u_ref[...]
    _pool_segment(ext_ref, pooled_ref, 0, tm, i * tm, gdim)
    _pool_project(pooled_ref, pw_ref, sc_ref, o_ref, gdim)
    ext_ref[:HIST_ROWS] = ext_ref[tm:tm + HIST_ROWS]


def _pool_sample_body(u_ref, hist_ref, pw_ref, sc_ref, o_ref, ext_ref, pooled_ref, *, nb, s, past, gdim):
    for bi in range(nb):
        ext = ext_ref.at[bi]
        ext[0:1] = jnp.zeros((1, ext_ref.shape[2]), F32)
        ext[1:HIST_ROWS] = hist_ref[bi]
        ext[HIST_ROWS:] = u_ref[bi * s:(bi + 1) * s]
        _pool_segment(ext, pooled_ref, bi * s, s, past, gdim)
    _pool_project(pooled_ref, pw_ref, sc_ref, o_ref, gdim)


def _pool(u, state_pool, pw, scale, e, lp, b, s, past):
    m, pool_dim = u.shape
    ng = len(POOL_WINDOWS)
    gdim = pool_dim // ng
    w_spec = pl.BlockSpec((None, ng, gdim, gdim), lambda i: (e, 0, 0, 0))
    s_spec = pl.BlockSpec((None, 1, pool_dim), lambda i: (e, 0, 0))
    tm = _tile(lp, 512, HIST_ROWS)
    out_p = pl.pallas_call(
        functools.partial(_pool_prompt_body, tm=tm, gdim=gdim),
        grid=(lp // tm,),
        in_specs=[pl.BlockSpec((tm, pool_dim), lambda i: (i, 0)), w_spec, s_spec],
        out_specs=pl.BlockSpec((tm, pool_dim), lambda i: (i, 0)),
        out_shape=jax.ShapeDtypeStruct((lp, pool_dim), BF16),
        scratch_shapes=[pltpu.VMEM((HIST_ROWS + tm, pool_dim), F32), pltpu.VMEM((tm, pool_dim), BF16)],
        compiler_params=_params("arbitrary"),
        name="pool_prompt",
    )(u, pw, scale)
    nb = math.gcd(b, max(1, 256 // s))
    row0 = lp // (nb * s)
    out_s = pl.pallas_call(
        functools.partial(_pool_sample_body, nb=nb, s=s, past=past, gdim=gdim),
        grid=(b // nb,),
        in_specs=[
            pl.BlockSpec((nb * s, pool_dim), lambda i: (row0 + i, 0)),
            pl.BlockSpec((None, nb, POOL_HIST, pool_dim), lambda i: (e, i, 0, 0)),
            w_spec, s_spec,
        ],
        out_specs=pl.BlockSpec((nb * s, pool_dim), lambda i: (i, 0)),
        out_shape=jax.ShapeDtypeStruct((b * s, pool_dim), BF16),
        scratch_shapes=[pltpu.VMEM((nb, HIST_ROWS + s, pool_dim), F32), pltpu.VMEM((nb * s, pool_dim), BF16)],
        compiler_params=_params("parallel"),
        name="pool_sample",
    )(u, state_pool, pw, scale)
    return out_p, out_s


def _even_out_body(x_ref, pp_ref, ps_ref, ap_ref, as_ref, w_ref, o_ref, *, pool_dim, n_prompt_tiles):
    def project(p_ref, a_ref):
        o_ref[...] = x_ref[...] + _dot(p_ref[...], w_ref[:pool_dim]) + _dot(a_ref[...], w_ref[pool_dim:])

    is_prompt = pl.program_id(0) < n_prompt_tiles
    pl.when(is_prompt)(lambda: project(pp_ref, ap_ref))
    pl.when(jnp.logical_not(is_prompt))(lambda: project(ps_ref, as_ref))


def _even_out(x, pool_p, pool_s, attn_p, attn_s, wout, e):
    m, d = x.shape
    lp, pool_dim = pool_p.shape
    attn_dim = attn_p.shape[1]
    tm = _tile(math.gcd(lp, m - lp), 512)
    npt = lp // tm
    prompt_rows = lambda w: pl.BlockSpec((tm, w), lambda i: (jnp.minimum(i, npt - 1), 0))
    sample_rows = lambda w: pl.BlockSpec((tm, w), lambda i: (jnp.maximum(i - npt, 0), 0))
    return pl.pallas_call(
        functools.partial(_even_out_body, pool_dim=pool_dim, n_prompt_tiles=npt),
        grid=(m // tm,),
        in_specs=[
            pl.BlockSpec((tm, d), lambda i: (i, 0)),
            prompt_rows(pool_dim), sample_rows(pool_dim), prompt_rows(attn_dim), sample_rows(attn_dim),
            pl.BlockSpec((None,) + wout.shape[1:], lambda i: (e, 0, 0), pipeline_mode=pl.Buffered(1)),
        ],
        out_specs=pl.BlockSpec((tm, d), lambda i: (i, 0)),
        out_shape=jax.ShapeDtypeStruct((m, d), F32),
        compiler_params=_params("arbitrary"),
        name="even_out",
    )(x, pool_p, pool_s, attn_p, attn_s, wout)


def _odd_in_body(x_ref, g_ref, w_ref, vn_ref, u_ref, v_ref, vf_ref, *, gate):
    h = _rms(x_ref[...], g_ref[...]).astype(BF16)
    u_ref[...] = jax.nn.gelu(_dot(h, w_ref[:, :gate])).astype(BF16)
    v = _rms(jax.nn.gelu(_dot(h, w_ref[:, gate:])), vn_ref[...])
    v_ref[...] = v.astype(BF16)
    vf_ref[...] = v


def _odd_in(x, g, layer, win, vn, o, lp):
    m, d = x.shape
    gate = win.shape[-1] // 2
    tm = _tile(math.gcd(lp, m - lp), 512)
    first = lp // tm
    return pl.pallas_call(
        functools.partial(_odd_in_body, gate=gate),
        grid=(m // tm,),
        in_specs=[
            pl.BlockSpec((tm, d), lambda i: (i, 0)),
            pl.BlockSpec((None, 1, d), lambda i: (layer, 0, 0)),
            pl.BlockSpec((None, d, 2 * gate), lambda i: (o, 0, 0), pipeline_mode=pl.Buffered(1)),
            pl.BlockSpec((None, 1, gate), lambda i: (o, 0, 0)),
        ],
        out_specs=[
            pl.BlockSpec((tm, gate), lambda i: (i, 0)),
            pl.BlockSpec((tm, gate), lambda i: (i, 0)),
            pl.BlockSpec((tm, gate), lambda i: (jnp.maximum(i - first, 0), 0)),
        ],
        out_shape=[
            jax.ShapeDtypeStruct((m, gate), BF16),
            jax.ShapeDtypeStruct((m, gate), BF16),
            jax.ShapeDtypeStruct((m - lp, gate), F32),
        ],
        compiler_params=_params("arbitrary"),
        name="odd_in",
    )(x, g, win, vn)


def _odd_out_body(x_ref, u_ref, v_ref, ws_ref, b_ref, w_ref, o_ref, us_ref, *, tm, n_prompt_tiles, s):
    is_prompt = pl.program_id(0) < n_prompt_tiles
    ii = lax.broadcasted_iota(jnp.int32, (GMLP_CHUNK, GMLP_CHUNK), 0)
    jj = lax.broadcasted_iota(jnp.int32, (GMLP_CHUNK, GMLP_CHUNK), 1)
    same_stream = jnp.where((ii // s) == (jj // s), 1, 0) + jnp.where(is_prompt, 1, 0)
    keep = (jj <= ii) & (same_stream > 0)
    gdim = u_ref.shape[1] // GMLP_GROUPS
    for gi in range(GMLP_GROUPS):
        cols = slice(gi * gdim, (gi + 1) * gdim)
        wmat = jnp.where(keep, ws_ref[gi], 0.0).astype(BF16)
        bias = b_ref[:, gi:gi + 1]
        for ci in range(tm // GMLP_CHUNK):
            rows = slice(ci * GMLP_CHUNK, (ci + 1) * GMLP_CHUNK)
            sg = _dot(wmat, v_ref[rows, cols]) + bias
            us_ref[rows, cols] = (u_ref[rows, cols].astype(F32) * sg).astype(BF16)
    o_ref[...] = x_ref[...] + _dot(us_ref[...], w_ref[...])


def _odd_out(x, u, v, ws_sel, b_sel, wout, o, lp, s):
    m, d = x.shape
    gate = u.shape[1]
    tm = _tile(math.gcd(lp, m - lp), 512, GMLP_CHUNK)
    npt = lp // tm
    sel = lambda i: jnp.where(i >= npt, 1, 0)
    return pl.pallas_call(
        functools.partial(_odd_out_body, tm=tm, n_prompt_tiles=npt, s=s),
        grid=(m // tm,),
        in_specs=[
            pl.BlockSpec((tm, d), lambda i: (i, 0)),
            pl.BlockSpec((tm, gate), lambda i: (i, 0)),
            pl.BlockSpec((tm, gate), lambda i: (i, 0)),
            pl.BlockSpec((None, None, GMLP_GROUPS, GMLP_CHUNK, GMLP_CHUNK), lambda i: (sel(i), o, 0, 0, 0)),
            pl.BlockSpec((None, None, GMLP_CHUNK, GMLP_GROUPS), lambda i: (sel(i), o, 0, 0)),
            pl.BlockSpec((None,) + wout.shape[1:], lambda i: (o, 0, 0), pipeline_mode=pl.Buffered(1)),
        ],
        out_specs=pl.BlockSpec((tm, d), lambda i: (i, 0)),
        out_shape=jax.ShapeDtypeStruct((m, d), F32),
        scratch_shapes=[pltpu.VMEM((tm, gate), BF16)],
        compiler_params=_params("parallel"),
        name="odd_out",
    )(x, u, v, ws_sel, b_sel, wout)


def kernel(x_prompt, x_sample, cache_mla_ckv, cache_mla_kpe, state_pool, norm_ffn1, norm_mix, norm_ffn2, ffn1_w_gate, ffn1_w_up, ffn1_w_down, ffn2_w_gate, ffn2_w_up, ffn2_w_down, ev_w_in, ev_q_a_norm, ev_kv_a_norm, ev_w_qb, ev_w_kvb, ev_q_nope_norm, ev_q_pe_norm, ev_k_nope_norm, ev_k_pe_norm, ev_pool_w, ev_pool_scale, ev_w_out, od_w_in, od_v_norm, od_w_s, od_b_s, od_w_out):
    bp, lp, d = x_prompt.shape
    b, s, _ = x_sample.shape
    depth = norm_mix.shape[0]
    n_even, n_odd = ev_w_in.shape[0], od_w_in.shape[0]
    past = cache_mla_ckv.shape[2]
    q_lora, kv_lora = ev_q_a_norm.shape[1], ev_kv_a_norm.shape[1]
    pool_dim = ev_pool_scale.shape[1]
    gate = od_v_norm.shape[1]
    m = lp + b * s
    assert bp == 1 and lp % GMLP_CHUNK == 0 and (b * s) % GMLP_CHUNK == 0 and GMLP_CHUNK % s == 0
    assert past % CHUNK == 0 and s <= CHUNK, "every cached and new key must be visible to every sample query"
    assert s >= POOL_HIST and lp >= POOL_HIST and lp % s == 0
    assert ev_w_in.shape[2] == pool_dim + q_lora + kv_lora + QK_ROPE

    vec = lambda a: a[:, None, :]
    bf = lambda a: a.astype(BF16)
    pad_lanes = lambda a: jnp.pad(a, [(0, 0)] * (a.ndim - 1) + [(0, (-a.shape[-1]) % LANES)])

    w_ffn = ((ffn1_w_gate, ffn1_w_up, ffn1_w_down), (ffn2_w_gate, ffn2_w_up, ffn2_w_down))
    g_ffn = (vec(norm_ffn1), vec(norm_ffn2))
    g_mix = vec(norm_mix)
    w_in_e = bf(pad_lanes(ev_w_in))
    wq = ev_w_qb.reshape(n_even, q_lora, MLA_HEADS, QK_HEAD)
    w_qb = bf(jnp.concatenate([wq[..., :QK_NOPE].reshape(n_even, q_lora, -1),
                               wq[..., QK_NOPE:].reshape(n_even, q_lora, -1)], axis=-1))
    w_kvb, w_out_e, pool_w = bf(ev_w_kvb), bf(ev_w_out), bf(ev_pool_w)
    wkv = w_kvb.reshape(n_even, kv_lora, MLA_HEADS, QK_NOPE + V_HEAD)
    w_k = wkv[..., :QK_NOPE].reshape(n_even, kv_lora, -1)
    w_v = wkv[..., QK_NOPE:].reshape(n_even, kv_lora, -1)
    qpn = vec(jnp.tile(ev_q_pe_norm, (1, LANES // QK_ROPE)))
    kpn = vec(pad_lanes(ev_k_pe_norm))
    w_in_o, w_out_o = bf(od_w_in), bf(od_w_out)
    rep = GMLP_CHUNK // s
    ws_sel = jnp.stack([od_w_s, jnp.tile(od_w_s[:, :, :s, :s], (1, 1, rep, rep))])
    b_sel = jnp.stack([od_b_s, jnp.tile(od_b_s[:, :, :s], (1, 1, rep))]).transpose(0, 1, 3, 2)

    cos, sin = _rope_tables(m, lp, past, s)
    x = jnp.concatenate([x_prompt.reshape(lp, d), x_sample.reshape(b * s, d)], axis=0)
    new_ckv, new_kpe, new_u, new_v = [], [], [], []
    for layer in range(depth):
        x = _ffn(x, g_ffn[0], *w_ffn[0], layer)
        if layer % 2 == 0:
            e = layer // 2
            u, q, ckv, kpe = _even_in(x, g_mix, layer, w_in_e, vec(ev_q_a_norm), vec(ev_kv_a_norm), w_qb,
                                      vec(ev_q_nope_norm), vec(ev_k_nope_norm), qpn, kpn, cos, sin, e,
                                      pool_dim, q_lora, kv_lora)
            k_p, v_p = _kv_prompt(ckv, kpe, w_kvb, e, lp)
            attn_p = _attn_prompt(q, k_p, v_p, lp)
            attn_s = _attn_sample(q, cache_mla_ckv, ckv, cache_mla_kpe, kpe, w_k, w_v, e, lp, b, s)
            pool_p, pool_s = _pool(u, state_pool, pool_w, vec(ev_pool_scale), e, lp, b, s, past)
            x = _even_out(x, pool_p, pool_s, attn_p, attn_s, w_out_e, e)
            new_ckv.append((ckv[:lp], ckv[lp:]))
            new_kpe.append((kpe[:lp, :QK_ROPE], kpe[lp:, :QK_ROPE]))
            new_u.append((u[lp - POOL_HIST:lp], u[lp:].reshape(b, s, pool_dim)[:, s - POOL_HIST:]))
        else:
            o = layer // 2
            u, v, v_f32 = _odd_in(x, g_mix, layer, w_in_o, vec(od_v_norm), o, lp)
            x = _odd_out(x, u, v, ws_sel, b_sel, w_out_o, o, lp, s)
            new_v.append(v_f32)
        x = _ffn(x, g_ffn[1], *w_ffn[1], layer)

    stack = lambda pairs, k, shape: jnp.stack([p[k] for p in pairs]).reshape((n_even,) + shape)
    return (
        x[:lp].reshape(1, lp, d),
        x[lp:].reshape(b, s, d),
        stack(new_ckv, 0, (1, lp, kv_lora)),
        stack(new_kpe, 0, (1, lp, QK_ROPE)),
        stack(new_u, 0, (1, POOL_HIST, pool_dim)),
        stack(new_ckv, 1, (b, s, kv_lora)),
        stack(new_kpe, 1, (b, s, QK_ROPE)),
        stack(new_u, 1, (b, POOL_HIST, pool_dim)),
        jnp.stack(new_v).reshape(n_odd, b, s, gate),
    )
```

```python
import functools
import math

import jax
import jax.numpy as jnp
from jax import lax
from jax.experimental import pallas as pl
from jax.experimental.pallas import tpu as pltpu

F32 = jnp.float32
BF16 = jnp.bfloat16

EPS = 1e-6
CHUNK = 64
POOL_WINDOWS = (2, 4, 8, 16)
POOL_HIST = max(POOL_WINDOWS) - 1
HIST_ROWS = 16
MLA_HEADS = 8
QK_NOPE = 128
QK_ROPE = 64
V_HEAD = 128
QK_HEAD = QK_NOPE + QK_ROPE
ATTN_SCALE = QK_HEAD ** -0.5
Q_SCALE = ATTN_SCALE * math.log2(math.e)
ROPE_THETA = 10000.0
GMLP_CHUNK = 128
GMLP_GROUPS = 8
LANES = 128
NEG_BIG = -1e30
VMEM_LIMIT = 60 * 1024 * 1024


def _params(*sem):
    return pltpu.CompilerParams(dimension_semantics=sem, vmem_limit_bytes=VMEM_LIMIT)


def _tile(n, pref, mult=8):
    if n <= pref:
        return n
    for t in range(pref - pref % mult, 0, -mult):
        if n % t == 0:
            return t
    raise ValueError(f"no tile for {n}")


def _const_spec(shape):
    nd = len(shape)
    return pl.BlockSpec(shape, lambda *_: (0,) * nd, pipeline_mode=pl.Buffered(1))


def _rms(x, g):
    ms = jnp.mean(x * x, axis=-1, keepdims=True)
    return x * lax.rsqrt(ms + EPS) * g


def _dot(a, b):
    return jnp.dot(a, b, preferred_element_type=F32)


def _dot_nt(a, b):
    return lax.dot_general(a, b, (((1,), (1,)), ((), ())), preferred_element_type=F32)


def _ffn_body(x_ref, g_ref, wg_ref, wu_ref, wd_ref, *rest, convert_next):
    if convert_next:
        (*nxt, o_ref, cg_ref, cu_ref, cd_ref, h_ref) = rest
        for src, dst in zip(nxt, (cg_ref, cu_ref, cd_ref)):
            dst[...] = src[...].astype(BF16)
    else:
        o_ref, h_ref = rest

    @pl.when(pl.program_id(1) == 0)
    def _():
        x = x_ref[...]
        h_ref[...] = _rms(x, g_ref[...]).astype(BF16)
        o_ref[...] = x

    h = h_ref[...]
    a = _dot(h, wg_ref[...])
    b = _dot(h, wu_ref[...])
    act = (a * jax.nn.sigmoid(a) * b).astype(BF16)
    o_ref[...] += 0.5 * _dot(act, wd_ref[...])


def _ffn(x, g, layer, w, w_next=None, next_layer=None):
    m, d = x.shape
    f = w[0].shape[-1]
    tm, tf = _tile(m, 1024), _tile(f, 512, LANES)
    n_i = m // tm
    in_specs = [
        pl.BlockSpec((tm, d), lambda i, j: (i, 0)),
        pl.BlockSpec((None, 1, d), lambda i, j: (layer, 0, 0)),
        pl.BlockSpec((d, tf), lambda i, j: (0, j)),
        pl.BlockSpec((d, tf), lambda i, j: (0, j)),
        pl.BlockSpec((tf, d), lambda i, j: (j, 0)),
    ]
    out_specs = [pl.BlockSpec((tm, d), lambda i, j: (i, 0))]
    out_shape = [jax.ShapeDtypeStruct((m, d), F32)]
    if w_next is not None:
        nr = max(r for r in (1, 2, 4, 8, 16) if r <= n_i and d % (r * LANES) == 0)
        db = d // nr
        blk = lambda i: jnp.minimum(i, nr - 1)
        in_specs += [
            pl.BlockSpec((None, db, tf), lambda i, j: (next_layer, blk(i), j)),
            pl.BlockSpec((None, db, tf), lambda i, j: (next_layer, blk(i), j)),
            pl.BlockSpec((None, tf, db), lambda i, j: (next_layer, j, blk(i))),
        ]
        out_specs += [
            pl.BlockSpec((db, tf), lambda i, j: (blk(i), j)),
            pl.BlockSpec((db, tf), lambda i, j: (blk(i), j)),
            pl.BlockSpec((tf, db), lambda i, j: (j, blk(i))),
        ]
        out_shape += [jax.ShapeDtypeStruct((d, f), BF16), jax.ShapeDtypeStruct((d, f), BF16),
                      jax.ShapeDtypeStruct((f, d), BF16)]
    out = pl.pallas_call(
        functools.partial(_ffn_body, convert_next=w_next is not None),
        grid=(n_i, f // tf),
        in_specs=in_specs,
        out_specs=out_specs,
        out_shape=out_shape,
        scratch_shapes=[pltpu.VMEM((tm, d), BF16)],
        compiler_params=_params("arbitrary", "arbitrary"),
        name="ffn",
    )(x, g, *w, *(w_next or ()))
    return out[0], tuple(out[1:])


def _rope_tab_body(inv_ref, sgn_ref, cos_ref, sin_ref, *, tm, lp, past, s):
    r = lax.broadcasted_iota(jnp.int32, (tm, LANES), 0) + pl.program_id(0) * tm
    t = r - lp
    t = (t & (s - 1)) if s & (s - 1) == 0 else lax.rem(t, s)
    pos = jnp.where(r >= lp, past + t, r).astype(F32)
    ang = pos * inv_ref[...]
    cos_ref[...] = jnp.cos(ang)
    sin_ref[...] = jnp.sin(ang) * sgn_ref[...]


def _rope_tables(m, lp, past, s):
    half = QK_ROPE // 2
    inv = ROPE_THETA ** (-jnp.arange(0, QK_ROPE, 2, dtype=F32) / QK_ROPE)
    inv = jnp.tile(inv, LANES // half)[None, :]
    sgn = jnp.tile(jnp.concatenate([-jnp.ones((half,), F32), jnp.ones((half,), F32)]), LANES // QK_ROPE)[None, :]
    tm = _tile(m, 512)
    return pl.pallas_call(
        functools.partial(_rope_tab_body, tm=tm, lp=lp, past=past, s=s),
        grid=(m // tm,),
        in_specs=[_const_spec((1, LANES)), _const_spec((1, LANES))],
        out_specs=[pl.BlockSpec((tm, LANES), lambda i: (i, 0))] * 2,
        out_shape=[jax.ShapeDtypeStruct((m, LANES), F32)] * 2,
        compiler_params=_params("parallel"),
        name="rope_tables",
    )(inv, sgn)


def _swap_halves(x):
    lane = lax.broadcasted_iota(jnp.int32, x.shape, 1)
    left = pltpu.roll(x, LANES - QK_ROPE // 2, 1)
    right = pltpu.roll(x, QK_ROPE // 2, 1)
    return jnp.where((lane & (QK_ROPE - 1)) < QK_ROPE // 2, left, right)


def _even_in_body(x_ref, g_ref, win_ref, qan_ref, kvan_ref, wqb_ref, qnn_ref, knn_ref, qpn_ref, kpn_ref,
                  cos_ref, sin_ref, u_ref, q_ref, ckv_ref, kpe_ref, *, pool_dim, q_lora, kv_lora):
    h = _rms(x_ref[...], g_ref[...]).astype(BF16)
    z = _dot(h, win_ref[...])
    u_ref[...] = z[:, :pool_dim]
    o1, o2 = pool_dim + q_lora, pool_dim + q_lora + kv_lora
    ckv_ref[...] = _rms(z[:, o1:o2], kvan_ref[...])
    cos, sin = cos_ref[...], sin_ref[...]
    lane = lax.broadcasted_iota(jnp.int32, cos.shape, 1)
    lo = lane < QK_ROPE

    def rope(t):
        return t * cos + _swap_halves(t) * sin

    kp = z[:, o2:o2 + LANES]
    kp = kp * lax.rsqrt(jnp.sum(kp * kp, axis=-1, keepdims=True) / QK_ROPE + EPS) * kpn_ref[...]
    kpe_ref[...] = rope(kp)

    qn = _rms(z[:, pool_dim:o1], qan_ref[...]).astype(BF16)
    q = _dot(qn, wqb_ref[...])
    nope_w = MLA_HEADS * QK_NOPE
    k_gain = knn_ref[...] * Q_SCALE
    for hd in range(MLA_HEADS):
        qh = _rms(q[:, hd * QK_NOPE:(hd + 1) * QK_NOPE], qnn_ref[...]) * k_gain
        q_ref[hd, :, :QK_NOPE] = qh.astype(BF16)
    for j in range(MLA_HEADS // 2):
        t = q[:, nope_w + j * LANES:nope_w + (j + 1) * LANES]
        tt = t * t
        s_lo = jnp.sum(jnp.where(lo, tt, 0.0), axis=-1, keepdims=True)
        s_hi = jnp.sum(jnp.where(lo, 0.0, tt), axis=-1, keepdims=True)
        inv = jnp.where(lo, lax.rsqrt(s_lo / QK_ROPE + EPS), lax.rsqrt(s_hi / QK_ROPE + EPS))
        t = rope(t * inv * qpn_ref[...]) * Q_SCALE
        q_ref[2 * j, :, QK_NOPE:] = jnp.where(lo, t, 0.0).astype(BF16)
        q_ref[2 * j + 1, :, QK_NOPE:] = jnp.where(lo, 0.0, t).astype(BF16)


def _even_in(x, g, layer, win, qan, kvan, wqb, qnn, knn, qpn, kpn, cos, sin, e, pool_dim, q_lora, kv_lora):
    m, d = x.shape
    tm = _tile(m, 512)
    n_in = win.shape[-1]
    n_q = wqb.shape[-1]
    row = lambda w: pl.BlockSpec((tm, w), lambda i: (i, 0))
    vec = lambda w, idx: pl.BlockSpec((None, 1, w), lambda i: (idx, 0, 0))
    return pl.pallas_call(
        functools.partial(_even_in_body, pool_dim=pool_dim, q_lora=q_lora, kv_lora=kv_lora),
        grid=(m // tm,),
        in_specs=[
            row(d), vec(d, layer),
            pl.BlockSpec((None, d, n_in), lambda i: (e, 0, 0), pipeline_mode=pl.Buffered(1)),
            vec(q_lora, e), vec(kv_lora, e),
            pl.BlockSpec((None, q_lora, n_q), lambda i: (e, 0, 0), pipeline_mode=pl.Buffered(1)),
            vec(QK_NOPE, e), vec(QK_NOPE, e), vec(LANES, e), vec(LANES, e),
            row(LANES), row(LANES),
        ],
        out_specs=[
            row(pool_dim),
            pl.BlockSpec((MLA_HEADS, tm, 2 * LANES), lambda i: (0, i, 0)),
            row(kv_lora), row(LANES),
        ],
        out_shape=[
            jax.ShapeDtypeStruct((m, pool_dim), F32),
            jax.ShapeDtypeStruct((MLA_HEADS, m, 2 * LANES), BF16),
            jax.ShapeDtypeStruct((m, kv_lora), F32),
            jax.ShapeDtypeStruct((m, LANES), F32),
        ],
        compiler_params=_params("parallel"),
        name="even_in",
    )(x, g, win, qan, kvan, wqb, qnn, knn, qpn, kpn, cos, sin)


def _kv_body(ckv_ref, kpe_ref, w_ref, k_ref, v_ref):
    c = ckv_ref[...].astype(BF16)
    kp = kpe_ref[...]
    kp_both = (kp + pltpu.roll(kp, QK_ROPE, 1)).astype(BF16)
    hw = QK_NOPE + V_HEAD
    for hd in range(MLA_HEADS):
        kv = _dot(c, w_ref[:, hd * hw:(hd + 1) * hw])
        k_ref[hd, :, :QK_NOPE] = _rms(kv[:, :QK_NOPE], 1.0).astype(BF16)
        k_ref[hd, :, QK_NOPE:] = kp_both
        v_ref[hd, :, :V_HEAD] = kv[:, QK_NOPE:].astype(BF16)
        v_ref[hd, :, V_HEAD:] = jnp.ones((c.shape[0], LANES), BF16)


def _kv_prompt(ckv, kpe, wkvb, e, lp):
    kv_lora = ckv.shape[-1]
    tm = _tile(lp, 512)
    return pl.pallas_call(
        _kv_body,
        grid=(lp // tm,),
        in_specs=[
            pl.BlockSpec((tm, kv_lora), lambda i: (i, 0)),
            pl.BlockSpec((tm, LANES), lambda i: (i, 0)),
            pl.BlockSpec((None,) + wkvb.shape[1:], lambda i: (e, 0, 0), pipeline_mode=pl.Buffered(1)),
        ],
        out_specs=[
            pl.BlockSpec((MLA_HEADS, tm, 2 * LANES), lambda i: (0, i, 0)),
            pl.BlockSpec((MLA_HEADS, tm, V_HEAD + LANES), lambda i: (0, i, 0)),
        ],
        out_shape=[
            jax.ShapeDtypeStruct((MLA_HEADS, lp, 2 * LANES), BF16),
            jax.ShapeDtypeStruct((MLA_HEADS, lp, V_HEAD + LANES), BF16),
        ],
        compiler_params=_params("parallel"),
        name="kv_prompt",
    )(ckv, kpe, wkvb)


def _flash_body(q_ref, k_ref, v_ref, o_ref, sa_ref, sb_ref, m_ref, acc_ref, *, t):
    i = pl.program_id(1)
    q = q_ref[...]
    m_ref[...] = jnp.full(m_ref.shape, NEG_BIG, F32)
    acc_ref[...] = jnp.zeros(acc_ref.shape, F32)

    def scores(j):
        return _dot_nt(q, k_ref[pl.ds(pl.multiple_of(j * t, t), t), :])

    def update(s_ref, j, diagonal=False):
        s = s_ref[...]
        if diagonal:
            qc = lax.broadcasted_iota(jnp.int32, s.shape, 0) // CHUNK
            kc = lax.broadcasted_iota(jnp.int32, s.shape, 1) // CHUNK
            s = jnp.where(kc <= qc, s, NEG_BIG)
        m_old = m_ref[...]
        m_new = jnp.maximum(m_old, jnp.max(s, axis=-1, keepdims=True))
        p = jnp.exp2(s - jnp.tile(m_new, (1, t // LANES)))
        alpha = jnp.tile(jnp.exp2(m_old - m_new), (1, acc_ref.shape[1] // LANES))
        pv = _dot(p.astype(BF16), v_ref[pl.ds(pl.multiple_of(j * t, t), t), :])
        acc_ref[...] = alpha * acc_ref[...] + pv
        m_ref[...] = m_new

    sa_ref[...] = scores(0)

    def pair(jj, carry):
        j = 2 * jj
        sb_ref[...] = scores(j + 1)
        update(sa_ref, j)
        sa_ref[...] = scores(j + 2)
        update(sb_ref, j + 1)
        return carry

    lax.fori_loop(0, i // 2, pair, 0)

    @pl.when(i % 2 == 0)
    def _():
        update(sa_ref, i, diagonal=True)

    @pl.when(i % 2 == 1)
    def _():
        sb_ref[...] = scores(i)
        update(sa_ref, i - 1)
        update(sb_ref, i, diagonal=True)

    acc = acc_ref[...]
    o_ref[...] = (acc[:, :V_HEAD] / acc[:, V_HEAD:]).astype(BF16)


def _attn_prompt(q, k, v, lp):
    t = _tile(lp, 1024, LANES)
    return pl.pallas_call(
        functools.partial(_flash_body, t=t),
        grid=(MLA_HEADS, lp // t),
        in_specs=[
            pl.BlockSpec((None, t, 2 * LANES), lambda h, i: (h, i, 0)),
            pl.BlockSpec((None, lp, 2 * LANES), lambda h, i: (h, 0, 0)),
            pl.BlockSpec((None, lp, V_HEAD + LANES), lambda h, i: (h, 0, 0)),
        ],
        out_specs=pl.BlockSpec((t, V_HEAD), lambda h, i: (i, h)),
        out_shape=jax.ShapeDtypeStruct((lp, MLA_HEADS * V_HEAD), BF16),
        scratch_shapes=[pltpu.VMEM((t, t), F32), pltpu.VMEM((t, t), F32),
                        pltpu.VMEM((t, LANES), F32), pltpu.VMEM((t, V_HEAD + LANES), F32)],
        compiler_params=_params("parallel", "arbitrary"),
        name="attn_prompt",
    )(q, k, v)


def _attn_sample_body(q_ref, cc_ref, cn_ref, pc_ref, pn_ref, wk_ref, wv_ref, o_ref, c_sc, kp_sc, p_sc,
                      *, past, s, lk):
    c_sc[:past] = cc_ref[...].astype(BF16)
    c_sc[past:past + s] = cn_ref[...].astype(BF16)
    pc, pn = pc_ref[...], pn_ref[...]
    kp_sc[:past] = jnp.concatenate([pc, pc], axis=1).astype(BF16)
    kp_sc[past:past + s] = (pn + pltpu.roll(pn, QK_ROPE, 1)).astype(BF16)
    if lk > past + s:
        c_sc[past + s:] = jnp.zeros((lk - past - s, c_sc.shape[1]), BF16)
        kp_sc[past + s:] = jnp.zeros((lk - past - s, LANES), BF16)
    c = c_sc[...]
    valid = lax.broadcasted_iota(jnp.int32, (s, lk), 1) < past + s
    ones = jnp.ones((8, QK_NOPE), BF16)
    n_ch = 4 if lk % 64 == 0 else 1
    ch = lk // n_ch
    pe = _dot_nt(jnp.concatenate([q_ref[hd][:, QK_NOPE:] for hd in range(MLA_HEADS)], axis=0), kp_sc[...])
    inv_l = []
    for pair in range(MLA_HEADS // 2):
        wk = wk_ref[:, pair * 2 * QK_NOPE:(pair + 1) * 2 * QK_NOPE]
        kk = jnp.concatenate([_dot(c[r * ch:(r + 1) * ch], wk) for r in range(n_ch)])
        for hd in (2 * pair, 2 * pair + 1):
            kn = kk[:, (hd % 2) * QK_NOPE:(hd % 2 + 1) * QK_NOPE]
            ms = _dot_nt(ones, (kn * kn).astype(BF16))[0:1] / QK_NOPE
            sc = _dot_nt(q_ref[hd][:, :QK_NOPE], kn.astype(BF16)) * lax.rsqrt(ms + EPS) + pe[hd * s:(hd + 1) * s]
            sc = jnp.where(valid, sc, NEG_BIG)
            p = jnp.exp2(sc - jnp.max(sc, axis=-1, keepdims=True))
            inv_l.append(1.0 / jnp.sum(p, axis=-1, keepdims=True))
            p_sc[hd * s:(hd + 1) * s] = p.astype(BF16)
    ctx = _dot(p_sc[...], c)
    for hd in range(MLA_HEADS):
        ctx_h = (ctx[hd * s:(hd + 1) * s] * inv_l[hd]).astype(BF16)
        o_ref[:, hd * V_HEAD:(hd + 1) * V_HEAD] = _dot(ctx_h, wv_ref[:, hd * V_HEAD:(hd + 1) * V_HEAD]).astype(BF16)


def _attn_sample(q, ckv_cache, ckv, kpe_cache, kpe, wk, wv, e, lp, b, s):
    past, kv_lora = ckv_cache.shape[2], ckv_cache.shape[3]
    lk = past + -(-s // LANES) * LANES
    row0 = lp // s
    return pl.pallas_call(
        functools.partial(_attn_sample_body, past=past, s=s, lk=lk),
        grid=(b,),
        in_specs=[
            pl.BlockSpec((MLA_HEADS, s, 2 * LANES), lambda i: (0, row0 + i, 0)),
            pl.BlockSpec((None, None, past, kv_lora), lambda i: (e, i, 0, 0)),
            pl.BlockSpec((s, kv_lora), lambda i: (row0 + i, 0)),
            pl.BlockSpec((None, None, past, QK_ROPE), lambda i: (e, i, 0, 0)),
            pl.BlockSpec((s, LANES), lambda i: (row0 + i, 0)),
            pl.BlockSpec((None,) + wk.shape[1:], lambda i: (e, 0, 0), pipeline_mode=pl.Buffered(1)),
            pl.BlockSpec((None,) + wv.shape[1:], lambda i: (e, 0, 0), pipeline_mode=pl.Buffered(1)),
        ],
        out_specs=pl.BlockSpec((s, MLA_HEADS * V_HEAD), lambda i: (i, 0)),
        out_shape=jax.ShapeDtypeStruct((b * s, MLA_HEADS * V_HEAD), BF16),
        scratch_shapes=[pltpu.VMEM((lk, kv_lora), BF16), pltpu.VMEM((lk, LANES), BF16),
                        pltpu.VMEM((MLA_HEADS * s, lk), BF16)],
        compiler_params=_params("parallel"),
        name="attn_sample",
    )(q, ckv_cache, ckv, kpe_cache, kpe, wk, wv)


def _pool_segment(ext_ref, pooled_ref, row0, n, pos0, gdim):
    pos = (lax.broadcasted_iota(jnp.int32, (n, 1), 0) + pos0).astype(F32)
    for gi, w in enumerate(POOL_WINDOWS):
        cols = slice(gi * gdim, (gi + 1) * gdim)
        u = ext_ref[HIST_ROWS:HIST_ROWS + n, cols]
        acc = u
        for k in range(1, w):
            acc = acc + ext_ref[HIST_ROWS - k:HIST_ROWS - k + n, cols]
        cnt = jnp.minimum(jnp.float32(w), pos + 1.0)
        pooled_ref[row0:row0 + n, cols] = (acc / cnt - u).astype(BF16)


def _pool_project(pooled_ref, pw_ref, sc_ref, o_ref, gdim):
    for gi in range(len(POOL_WINDOWS)):
        cols = slice(gi * gdim, (gi + 1) * gdim)
        o_ref[:, cols] = (_dot(pooled_ref[:, cols], pw_ref[gi]) * sc_ref[:, cols]).astype(BF16)


def _pool_prompt_body(u_ref, pw_ref, sc_ref, o_ref, ext_ref, pooled_ref, *, tm, gdim):
    i = pl.program_id(0)

    @pl.when(i == 0)
    def _():
        ext_ref[:HIST_ROWS] = jnp.zeros((HIST_ROWS, ext_ref.shape[1]), F32)

    ext_ref[HIST_ROWS:] = u_ref[...]
    _pool_segment(ext_ref, pooled_ref, 0, tm, i * tm, gdim)
    _pool_project(pooled_ref, pw_ref, sc_ref, o_ref, gdim)
    ext_ref[:HIST_ROWS] = ext_ref[tm:tm + HIST_ROWS]


def _pool_sample_body(u_ref, hist_ref, pw_ref, sc_ref, o_ref, ext_ref, pooled_ref, *, nb, s, past, gdim):
    for bi in range(nb):
        ext = ext_ref.at[bi]
        ext[0:1] = jnp.zeros((1, ext_ref.shape[2]), F32)
        ext[1:HIST_ROWS] = hist_ref[bi]
        ext[HIST_ROWS:] = u_ref[bi * s:(bi + 1) * s]
        _pool_segment(ext, pooled_ref, bi * s, s, past, gdim)
    _pool_project(pooled_ref, pw_ref, sc_ref, o_ref, gdim)


def _pool(u, state_pool, pw, scale, e, lp, b, s, past):
    m, pool_dim = u.shape
    ng = len(POOL_WINDOWS)
    gdim = pool_dim // ng
    w_spec = pl.BlockSpec((None, ng, gdim, gdim), lambda i: (e, 0, 0, 0))
    s_spec = pl.BlockSpec((None, 1, pool_dim), lambda i: (e, 0, 0))
    tm = _tile(lp, 512, HIST_ROWS)
    out_p = pl.pallas_call(
        functools.partial(_pool_prompt_body, tm=tm, gdim=gdim),
        grid=(lp // tm,),
        in_specs=[pl.BlockSpec((tm, pool_dim), lambda i: (i, 0)), w_spec, s_spec],
        out_specs=pl.BlockSpec((tm, pool_dim), lambda i: (i, 0)),
        out_shape=jax.ShapeDtypeStruct((lp, pool_dim), BF16),
        scratch_shapes=[pltpu.VMEM((HIST_ROWS + tm, pool_dim), F32), pltpu.VMEM((tm, pool_dim), BF16)],
        compiler_params=_params("arbitrary"),
        name="pool_prompt",
    )(u, pw, scale)
    nb = math.gcd(b, max(1, 256 // s))
    row0 = lp // (nb * s)
    out_s = pl.pallas_call(
        functools.partial(_pool_sample_body, nb=nb, s=s, past=past, gdim=gdim),
        grid=(b // nb,),
        in_specs=[
            pl.BlockSpec((nb * s, pool_dim), lambda i: (row0 + i, 0)),
            pl.BlockSpec((None, nb, POOL_HIST, pool_dim), lambda i: (e, i, 0, 0)),
            w_spec, s_spec,
        ],
        out_specs=pl.BlockSpec((nb * s, pool_dim), lambda i: (i, 0)),
        out_shape=jax.ShapeDtypeStruct((b * s, pool_dim), BF16),
        scratch_shapes=[pltpu.VMEM((nb, HIST_ROWS + s, pool_dim), F32), pltpu.VMEM((nb * s, pool_dim), BF16)],
        compiler_params=_params("parallel"),
        name="pool_sample",
    )(u, state_pool, pw, scale)
    return out_p, out_s


def _even_out_body(x_ref, pp_ref, ps_ref, ap_ref, as_ref, w_ref, o_ref, *, pool_dim, n_prompt_tiles):
    def project(p_ref, a_ref):
        o_ref[...] = x_ref[...] + _dot(p_ref[...], w_ref[:pool_dim]) + _dot(a_ref[...], w_ref[pool_dim:])

    is_prompt = pl.program_id(0) < n_prompt_tiles
    pl.when(is_prompt)(lambda: project(pp_ref, ap_ref))
    pl.when(jnp.logical_not(is_prompt))(lambda: project(ps_ref, as_ref))


def _even_out(x, pool_p, pool_s, attn_p, attn_s, wout, e):
    m, d = x.shape
    lp, pool_dim = pool_p.shape
    attn_dim = attn_p.shape[1]
    tm = _tile(math.gcd(lp, m - lp), 512)
    npt = lp // tm
    prompt_rows = lambda w: pl.BlockSpec((tm, w), lambda i: (jnp.minimum(i, npt - 1), 0))
    sample_rows = lambda w: pl.BlockSpec((tm, w), lambda i: (jnp.maximum(i - npt, 0), 0))
    return pl.pallas_call(
        functools.partial(_even_out_body, pool_dim=pool_dim, n_prompt_tiles=npt),
        grid=(m // tm,),
        in_specs=[
            pl.BlockSpec((tm, d), lambda i: (i, 0)),
            prompt_rows(pool_dim), sample_rows(pool_dim), prompt_rows(attn_dim), sample_rows(attn_dim),
            pl.BlockSpec((None,) + wout.shape[1:], lambda i: (e, 0, 0), pipeline_mode=pl.Buffered(1)),
        ],
        out_specs=pl.BlockSpec((tm, d), lambda i: (i, 0)),
        out_shape=jax.ShapeDtypeStruct((m, d), F32),
        compiler_params=_params("arbitrary"),
        name="even_out",
    )(x, pool_p, pool_s, attn_p, attn_s, wout)


def _odd_in_body(x_ref, g_ref, w_ref, vn_ref, u_ref, v_ref, vf_ref, *, gate):
    h = _rms(x_ref[...], g_ref[...]).astype(BF16)
    u_ref[...] = jax.nn.gelu(_dot(h, w_ref[:, :gate])).astype(BF16)
    v = _rms(jax.nn.gelu(_dot(h, w_ref[:, gate:])), vn_ref[...])
    v_ref[...] = v.astype(BF16)
    vf_ref[...] = v


def _odd_in(x, g, layer, win, vn, o, lp):
    m, d = x.shape
    gate = win.shape[-1] // 2
    tm = _tile(math.gcd(lp, m - lp), 512)
    first = lp // tm
    return pl.pallas_call(
        functools.partial(_odd_in_body, gate=gate),
        grid=(m // tm,),
        in_specs=[
            pl.BlockSpec((tm, d), lambda i: (i, 0)),
            pl.BlockSpec((None, 1, d), lambda i: (layer, 0, 0)),
            pl.BlockSpec((None, d, 2 * gate), lambda i: (o, 0, 0), pipeline_mode=pl.Buffered(1)),
            pl.BlockSpec((None, 1, gate), lambda i: (o, 0, 0)),
        ],
        out_specs=[
            pl.BlockSpec((tm, gate), lambda i: (i, 0)),
            pl.BlockSpec((tm, gate), lambda i: (i, 0)),
            pl.BlockSpec((tm, gate), lambda i: (jnp.maximum(i - first, 0), 0)),
        ],
        out_shape=[
            jax.ShapeDtypeStruct((m, gate), BF16),
            jax.ShapeDtypeStruct((m, gate), BF16),
            jax.ShapeDtypeStruct((m - lp, gate), F32),
        ],
        compiler_params=_params("arbitrary"),
        name="odd_in",
    )(x, g, win, vn)


def _odd_out_body(x_ref, u_ref, v_ref, ws_ref, b_ref, w_ref, o_ref, us_ref, *, tm, n_prompt_tiles, s):
    is_prompt = pl.program_id(0) < n_prompt_tiles
    ii = lax.broadcasted_iota(jnp.int32, (GMLP_CHUNK, GMLP_CHUNK), 0)
    jj = lax.broadcasted_iota(jnp.int32, (GMLP_CHUNK, GMLP_CHUNK), 1)
    same_stream = jnp.where((ii // s) == (jj // s), 1, 0) + jnp.where(is_prompt, 1, 0)
    keep = (jj <= ii) & (same_stream > 0)
    gdim = u_ref.shape[1] // GMLP_GROUPS
    for gi in range(GMLP_GROUPS):
        cols = slice(gi * gdim, (gi + 1) * gdim)
        wmat = jnp.where(keep, ws_ref[gi], 0.0).astype(BF16)
        bias = b_ref[:, gi:gi + 1]
        for ci in range(tm // GMLP_CHUNK):
            rows = slice(ci * GMLP_CHUNK, (ci + 1) * GMLP_CHUNK)
            sg = _dot(wmat, v_ref[rows, cols]) + bias
            us_ref[rows, cols] = (u_ref[rows, cols].astype(F32) * sg).astype(BF16)
    o_ref[...] = x_ref[...] + _dot(us_ref[...], w_ref[...])


def _odd_out(x, u, v, ws_sel, b_sel, wout, o, lp, s):
    m, d = x.shape
    gate = u.shape[1]
    tm = _tile(math.gcd(lp, m - lp), 512, GMLP_CHUNK)
    npt = lp // tm
    sel = lambda i: jnp.where(i >= npt, 1, 0)
    return pl.pallas_call(
        functools.partial(_odd_out_body, tm=tm, n_prompt_tiles=npt, s=s),
        grid=(m // tm,),
        in_specs=[
            pl.BlockSpec((tm, d), lambda i: (i, 0)),
            pl.BlockSpec((tm, gate), lambda i: (i, 0)),
            pl.BlockSpec((tm, gate), lambda i: (i, 0)),
            pl.BlockSpec((None, None, GMLP_GROUPS, GMLP_CHUNK, GMLP_CHUNK), lambda i: (sel(i), o, 0, 0, 0)),
            pl.BlockSpec((None, None, GMLP_CHUNK, GMLP_GROUPS), lambda i: (sel(i), o, 0, 0)),
            pl.BlockSpec((None,) + wout.shape[1:], lambda i: (o, 0, 0), pipeline_mode=pl.Buffered(1)),
        ],
        out_specs=pl.BlockSpec((tm, d), lambda i: (i, 0)),
        out_shape=jax.ShapeDtypeStruct((m, d), F32),
        scratch_shapes=[pltpu.VMEM((tm, gate), BF16)],
        compiler_params=_params("parallel"),
        name="odd_out",
    )(x, u, v, ws_sel, b_sel, wout)


def kernel(x_prompt, x_sample, cache_mla_ckv, cache_mla_kpe, state_pool, norm_ffn1, norm_mix, norm_ffn2, ffn1_w_gate, ffn1_w_up, ffn1_w_down, ffn2_w_gate, ffn2_w_up, ffn2_w_down, ev_w_in, ev_q_a_norm, ev_kv_a_norm, ev_w_qb, ev_w_kvb, ev_q_nope_norm, ev_q_pe_norm, ev_k_nope_norm, ev_k_pe_norm, ev_pool_w, ev_pool_scale, ev_w_out, od_w_in, od_v_norm, od_w_s, od_b_s, od_w_out):
    bp, lp, d = x_prompt.shape
    b, s, _ = x_sample.shape
    depth = norm_mix.shape[0]
    n_even, n_odd = ev_w_in.shape[0], od_w_in.shape[0]
    past = cache_mla_ckv.shape[2]
    q_lora, kv_lora = ev_q_a_norm.shape[1], ev_kv_a_norm.shape[1]
    pool_dim = ev_pool_scale.shape[1]
    gate = od_v_norm.shape[1]
    m = lp + b * s
    assert bp == 1 and lp % GMLP_CHUNK == 0 and (b * s) % GMLP_CHUNK == 0 and GMLP_CHUNK % s == 0
    assert past % CHUNK == 0 and s <= CHUNK, "every cached and new key must be visible to every sample query"
    assert s >= POOL_HIST and lp >= POOL_HIST and lp % s == 0
    assert ev_w_in.shape[2] == pool_dim + q_lora + kv_lora + QK_ROPE

    vec = lambda a: a[:, None, :]
    bf = lambda a: a.astype(BF16)
    pad_lanes = lambda a: jnp.pad(a, [(0, 0)] * (a.ndim - 1) + [(0, (-a.shape[-1]) % LANES)])

    w_ffn = ((ffn1_w_gate, ffn1_w_up, ffn1_w_down), (ffn2_w_gate, ffn2_w_up, ffn2_w_down))
    g_ffn = (vec(norm_ffn1), vec(norm_ffn2))
    g_mix = vec(norm_mix)
    w_in_e = bf(pad_lanes(ev_w_in))
    wq = ev_w_qb.reshape(n_even, q_lora, MLA_HEADS, QK_HEAD)
    w_qb = bf(jnp.concatenate([wq[..., :QK_NOPE].reshape(n_even, q_lora, -1),
                               wq[..., QK_NOPE:].reshape(n_even, q_lora, -1)], axis=-1))
    w_kvb, w_out_e, pool_w = bf(ev_w_kvb), bf(ev_w_out), bf(ev_pool_w)
    wkv = w_kvb.reshape(n_even, kv_lora, MLA_HEADS, QK_NOPE + V_HEAD)
    w_k = wkv[..., :QK_NOPE].reshape(n_even, kv_lora, -1)
    w_v = wkv[..., QK_NOPE:].reshape(n_even, kv_lora, -1)
    qpn = vec(jnp.tile(ev_q_pe_norm, (1, LANES // QK_ROPE)))
    kpn = vec(pad_lanes(ev_k_pe_norm))
    w_in_o, w_out_o = bf(od_w_in), bf(od_w_out)
    rep = GMLP_CHUNK // s
    ws_sel = jnp.stack([od_w_s, jnp.tile(od_w_s[:, :, :s, :s], (1, 1, rep, rep))])
    b_sel = jnp.stack([od_b_s, jnp.tile(od_b_s[:, :, :s], (1, 1, rep))]).transpose(0, 1, 3, 2)

    cos, sin = _rope_tables(m, lp, past, s)
    x = jnp.concatenate([x_prompt.reshape(lp, d), x_sample.reshape(b * s, d)], axis=0)
    new_ckv, new_kpe, new_u, new_v = [], [], [], []
    w_cur = tuple(bf(w[0]) for w in w_ffn[0])
    for layer in range(depth):
        x, w_cur = _ffn(x, g_ffn[0], layer, w_cur, w_ffn[1], layer)
        if layer % 2 == 0:
            e = layer // 2
            u, q, ckv, kpe = _even_in(x, g_mix, layer, w_in_e, vec(ev_q_a_norm), vec(ev_kv_a_norm), w_qb,
                                      vec(ev_q_nope_norm), vec(ev_k_nope_norm), qpn, kpn, cos, sin, e,
                                      pool_dim, q_lora, kv_lora)
            k_p, v_p = _kv_prompt(ckv, kpe, w_kvb, e, lp)
            attn_p = _attn_prompt(q, k_p, v_p, lp)
            attn_s = _attn_sample(q, cache_mla_ckv, ckv, cache_mla_kpe, kpe, w_k, w_v, e, lp, b, s)
            pool_p, pool_s = _pool(u, state_pool, pool_w, vec(ev_pool_scale), e, lp, b, s, past)
            x = _even_out(x, pool_p, pool_s, attn_p, attn_s, w_out_e, e)
            new_ckv.append((ckv[:lp], ckv[lp:]))
            new_kpe.append((kpe[:lp, :QK_ROPE], kpe[lp:, :QK_ROPE]))
            new_u.append((u[lp - POOL_HIST:lp], u[lp:].reshape(b, s, pool_dim)[:, s - POOL_HIST:]))
        else:
            o = layer // 2
            u, v, v_f32 = _odd_in(x, g_mix, layer, w_in_o, vec(od_v_norm), o, lp)
            x = _odd_out(x, u, v, ws_sel, b_sel, w_out_o, o, lp, s)
            new_v.append(v_f32)
        if layer + 1 < depth:
            x, w_cur = _ffn(x, g_ffn[1], layer, w_cur, w_ffn[0], layer + 1)
        else:
            x, _ = _ffn(x, g_ffn[1], layer, w_cur)

    stack = lambda pairs, k, shape: jnp.stack([p[k] for p in pairs]).reshape((n_even,) + shape)
    return (
        x[:lp].reshape(1, lp, d),
        x[lp:].reshape(b, s, d),
        stack(new_ckv, 0, (1, lp, kv_lora)),
        stack(new_kpe, 0, (1, lp, QK_ROPE)),
        stack(new_u, 0, (1, POOL_HIST, pool_dim)),
        stack(new_ckv, 1, (b, s, kv_lora)),
        stack(new_kpe, 1, (b, s, QK_ROPE)),
        stack(new_u, 1, (b, POOL_HIST, pool_dim)),
        jnp.stack(new_v).reshape(n_odd, b, s, gate),
    )
```

```python
import functools
import math

import jax
import jax.numpy as jnp
from jax import lax
from jax.experimental import pallas as pl
from jax.experimental.pallas import tpu as pltpu

F32 = jnp.float32
BF16 = jnp.bfloat16

EPS = 1e-6
CHUNK = 64
POOL_WINDOWS = (2, 4, 8, 16)
POOL_HIST = max(POOL_WINDOWS) - 1
HIST_ROWS = 16
MLA_HEADS = 8
QK_NOPE = 128
QK_ROPE = 64
V_HEAD = 128
QK_HEAD = QK_NOPE + QK_ROPE
ATTN_SCALE = QK_HEAD ** -0.5
Q_SCALE = ATTN_SCALE * math.log2(math.e)
ROPE_THETA = 10000.0
GMLP_CHUNK = 128
GMLP_GROUPS = 8
LANES = 128
NEG_BIG = -1e30
VMEM_LIMIT = 60 * 1024 * 1024


def _params(*sem):
    return pltpu.CompilerParams(dimension_semantics=sem, vmem_limit_bytes=VMEM_LIMIT)


def _tile(n, pref, mult=8):
    if n <= pref:
        return n
    for t in range(pref - pref % mult, 0, -mult):
        if n % t == 0:
            return t
    raise ValueError(f"no tile for {n}")


def _const_spec(shape):
    nd = len(shape)
    return pl.BlockSpec(shape, lambda *_: (0,) * nd, pipeline_mode=pl.Buffered(1))


def _rms(x, g):
    ms = jnp.mean(x * x, axis=-1, keepdims=True)
    return x * lax.rsqrt(ms + EPS) * g


def _dot(a, b):
    return jnp.dot(a, b, preferred_element_type=F32)


def _dot_nt(a, b):
    return lax.dot_general(a, b, (((1,), (1,)), ((), ())), preferred_element_type=F32)


def _ffn_body(x_ref, g_ref, wg_ref, wu_ref, wd_ref, *rest, convert_next):
    if convert_next:
        (*nxt, o_ref, cg_ref, cu_ref, cd_ref, h_ref) = rest
        for src, dst in zip(nxt, (cg_ref, cu_ref, cd_ref)):
            dst[...] = src[...].astype(BF16)
    else:
        o_ref, h_ref = rest

    @pl.when(pl.program_id(1) == 0)
    def _():
        x = x_ref[...]
        h_ref[...] = _rms(x, g_ref[...]).astype(BF16)
        o_ref[...] = x

    h = h_ref[...]
    a = _dot(h, wg_ref[...])
    b = _dot(h, wu_ref[...])
    act = (a * jax.nn.sigmoid(a) * b).astype(BF16)
    o_ref[...] += 0.5 * _dot(act, wd_ref[...])


def _ffn(x, g, layer, w, w_next=None, next_layer=None):
    m, d = x.shape
    f = w[0].shape[-1]
    tm, tf = _tile(m, 1024), _tile(f, 512, LANES)
    n_i = m // tm
    in_specs = [
        pl.BlockSpec((tm, d), lambda i, j: (i, 0)),
        pl.BlockSpec((None, 1, d), lambda i, j: (layer, 0, 0)),
        pl.BlockSpec((d, tf), lambda i, j: (0, j)),
        pl.BlockSpec((d, tf), lambda i, j: (0, j)),
        pl.BlockSpec((tf, d), lambda i, j: (j, 0)),
    ]
    out_specs = [pl.BlockSpec((tm, d), lambda i, j: (i, 0))]
    out_shape = [jax.ShapeDtypeStruct((m, d), F32)]
    if w_next is not None:
        nr = max(r for r in (1, 2, 4, 8, 16) if r <= n_i and d % (r * LANES) == 0)
        db = d // nr
        blk = lambda i: jnp.minimum(i, nr - 1)
        in_specs += [
            pl.BlockSpec((None, db, tf), lambda i, j: (next_layer, blk(i), j)),
            pl.BlockSpec((None, db, tf), lambda i, j: (next_layer, blk(i), j)),
            pl.BlockSpec((None, tf, db), lambda i, j: (next_layer, j, blk(i))),
        ]
        out_specs += [
            pl.BlockSpec((db, tf), lambda i, j: (blk(i), j)),
            pl.BlockSpec((db, tf), lambda i, j: (blk(i), j)),
            pl.BlockSpec((tf, db), lambda i, j: (j, blk(i))),
        ]
        out_shape += [jax.ShapeDtypeStruct((d, f), BF16), jax.ShapeDtypeStruct((d, f), BF16),
                      jax.ShapeDtypeStruct((f, d), BF16)]
    out = pl.pallas_call(
        functools.partial(_ffn_body, convert_next=w_next is not None),
        grid=(n_i, f // tf),
        in_specs=in_specs,
        out_specs=out_specs,
        out_shape=out_shape,
        scratch_shapes=[pltpu.VMEM((tm, d), BF16)],
        compiler_params=_params("arbitrary", "arbitrary"),
        name="ffn",
    )(x, g, *w, *(w_next or ()))
    return out[0], tuple(out[1:])


def _rope_tab_body(inv_ref, sgn_ref, cos_ref, sin_ref, *, tm, lp, past, s):
    r = lax.broadcasted_iota(jnp.int32, (tm, LANES), 0) + pl.program_id(0) * tm
    t = r - lp
    t = (t & (s - 1)) if s & (s - 1) == 0 else lax.rem(t, s)
    pos = jnp.where(r >= lp, past + t, r).astype(F32)
    ang = pos * inv_ref[...]
    cos_ref[...] = jnp.cos(ang)
    sin_ref[...] = jnp.sin(ang) * sgn_ref[...]


def _rope_tables(m, lp, past, s):
    half = QK_ROPE // 2
    inv = ROPE_THETA ** (-jnp.arange(0, QK_ROPE, 2, dtype=F32) / QK_ROPE)
    inv = jnp.tile(inv, LANES // half)[None, :]
    sgn = jnp.tile(jnp.concatenate([-jnp.ones((half,), F32), jnp.ones((half,), F32)]), LANES // QK_ROPE)[None, :]
    tm = _tile(m, 512)
    return pl.pallas_call(
        functools.partial(_rope_tab_body, tm=tm, lp=lp, past=past, s=s),
        grid=(m // tm,),
        in_specs=[_const_spec((1, LANES)), _const_spec((1, LANES))],
        out_specs=[pl.BlockSpec((tm, LANES), lambda i: (i, 0))] * 2,
        out_shape=[jax.ShapeDtypeStruct((m, LANES), F32)] * 2,
        compiler_params=_params("parallel"),
        name="rope_tables",
    )(inv, sgn)


def _swap_halves(x):
    lane = lax.broadcasted_iota(jnp.int32, x.shape, 1)
    left = pltpu.roll(x, LANES - QK_ROPE // 2, 1)
    right = pltpu.roll(x, QK_ROPE // 2, 1)
    return jnp.where((lane & (QK_ROPE - 1)) < QK_ROPE // 2, left, right)


def _even_in_body(x_ref, g_ref, win_ref, qan_ref, kvan_ref, wqb_ref, qnn_ref, knn_ref, qpn_ref, kpn_ref,
                  cos_ref, sin_ref, u_ref, q_ref, ckv_ref, kpe_ref, *, pool_dim, q_lora, kv_lora):
    h = _rms(x_ref[...], g_ref[...]).astype(BF16)
    o1, o2 = pool_dim + q_lora, pool_dim + q_lora + kv_lora
    cos, sin = cos_ref[...], sin_ref[...]
    lane = lax.broadcasted_iota(jnp.int32, cos.shape, 1)
    lo = lane < QK_ROPE

    def rope(t):
        return t * cos + _swap_halves(t) * sin

    qn = _rms(_dot(h, win_ref[:, pool_dim:o1]), qan_ref[...]).astype(BF16)
    q = _dot(qn, wqb_ref[...])
    zk = _dot(h, win_ref[:, o1:])
    ckv_ref[...] = _rms(zk[:, :kv_lora], kvan_ref[...])
    kp = zk[:, kv_lora:]
    kp = kp * lax.rsqrt(jnp.sum(kp * kp, axis=-1, keepdims=True) / QK_ROPE + EPS) * kpn_ref[...]
    kpe_ref[...] = rope(kp)
    u_ref[...] = _dot(h, win_ref[:, :pool_dim])

    nope_w = MLA_HEADS * QK_NOPE
    k_gain = knn_ref[...] * Q_SCALE
    for hd in range(MLA_HEADS):
        qh = _rms(q[:, hd * QK_NOPE:(hd + 1) * QK_NOPE], qnn_ref[...]) * k_gain
        q_ref[hd, :, :QK_NOPE] = qh.astype(BF16)
    for j in range(MLA_HEADS // 2):
        t = q[:, nope_w + j * LANES:nope_w + (j + 1) * LANES]
        tt = t * t
        s_lo = jnp.sum(jnp.where(lo, tt, 0.0), axis=-1, keepdims=True)
        s_hi = jnp.sum(jnp.where(lo, 0.0, tt), axis=-1, keepdims=True)
        inv = jnp.where(lo, lax.rsqrt(s_lo / QK_ROPE + EPS), lax.rsqrt(s_hi / QK_ROPE + EPS))
        t = rope(t * inv * qpn_ref[...]) * Q_SCALE
        q_ref[2 * j, :, QK_NOPE:] = jnp.where(lo, t, 0.0).astype(BF16)
        q_ref[2 * j + 1, :, QK_NOPE:] = jnp.where(lo, 0.0, t).astype(BF16)


def _even_in(x, g, layer, win, qan, kvan, wqb, qnn, knn, qpn, kpn, cos, sin, e, pool_dim, q_lora, kv_lora):
    m, d = x.shape
    tm = _tile(m, 512)
    n_in = win.shape[-1]
    n_q = wqb.shape[-1]
    row = lambda w: pl.BlockSpec((tm, w), lambda i: (i, 0))
    vec = lambda w, idx: pl.BlockSpec((None, 1, w), lambda i: (idx, 0, 0))
    return pl.pallas_call(
        functools.partial(_even_in_body, pool_dim=pool_dim, q_lora=q_lora, kv_lora=kv_lora),
        grid=(m // tm,),
        in_specs=[
            row(d), vec(d, layer),
            pl.BlockSpec((None, d, n_in), lambda i: (e, 0, 0), pipeline_mode=pl.Buffered(1)),
            vec(q_lora, e), vec(kv_lora, e),
            pl.BlockSpec((None, q_lora, n_q), lambda i: (e, 0, 0), pipeline_mode=pl.Buffered(1)),
            vec(QK_NOPE, e), vec(QK_NOPE, e), vec(LANES, e), vec(LANES, e),
            row(LANES), row(LANES),
        ],
        out_specs=[
            row(pool_dim),
            pl.BlockSpec((MLA_HEADS, tm, 2 * LANES), lambda i: (0, i, 0)),
            row(kv_lora), row(LANES),
        ],
        out_shape=[
            jax.ShapeDtypeStruct((m, pool_dim), F32),
            jax.ShapeDtypeStruct((MLA_HEADS, m, 2 * LANES), BF16),
            jax.ShapeDtypeStruct((m, kv_lora), F32),
            jax.ShapeDtypeStruct((m, LANES), F32),
        ],
        compiler_params=_params("parallel"),
        name="even_in",
    )(x, g, win, qan, kvan, wqb, qnn, knn, qpn, kpn, cos, sin)


def _kv_body(ckv_ref, kpe_ref, w_ref, k_ref, v_ref):
    c = ckv_ref[...].astype(BF16)
    kp = kpe_ref[...]
    kp_both = (kp + pltpu.roll(kp, QK_ROPE, 1)).astype(BF16)
    hw = QK_NOPE + V_HEAD
    for hd in range(MLA_HEADS):
        kv = _dot(c, w_ref[:, hd * hw:(hd + 1) * hw])
        k_ref[hd, :, :QK_NOPE] = _rms(kv[:, :QK_NOPE], 1.0).astype(BF16)
        k_ref[hd, :, QK_NOPE:] = kp_both
        v_ref[hd, :, :V_HEAD] = kv[:, QK_NOPE:].astype(BF16)
        v_ref[hd, :, V_HEAD:] = jnp.ones((c.shape[0], LANES), BF16)


def _kv_prompt(ckv, kpe, wkvb, e, lp):
    kv_lora = ckv.shape[-1]
    tm = _tile(lp, 512)
    return pl.pallas_call(
        _kv_body,
        grid=(lp // tm,),
        in_specs=[
            pl.BlockSpec((tm, kv_lora), lambda i: (i, 0)),
            pl.BlockSpec((tm, LANES), lambda i: (i, 0)),
            pl.BlockSpec((None,) + wkvb.shape[1:], lambda i: (e, 0, 0), pipeline_mode=pl.Buffered(1)),
        ],
        out_specs=[
            pl.BlockSpec((MLA_HEADS, tm, 2 * LANES), lambda i: (0, i, 0)),
            pl.BlockSpec((MLA_HEADS, tm, V_HEAD + LANES), lambda i: (0, i, 0)),
        ],
        out_shape=[
            jax.ShapeDtypeStruct((MLA_HEADS, lp, 2 * LANES), BF16),
            jax.ShapeDtypeStruct((MLA_HEADS, lp, V_HEAD + LANES), BF16),
        ],
        compiler_params=_params("parallel"),
        name="kv_prompt",
    )(ckv, kpe, wkvb)


def _flash_body(q_ref, k_ref, v_ref, o_ref, sa_ref, sb_ref, m_ref, acc_ref, *, t):
    i = pl.program_id(1)
    q = q_ref[...]
    m_ref[...] = jnp.full(m_ref.shape, NEG_BIG, F32)
    acc_ref[...] = jnp.zeros(acc_ref.shape, F32)

    def scores(j):
        return _dot_nt(q, k_ref[pl.ds(pl.multiple_of(j * t, t), t), :])

    def update(s_ref, j, diagonal=False):
        s = s_ref[...]
        if diagonal:
            qc = lax.broadcasted_iota(jnp.int32, s.shape, 0) // CHUNK
            kc = lax.broadcasted_iota(jnp.int32, s.shape, 1) // CHUNK
            s = jnp.where(kc <= qc, s, NEG_BIG)
        m_old = m_ref[...]
        m_new = jnp.maximum(m_old, jnp.max(s, axis=-1, keepdims=True))
        p = jnp.exp2(s - jnp.tile(m_new, (1, t // LANES)))
        alpha = jnp.tile(jnp.exp2(m_old - m_new), (1, acc_ref.shape[1] // LANES))
        pv = _dot(p.astype(BF16), v_ref[pl.ds(pl.multiple_of(j * t, t), t), :])
        acc_ref[...] = alpha * acc_ref[...] + pv
        m_ref[...] = m_new

    sa_ref[...] = scores(0)

    def pair(jj, carry):
        j = 2 * jj
        sb_ref[...] = scores(j + 1)
        update(sa_ref, j)
        sa_ref[...] = scores(j + 2)
        update(sb_ref, j + 1)
        return carry

    lax.fori_loop(0, i // 2, pair, 0)

    @pl.when(i % 2 == 0)
    def _():
        update(sa_ref, i, diagonal=True)

    @pl.when(i % 2 == 1)
    def _():
        sb_ref[...] = scores(i)
        update(sa_ref, i - 1)
        update(sb_ref, i, diagonal=True)

    acc = acc_ref[...]
    o_ref[...] = (acc[:, :V_HEAD] / acc[:, V_HEAD:]).astype(BF16)


def _attn_prompt(q, k, v, lp):
    t = _tile(lp, 1024, LANES)
    return pl.pallas_call(
        functools.partial(_flash_body, t=t),
        grid=(MLA_HEADS, lp // t),
        in_specs=[
            pl.BlockSpec((None, t, 2 * LANES), lambda h, i: (h, i, 0)),
            pl.BlockSpec((None, lp, 2 * LANES), lambda h, i: (h, 0, 0)),
            pl.BlockSpec((None, lp, V_HEAD + LANES), lambda h, i: (h, 0, 0)),
        ],
        out_specs=pl.BlockSpec((t, V_HEAD), lambda h, i: (i, h)),
        out_shape=jax.ShapeDtypeStruct((lp, MLA_HEADS * V_HEAD), BF16),
        scratch_shapes=[pltpu.VMEM((t, t), F32), pltpu.VMEM((t, t), F32),
                        pltpu.VMEM((t, LANES), F32), pltpu.VMEM((t, V_HEAD + LANES), F32)],
        compiler_params=_params("parallel", "arbitrary"),
        name="attn_prompt",
    )(q, k, v)


def _attn_sample_body(q_ref, cc_ref, cn_ref, pct_ref, pn_ref, wk_ref, wv_ref, o_ref, c_sc, p_sc, *, past, s, lk):
    c_sc[:past] = cc_ref[...].astype(BF16)
    c_sc[past:past + s] = cn_ref[...].astype(BF16)
    if lk > past + s:
        c_sc[past + s:] = jnp.zeros((lk - past - s, c_sc.shape[1]), BF16)
    c = c_sc[...]
    valid = lax.broadcasted_iota(jnp.int32, (s, lk), 1) < past + s
    ones = jnp.ones((8, QK_NOPE), BF16)
    n_ch = 4 if lk % 64 == 0 else 1
    ch = lk // n_ch
    q_pe = jnp.concatenate([q_ref[hd][:, QK_NOPE:] for hd in range(MLA_HEADS)], axis=0)
    pct = pct_ref[...].astype(BF16)
    pn = pn_ref[...]
    kp_new = jnp.concatenate([(pn + pltpu.roll(pn, QK_ROPE, 1)).astype(BF16),
                              jnp.zeros((lk - past - s, LANES), BF16)], axis=0)
    pe = jnp.concatenate([_dot(q_pe, jnp.concatenate([pct, pct], axis=0)), _dot_nt(q_pe, kp_new)], axis=1)
    inv_l = []
    for pair in range(MLA_HEADS // 2):
        wk = wk_ref[:, pair * 2 * QK_NOPE:(pair + 1) * 2 * QK_NOPE]
        kk = jnp.concatenate([_dot(c[r * ch:(r + 1) * ch], wk) for r in range(n_ch)])
        for hd in (2 * pair, 2 * pair + 1):
            kn = kk[:, (hd % 2) * QK_NOPE:(hd % 2 + 1) * QK_NOPE]
            ms = _dot_nt(ones, (kn * kn).astype(BF16))[0:1] / QK_NOPE
            sc = _dot_nt(q_ref[hd][:, :QK_NOPE], kn.astype(BF16)) * lax.rsqrt(ms + EPS) + pe[hd * s:(hd + 1) * s]
            sc = jnp.where(valid, sc, NEG_BIG)
            p = jnp.exp2(sc - jnp.max(sc, axis=-1, keepdims=True))
            inv_l.append(1.0 / jnp.sum(p, axis=-1, keepdims=True))
            p_sc[hd * s:(hd + 1) * s] = p.astype(BF16)
    ctx = _dot(p_sc[...], c)
    for hd in range(MLA_HEADS):
        ctx_h = (ctx[hd * s:(hd + 1) * s] * inv_l[hd]).astype(BF16)
        o_ref[:, hd * V_HEAD:(hd + 1) * V_HEAD] = _dot(ctx_h, wv_ref[:, hd * V_HEAD:(hd + 1) * V_HEAD]).astype(BF16)


def _attn_sample(q, ckv_cache, ckv, kpe_cache, kpe, wk, wv, e, lp, b, s):
    past, kv_lora = ckv_cache.shape[2], ckv_cache.shape[3]
    lk = past + -(-s // LANES) * LANES
    row0 = lp // s
    return pl.pallas_call(
        functools.partial(_attn_sample_body, past=past, s=s, lk=lk),
        grid=(b,),
        in_specs=[
            pl.BlockSpec((MLA_HEADS, s, 2 * LANES), lambda i: (0, row0 + i, 0)),
            pl.BlockSpec((None, None, past, kv_lora), lambda i: (e, i, 0, 0)),
            pl.BlockSpec((s, kv_lora), lambda i: (row0 + i, 0)),
            pl.BlockSpec((None, None, QK_ROPE, past), lambda i: (e, i, 0, 0)),
            pl.BlockSpec((s, LANES), lambda i: (row0 + i, 0)),
            pl.BlockSpec((None,) + wk.shape[1:], lambda i: (e, 0, 0), pipeline_mode=pl.Buffered(1)),
            pl.BlockSpec((None,) + wv.shape[1:], lambda i: (e, 0, 0), pipeline_mode=pl.Buffered(1)),
        ],
        out_specs=pl.BlockSpec((s, MLA_HEADS * V_HEAD), lambda i: (i, 0)),
        out_shape=jax.ShapeDtypeStruct((b * s, MLA_HEADS * V_HEAD), BF16),
        scratch_shapes=[pltpu.VMEM((lk, kv_lora), BF16), pltpu.VMEM((MLA_HEADS * s, lk), BF16)],
        compiler_params=_params("parallel"),
        name="attn_sample",
    )(q, ckv_cache, ckv, kpe_cache, kpe, wk, wv)


def _pool_segment(ext_ref, pooled_ref, row0, n, pos0, gdim):
    pos = (lax.broadcasted_iota(jnp.int32, (n, 1), 0) + pos0).astype(F32)
    for gi, w in enumerate(POOL_WINDOWS):
        cols = slice(gi * gdim, (gi + 1) * gdim)
        u = ext_ref[HIST_ROWS:HIST_ROWS + n, cols]
        acc = u
        for k in range(1, w):
            acc = acc + ext_ref[HIST_ROWS - k:HIST_ROWS - k + n, cols]
        cnt = jnp.minimum(jnp.float32(w), pos + 1.0)
        pooled_ref[row0:row0 + n, cols] = (acc / cnt - u).astype(BF16)


def _pool_project(pooled_ref, pw_ref, sc_ref, o_ref, gdim):
    for gi in range(len(POOL_WINDOWS)):
        cols = slice(gi * gdim, (gi + 1) * gdim)
        o_ref[:, cols] = (_dot(pooled_ref[:, cols], pw_ref[gi]) * sc_ref[:, cols]).astype(BF16)


def _pool_prompt_body(u_ref, pw_ref, sc_ref, o_ref, ext_ref, pooled_ref, *, tm, gdim):
    i = pl.program_id(0)

    @pl.when(i == 0)
    def _():
        ext_ref[:HIST_ROWS] = jnp.zeros((HIST_ROWS, ext_ref.shape[1]), F32)

    ext_ref[HIST_ROWS:] = u_ref[...]
    _pool_segment(ext_ref, pooled_ref, 0, tm, i * tm, gdim)
    _pool_project(pooled_ref, pw_ref, sc_ref, o_ref, gdim)
    ext_ref[:HIST_ROWS] = ext_ref[tm:tm + HIST_ROWS]


def _pool_sample_body(u_ref, hist_ref, pw_ref, sc_ref, o_ref, ext_ref, pooled_ref, *, nb, s, past, gdim):
    for bi in range(nb):
        ext = ext_ref.at[bi]
        ext[0:1] = jnp.zeros((1, ext_ref.shape[2]), F32)
        ext[1:HIST_ROWS] = hist_ref[bi]
        ext[HIST_ROWS:] = u_ref[bi * s:(bi + 1) * s]
        _pool_segment(ext, pooled_ref, bi * s, s, past, gdim)
    _pool_project(pooled_ref, pw_ref, sc_ref, o_ref, gdim)


def _pool(u, state_pool, pw, scale, e, lp, b, s, past):
    m, pool_dim = u.shape
    ng = len(POOL_WINDOWS)
    gdim = pool_dim // ng
    w_spec = pl.BlockSpec((None, ng, gdim, gdim), lambda i: (e, 0, 0, 0))
    s_spec = pl.BlockSpec((None, 1, pool_dim), lambda i: (e, 0, 0))
    tm = _tile(lp, 512, HIST_ROWS)
    out_p = pl.pallas_call(
        functools.partial(_pool_prompt_body, tm=tm, gdim=gdim),
        grid=(lp // tm,),
        in_specs=[pl.BlockSpec((tm, pool_dim), lambda i: (i, 0)), w_spec, s_spec],
        out_specs=pl.BlockSpec((tm, pool_dim), lambda i: (i, 0)),
        out_shape=jax.ShapeDtypeStruct((lp, pool_dim), BF16),
        scratch_shapes=[pltpu.VMEM((HIST_ROWS + tm, pool_dim), F32), pltpu.VMEM((tm, pool_dim), BF16)],
        compiler_params=_params("arbitrary"),
        name="pool_prompt",
    )(u, pw, scale)
    nb = math.gcd(b, max(1, 256 // s))
    row0 = lp // (nb * s)
    out_s = pl.pallas_call(
        functools.partial(_pool_sample_body, nb=nb, s=s, past=past, gdim=gdim),
        grid=(b // nb,),
        in_specs=[
            pl.BlockSpec((nb * s, pool_dim), lambda i: (row0 + i, 0)),
            pl.BlockSpec((None, nb, POOL_HIST, pool_dim), lambda i: (e, i, 0, 0)),
            w_spec, s_spec,
        ],
        out_specs=pl.BlockSpec((nb * s, pool_dim), lambda i: (i, 0)),
        out_shape=jax.ShapeDtypeStruct((b * s, pool_dim), BF16),
        scratch_shapes=[pltpu.VMEM((nb, HIST_ROWS + s, pool_dim), F32), pltpu.VMEM((nb * s, pool_dim), BF16)],
        compiler_params=_params("parallel"),
        name="pool_sample",
    )(u, state_pool, pw, scale)
    return out_p, out_s


def _even_out_body(x_ref, pp_ref, ps_ref, ap_ref, as_ref, w_ref, o_ref, *, pool_dim, n_prompt_tiles):
    def project(p_ref, a_ref):
        o_ref[...] = x_ref[...] + _dot(p_ref[...], w_ref[:pool_dim]) + _dot(a_ref[...], w_ref[pool_dim:])

    is_prompt = pl.program_id(0) < n_prompt_tiles
    pl.when(is_prompt)(lambda: project(pp_ref, ap_ref))
    pl.when(jnp.logical_not(is_prompt))(lambda: project(ps_ref, as_ref))


def _even_out(x, pool_p, pool_s, attn_p, attn_s, wout, e):
    m, d = x.shape
    lp, pool_dim = pool_p.shape
    attn_dim = attn_p.shape[1]
    tm = _tile(math.gcd(lp, m - lp), 512)
    npt = lp // tm
    prompt_rows = lambda w: pl.BlockSpec((tm, w), lambda i: (jnp.minimum(i, npt - 1), 0))
    sample_rows = lambda w: pl.BlockSpec((tm, w), lambda i: (jnp.maximum(i - npt, 0), 0))
    return pl.pallas_call(
        functools.partial(_even_out_body, pool_dim=pool_dim, n_prompt_tiles=npt),
        grid=(m // tm,),
        in_specs=[
            pl.BlockSpec((tm, d), lambda i: (i, 0)),
            prompt_rows(pool_dim), sample_rows(pool_dim), prompt_rows(attn_dim), sample_rows(attn_dim),
            pl.BlockSpec((None,) + wout.shape[1:], lambda i: (e, 0, 0), pipeline_mode=pl.Buffered(1)),
        ],
        out_specs=pl.BlockSpec((tm, d), lambda i: (i, 0)),
        out_shape=jax.ShapeDtypeStruct((m, d), F32),
        compiler_params=_params("arbitrary"),
        name="even_out",
    )(x, pool_p, pool_s, attn_p, attn_s, wout)


def _odd_in_body(x_ref, g_ref, w_ref, vn_ref, u_ref, v_ref, vf_ref, *, gate):
    h = _rms(x_ref[...], g_ref[...]).astype(BF16)
    v = _rms(jax.nn.gelu(_dot(h, w_ref[:, gate:])), vn_ref[...])
    v_ref[...] = v.astype(BF16)
    vf_ref[...] = v
    n_chunks = 4 if gate % (4 * LANES) == 0 else 1
    cw = gate // n_chunks
    for c in range(n_chunks):
        u_ref[:, c * cw:(c + 1) * cw] = jax.nn.gelu(_dot(h, w_ref[:, c * cw:(c + 1) * cw])).astype(BF16)


def _odd_in(x, g, layer, win, vn, o, lp):
    m, d = x.shape
    gate = win.shape[-1] // 2
    tm = _tile(math.gcd(lp, m - lp), 512)
    first = lp // tm
    return pl.pallas_call(
        functools.partial(_odd_in_body, gate=gate),
        grid=(m // tm,),
        in_specs=[
            pl.BlockSpec((tm, d), lambda i: (i, 0)),
            pl.BlockSpec((None, 1, d), lambda i: (layer, 0, 0)),
            pl.BlockSpec((None, d, 2 * gate), lambda i: (o, 0, 0), pipeline_mode=pl.Buffered(1)),
            pl.BlockSpec((None, 1, gate), lambda i: (o, 0, 0)),
        ],
        out_specs=[
            pl.BlockSpec((tm, gate), lambda i: (i, 0)),
            pl.BlockSpec((tm, gate), lambda i: (i, 0)),
            pl.BlockSpec((tm, gate), lambda i: (jnp.maximum(i - first, 0), 0)),
        ],
        out_shape=[
            jax.ShapeDtypeStruct((m, gate), BF16),
            jax.ShapeDtypeStruct((m, gate), BF16),
            jax.ShapeDtypeStruct((m - lp, gate), F32),
        ],
        compiler_params=_params("arbitrary"),
        name="odd_in",
    )(x, g, win, vn)


def _odd_out_body(x_ref, u_ref, v_ref, ws_ref, b_ref, w_ref, o_ref, us_ref, *, tm, n_prompt_tiles, s):
    is_prompt = pl.program_id(0) < n_prompt_tiles
    ii = lax.broadcasted_iota(jnp.int32, (GMLP_CHUNK, GMLP_CHUNK), 0)
    jj = lax.broadcasted_iota(jnp.int32, (GMLP_CHUNK, GMLP_CHUNK), 1)
    same_stream = jnp.where((ii // s) == (jj // s), 1, 0) + jnp.where(is_prompt, 1, 0)
    keep = (jj <= ii) & (same_stream > 0)
    gdim = u_ref.shape[1] // GMLP_GROUPS
    for gi in range(GMLP_GROUPS):
        cols = slice(gi * gdim, (gi + 1) * gdim)
        wmat = jnp.where(keep, ws_ref[gi], 0.0).astype(BF16)
        bias = b_ref[:, gi:gi + 1]
        for ci in range(tm // GMLP_CHUNK):
            rows = slice(ci * GMLP_CHUNK, (ci + 1) * GMLP_CHUNK)
            sg = _dot(wmat, v_ref[rows, cols]) + bias
            us_ref[rows, cols] = (u_ref[rows, cols].astype(F32) * sg).astype(BF16)
    o_ref[...] = x_ref[...] + _dot(us_ref[...], w_ref[...])


def _odd_out(x, u, v, ws_sel, b_sel, wout, o, lp, s):
    m, d = x.shape
    gate = u.shape[1]
    tm = _tile(math.gcd(lp, m - lp), 512, GMLP_CHUNK)
    npt = lp // tm
    sel = lambda i: jnp.where(i >= npt, 1, 0)
    return pl.pallas_call(
        functools.partial(_odd_out_body, tm=tm, n_prompt_tiles=npt, s=s),
        grid=(m // tm,),
        in_specs=[
            pl.BlockSpec((tm, d), lambda i: (i, 0)),
            pl.BlockSpec((tm, gate), lambda i: (i, 0)),
            pl.BlockSpec((tm, gate), lambda i: (i, 0)),
            pl.BlockSpec((None, None, GMLP_GROUPS, GMLP_CHUNK, GMLP_CHUNK), lambda i: (sel(i), o, 0, 0, 0)),
            pl.BlockSpec((None, None, GMLP_CHUNK, GMLP_GROUPS), lambda i: (sel(i), o, 0, 0)),
            pl.BlockSpec((None,) + wout.shape[1:], lambda i: (o, 0, 0), pipeline_mode=pl.Buffered(1)),
        ],
        out_specs=pl.BlockSpec((tm, d), lambda i: (i, 0)),
        out_shape=jax.ShapeDtypeStruct((m, d), F32),
        scratch_shapes=[pltpu.VMEM((tm, gate), BF16)],
        compiler_params=_params("parallel"),
        name="odd_out",
    )(x, u, v, ws_sel, b_sel, wout)


def kernel(x_prompt, x_sample, cache_mla_ckv, cache_mla_kpe, state_pool, norm_ffn1, norm_mix, norm_ffn2, ffn1_w_gate, ffn1_w_up, ffn1_w_down, ffn2_w_gate, ffn2_w_up, ffn2_w_down, ev_w_in, ev_q_a_norm, ev_kv_a_norm, ev_w_qb, ev_w_kvb, ev_q_nope_norm, ev_q_pe_norm, ev_k_nope_norm, ev_k_pe_norm, ev_pool_w, ev_pool_scale, ev_w_out, od_w_in, od_v_norm, od_w_s, od_b_s, od_w_out):
    bp, lp, d = x_prompt.shape
    b, s, _ = x_sample.shape
    depth = norm_mix.shape[0]
    n_even, n_odd = ev_w_in.shape[0], od_w_in.shape[0]
    past = cache_mla_ckv.shape[2]
    q_lora, kv_lora = ev_q_a_norm.shape[1], ev_kv_a_norm.shape[1]
    pool_dim = ev_pool_scale.shape[1]
    gate = od_v_norm.shape[1]
    m = lp + b * s
    assert bp == 1 and lp % GMLP_CHUNK == 0 and (b * s) % GMLP_CHUNK == 0 and GMLP_CHUNK % s == 0
    assert past % CHUNK == 0 and s <= CHUNK, "every cached and new key must be visible to every sample query"
    assert s >= POOL_HIST and lp >= POOL_HIST and lp % s == 0
    assert ev_w_in.shape[2] == pool_dim + q_lora + kv_lora + QK_ROPE

    vec = lambda a: a[:, None, :]
    bf = lambda a: a.astype(BF16)
    pad_lanes = lambda a: jnp.pad(a, [(0, 0)] * (a.ndim - 1) + [(0, (-a.shape[-1]) % LANES)])

    w_ffn = ((ffn1_w_gate, ffn1_w_up, ffn1_w_down), (ffn2_w_gate, ffn2_w_up, ffn2_w_down))
    g_ffn = (vec(norm_ffn1), vec(norm_ffn2))
    g_mix = vec(norm_mix)
    w_in_e = bf(pad_lanes(ev_w_in))
    wq = ev_w_qb.reshape(n_even, q_lora, MLA_HEADS, QK_HEAD)
    w_qb = bf(jnp.concatenate([wq[..., :QK_NOPE].reshape(n_even, q_lora, -1),
                               wq[..., QK_NOPE:].reshape(n_even, q_lora, -1)], axis=-1))
    w_kvb, w_out_e, pool_w = bf(ev_w_kvb), bf(ev_w_out), bf(ev_pool_w)
    wkv = w_kvb.reshape(n_even, kv_lora, MLA_HEADS, QK_NOPE + V_HEAD)
    w_k = wkv[..., :QK_NOPE].reshape(n_even, kv_lora, -1)
    w_v = wkv[..., QK_NOPE:].reshape(n_even, kv_lora, -1)
    qpn = vec(jnp.tile(ev_q_pe_norm, (1, LANES // QK_ROPE)))
    kpn = vec(pad_lanes(ev_k_pe_norm))
    kpe_cache_t = jnp.swapaxes(cache_mla_kpe, 2, 3)
    w_in_o, w_out_o = bf(od_w_in), bf(od_w_out)
    rep = GMLP_CHUNK // s
    ws_sel = jnp.stack([od_w_s, jnp.tile(od_w_s[:, :, :s, :s], (1, 1, rep, rep))])
    b_sel = jnp.stack([od_b_s, jnp.tile(od_b_s[:, :, :s], (1, 1, rep))]).transpose(0, 1, 3, 2)

    cos, sin = _rope_tables(m, lp, past, s)
    x = jnp.concatenate([x_prompt.reshape(lp, d), x_sample.reshape(b * s, d)], axis=0)
    new_ckv, new_kpe, new_u, new_v = [], [], [], []
    w_cur = tuple(bf(w[0]) for w in w_ffn[0])
    for layer in range(depth):
        x, w_cur = _ffn(x, g_ffn[0], layer, w_cur, w_ffn[1], layer)
        if layer % 2 == 0:
            e = layer // 2
            u, q, ckv, kpe = _even_in(x, g_mix, layer, w_in_e, vec(ev_q_a_norm), vec(ev_kv_a_norm), w_qb,
                                      vec(ev_q_nope_norm), vec(ev_k_nope_norm), qpn, kpn, cos, sin, e,
                                      pool_dim, q_lora, kv_lora)
            k_p, v_p = _kv_prompt(ckv, kpe, w_kvb, e, lp)
            attn_p = _attn_prompt(q, k_p, v_p, lp)
            attn_s = _attn_sample(q, cache_mla_ckv, ckv, kpe_cache_t, kpe, w_k, w_v, e, lp, b, s)
            pool_p, pool_s = _pool(u, state_pool, pool_w, vec(ev_pool_scale), e, lp, b, s, past)
            x = _even_out(x, pool_p, pool_s, attn_p, attn_s, w_out_e, e)
            new_ckv.append((ckv[:lp], ckv[lp:]))
            new_kpe.append((kpe[:lp, :QK_ROPE], kpe[lp:, :QK_ROPE]))
            new_u.append((u[lp - POOL_HIST:lp], u[lp:].reshape(b, s, pool_dim)[:, s - POOL_HIST:]))
        else:
            o = layer // 2
            u, v, v_f32 = _odd_in(x, g_mix, layer, w_in_o, vec(od_v_norm), o, lp)
            x = _odd_out(x, u, v, ws_sel, b_sel, w_out_o, o, lp, s)
            new_v.append(v_f32)
        if layer + 1 < depth:
            x, w_cur = _ffn(x, g_ffn[1], layer, w_cur, w_ffn[0], layer + 1)
        else:
            x, _ = _ffn(x, g_ffn[1], layer, w_cur)

    stack = lambda pairs, k, shape: jnp.stack([p[k] for p in pairs]).reshape((n_even,) + shape)
    return (
        x[:lp].reshape(1, lp, d),
        x[lp:].reshape(b, s, d),
        stack(new_ckv, 0, (1, lp, kv_lora)),
        stack(new_kpe, 0, (1, lp, QK_ROPE)),
        stack(new_u, 0, (1, POOL_HIST, pool_dim)),
        stack(new_ckv, 1, (b, s, kv_lora)),
        stack(new_kpe, 1, (b, s, QK_ROPE)),
        stack(new_u, 1, (b, POOL_HIST, pool_dim)),
        jnp.stack(new_v).reshape(n_odd, b, s, gate),
    )
```

```python
import functools
import math

import jax
import jax.numpy as jnp
from jax import lax
from jax.experimental import pallas as pl
from jax.experimental.pallas import tpu as pltpu

F32 = jnp.float32
BF16 = jnp.bfloat16

EPS = 1e-6
CHUNK = 64
POOL_WINDOWS = (2, 4, 8, 16)
POOL_HIST = max(POOL_WINDOWS) - 1
HIST_ROWS = 16
MLA_HEADS = 8
QK_NOPE = 128
QK_ROPE = 64
V_HEAD = 128
QK_HEAD = QK_NOPE + QK_ROPE
ATTN_SCALE = QK_HEAD ** -0.5
Q_SCALE = ATTN_SCALE * math.log2(math.e)
ROPE_THETA = 10000.0
GMLP_CHUNK = 128
GMLP_GROUPS = 8
LANES = 128
NEG_BIG = -1e30
VMEM_LIMIT = 60 * 1024 * 1024


def _params(*sem):
    return pltpu.CompilerParams(dimension_semantics=sem, vmem_limit_bytes=VMEM_LIMIT)


def _tile(n, pref, mult=8):
    if n <= pref:
        return n
    for t in range(pref - pref % mult, 0, -mult):
        if n % t == 0:
            return t
    raise ValueError(f"no tile for {n}")


def _const_spec(shape):
    nd = len(shape)
    return pl.BlockSpec(shape, lambda *_: (0,) * nd, pipeline_mode=pl.Buffered(1))


def _rms(x, g):
    ms = jnp.mean(x * x, axis=-1, keepdims=True)
    return x * lax.rsqrt(ms + EPS) * g


def _dot(a, b):
    return jnp.dot(a, b, preferred_element_type=F32)


def _dot_nt(a, b):
    return lax.dot_general(a, b, (((1,), (1,)), ((), ())), preferred_element_type=F32)


def _ffn_body(x_ref, g_ref, wg_ref, wu_ref, wd_ref, *rest, convert_next, n_first):
    *rest, h_ref = rest
    if convert_next:
        (*nxt, o_ref, cg_ref, cu_ref, cd_ref) = rest
        for src, dst in zip(nxt, (cg_ref, cu_ref, cd_ref)):
            dst[...] = src[...].astype(BF16)
        outs = (o_ref,)
    else:
        outs = tuple(rest)

    def run(o_ref):
        @pl.when(pl.program_id(1) == 0)
        def _():
            x = x_ref[...]
            h_ref[...] = _rms(x, g_ref[...]).astype(BF16)
            o_ref[...] = x

        h = h_ref[...]
        a = _dot(h, wg_ref[...])
        b = _dot(h, wu_ref[...])
        act = (a * jax.nn.sigmoid(a) * b * 0.5).astype(BF16)
        o_ref[...] += _dot(act, wd_ref[...])

    if len(outs) == 1:
        run(outs[0])
    else:
        first = pl.program_id(0) < n_first
        pl.when(first)(lambda: run(outs[0]))
        pl.when(jnp.logical_not(first))(lambda: run(outs[1]))


def _ffn(x, g, layer, w, w_next=None, next_layer=None, split_rows=None):
    m, d = x.shape
    f = w[0].shape[-1]
    tm = _tile(m if split_rows is None else math.gcd(split_rows, m - split_rows), 1024)
    tf = _tile(f, 512 if split_rows is None else 256, LANES)
    n_i = m // tm
    in_specs = [
        pl.BlockSpec((tm, d), lambda i, j: (i, 0)),
        pl.BlockSpec((None, 1, d), lambda i, j: (layer, 0, 0)),
        pl.BlockSpec((d, tf), lambda i, j: (0, j)),
        pl.BlockSpec((d, tf), lambda i, j: (0, j)),
        pl.BlockSpec((tf, d), lambda i, j: (j, 0)),
    ]
    n_first = None
    if split_rows is None:
        out_specs = [pl.BlockSpec((tm, d), lambda i, j: (i, 0))]
        out_shape = [jax.ShapeDtypeStruct((m, d), F32)]
    else:
        assert w_next is None
        n_first = split_rows // tm
        out_specs = [pl.BlockSpec((tm, d), lambda i, j: (jnp.minimum(i, n_first - 1), 0)),
                     pl.BlockSpec((tm, d), lambda i, j: (jnp.maximum(i - n_first, 0), 0),
                                  pipeline_mode=pl.Buffered(1))]
        out_shape = [jax.ShapeDtypeStruct((split_rows, d), F32), jax.ShapeDtypeStruct((m - split_rows, d), F32)]
    if w_next is not None:
        nr = max(r for r in (1, 2, 4, 8, 16) if r <= n_i and d % (r * LANES) == 0)
        db = d // nr
        blk = lambda i: jnp.minimum(i, nr - 1)
        in_specs += [
            pl.BlockSpec((None, db, tf), lambda i, j: (next_layer, blk(i), j)),
            pl.BlockSpec((None, db, tf), lambda i, j: (next_layer, blk(i), j)),
            pl.BlockSpec((None, tf, db), lambda i, j: (next_layer, j, blk(i))),
        ]
        out_specs += [
            pl.BlockSpec((db, tf), lambda i, j: (blk(i), j)),
            pl.BlockSpec((db, tf), lambda i, j: (blk(i), j)),
            pl.BlockSpec((tf, db), lambda i, j: (j, blk(i))),
        ]
        out_shape += [jax.ShapeDtypeStruct((d, f), BF16), jax.ShapeDtypeStruct((d, f), BF16),
                      jax.ShapeDtypeStruct((f, d), BF16)]
    out = pl.pallas_call(
        functools.partial(_ffn_body, convert_next=w_next is not None, n_first=n_first),
        grid=(n_i, f // tf),
        in_specs=in_specs,
        out_specs=out_specs,
        out_shape=out_shape,
        scratch_shapes=[pltpu.VMEM((tm, d), BF16)],
        compiler_params=_params("arbitrary", "arbitrary"),
        name="ffn",
    )(x, g, *w, *(w_next or ()))
    return out[0], tuple(out[1:])


def _rope_tab_body(inv_ref, sgn_ref, cos_ref, sin_ref, *, tm, lp, past, s):
    r = lax.broadcasted_iota(jnp.int32, (tm, LANES), 0) + pl.program_id(0) * tm
    t = r - lp
    t = (t & (s - 1)) if s & (s - 1) == 0 else lax.rem(t, s)
    pos = jnp.where(r >= lp, past + t, r).astype(F32)
    ang = pos * inv_ref[...]
    cos_ref[...] = jnp.cos(ang)
    sin_ref[...] = jnp.sin(ang) * sgn_ref[...]


def _rope_tables(m, lp, past, s):
    half = QK_ROPE // 2
    inv = ROPE_THETA ** (-jnp.arange(0, QK_ROPE, 2, dtype=F32) / QK_ROPE)
    inv = jnp.tile(inv, LANES // half)[None, :]
    sgn = jnp.tile(jnp.concatenate([-jnp.ones((half,), F32), jnp.ones((half,), F32)]), LANES // QK_ROPE)[None, :]
    tm = _tile(m, 512)
    return pl.pallas_call(
        functools.partial(_rope_tab_body, tm=tm, lp=lp, past=past, s=s),
        grid=(m // tm,),
        in_specs=[_const_spec((1, LANES)), _const_spec((1, LANES))],
        out_specs=[pl.BlockSpec((tm, LANES), lambda i: (i, 0))] * 2,
        out_shape=[jax.ShapeDtypeStruct((m, LANES), F32)] * 2,
        compiler_params=_params("parallel"),
        name="rope_tables",
    )(inv, sgn)


def _swap_halves(x):
    lane = lax.broadcasted_iota(jnp.int32, x.shape, 1)
    left = pltpu.roll(x, LANES - QK_ROPE // 2, 1)
    right = pltpu.roll(x, QK_ROPE // 2, 1)
    return jnp.where((lane & (QK_ROPE - 1)) < QK_ROPE // 2, left, right)


def _even_in_body(x_ref, g_ref, win_ref, qan_ref, kvan_ref, wqb_ref, qnn_ref, knn_ref, qpn_ref, kpn_ref,
                  cos_ref, sin_ref, u_ref, q_ref, ckv_ref, kpe_ref, *, pool_dim, q_lora, kv_lora):
    h = _rms(x_ref[...], g_ref[...]).astype(BF16)
    o1, o2 = pool_dim + q_lora, pool_dim + q_lora + kv_lora
    cos, sin = cos_ref[...], sin_ref[...]
    lane = lax.broadcasted_iota(jnp.int32, cos.shape, 1)
    lo = lane < QK_ROPE

    def rope(t):
        return t * cos + _swap_halves(t) * sin

    qn = _rms(_dot(h, win_ref[:, pool_dim:o1]), qan_ref[...]).astype(BF16)
    q = _dot(qn, wqb_ref[...])
    zk = _dot(h, win_ref[:, o1:])
    ckv_ref[...] = _rms(zk[:, :kv_lora], kvan_ref[...])
    kp = zk[:, kv_lora:]
    kp = kp * lax.rsqrt(jnp.sum(kp * kp, axis=-1, keepdims=True) / QK_ROPE + EPS) * kpn_ref[...]
    kpe_ref[...] = rope(kp)
    u_ref[...] = _dot(h, win_ref[:, :pool_dim])

    nope_w = MLA_HEADS * QK_NOPE
    k_gain = knn_ref[...] * Q_SCALE
    for hd in range(MLA_HEADS):
        qh = _rms(q[:, hd * QK_NOPE:(hd + 1) * QK_NOPE], qnn_ref[...]) * k_gain
        q_ref[hd, :, :QK_NOPE] = qh.astype(BF16)
    for j in range(MLA_HEADS // 2):
        t = q[:, nope_w + j * LANES:nope_w + (j + 1) * LANES]
        tt = t * t
        s_lo = jnp.sum(jnp.where(lo, tt, 0.0), axis=-1, keepdims=True)
        s_hi = jnp.sum(jnp.where(lo, 0.0, tt), axis=-1, keepdims=True)
        inv = jnp.where(lo, lax.rsqrt(s_lo / QK_ROPE + EPS), lax.rsqrt(s_hi / QK_ROPE + EPS))
        t = rope(t * inv * qpn_ref[...]) * Q_SCALE
        q_ref[2 * j, :, QK_NOPE:] = jnp.where(lo, t, 0.0).astype(BF16)
        q_ref[2 * j + 1, :, QK_NOPE:] = jnp.where(lo, 0.0, t).astype(BF16)


def _even_in(x, g, layer, win, qan, kvan, wqb, qnn, knn, qpn, kpn, cos, sin, e, pool_dim, q_lora, kv_lora):
    m, d = x.shape
    tm = _tile(m, 512)
    n_in = win.shape[-1]
    n_q = wqb.shape[-1]
    row = lambda w: pl.BlockSpec((tm, w), lambda i: (i, 0))
    vec = lambda w, idx: pl.BlockSpec((None, 1, w), lambda i: (idx, 0, 0))
    return pl.pallas_call(
        functools.partial(_even_in_body, pool_dim=pool_dim, q_lora=q_lora, kv_lora=kv_lora),
        grid=(m // tm,),
        in_specs=[
            row(d), vec(d, layer),
            pl.BlockSpec((None, d, n_in), lambda i: (e, 0, 0), pipeline_mode=pl.Buffered(1)),
            vec(q_lora, e), vec(kv_lora, e),
            pl.BlockSpec((None, q_lora, n_q), lambda i: (e, 0, 0), pipeline_mode=pl.Buffered(1)),
            vec(QK_NOPE, e), vec(QK_NOPE, e), vec(LANES, e), vec(LANES, e),
            row(LANES), row(LANES),
        ],
        out_specs=[
            row(pool_dim),
            pl.BlockSpec((MLA_HEADS, tm, 2 * LANES), lambda i: (0, i, 0)),
            row(kv_lora), row(LANES),
        ],
        out_shape=[
            jax.ShapeDtypeStruct((m, pool_dim), F32),
            jax.ShapeDtypeStruct((MLA_HEADS, m, 2 * LANES), BF16),
            jax.ShapeDtypeStruct((m, kv_lora), F32),
            jax.ShapeDtypeStruct((m, LANES), F32),
        ],
        compiler_params=_params("parallel"),
        name="even_in",
    )(x, g, win, qan, kvan, wqb, qnn, knn, qpn, kpn, cos, sin)


def _kv_body(ckv_ref, kpe_ref, w_ref, k_ref, v_ref):
    c = ckv_ref[...].astype(BF16)
    kp = kpe_ref[...]
    kp_both = (kp + pltpu.roll(kp, QK_ROPE, 1)).astype(BF16)
    hw = QK_NOPE + V_HEAD
    for hd in range(MLA_HEADS):
        kv = _dot(c, w_ref[:, hd * hw:(hd + 1) * hw])
        k_ref[hd, :, :QK_NOPE] = _rms(kv[:, :QK_NOPE], 1.0).astype(BF16)
        k_ref[hd, :, QK_NOPE:] = kp_both
        v_ref[hd, :, :V_HEAD] = kv[:, QK_NOPE:].astype(BF16)
        v_ref[hd, :, V_HEAD:] = jnp.ones((c.shape[0], LANES), BF16)


def _kv_prompt(ckv, kpe, wkvb, e, lp):
    kv_lora = ckv.shape[-1]
    tm = _tile(lp, 512)
    return pl.pallas_call(
        _kv_body,
        grid=(lp // tm,),
        in_specs=[
            pl.BlockSpec((tm, kv_lora), lambda i: (i, 0)),
            pl.BlockSpec((tm, LANES), lambda i: (i, 0)),
            pl.BlockSpec((None,) + wkvb.shape[1:], lambda i: (e, 0, 0), pipeline_mode=pl.Buffered(1)),
        ],
        out_specs=[
            pl.BlockSpec((MLA_HEADS, tm, 2 * LANES), lambda i: (0, i, 0)),
            pl.BlockSpec((MLA_HEADS, tm, V_HEAD + LANES), lambda i: (0, i, 0)),
        ],
        out_shape=[
            jax.ShapeDtypeStruct((MLA_HEADS, lp, 2 * LANES), BF16),
            jax.ShapeDtypeStruct((MLA_HEADS, lp, V_HEAD + LANES), BF16),
        ],
        compiler_params=_params("parallel"),
        name="kv_prompt",
    )(ckv, kpe, wkvb)


def _flash_body(q_ref, k_ref, v_ref, o_ref, sa_ref, sb_ref, m_ref, acc_ref, *, t):
    i = pl.program_id(1)
    q = q_ref[...]
    m_ref[...] = jnp.full(m_ref.shape, NEG_BIG, F32)
    acc_ref[...] = jnp.zeros(acc_ref.shape, F32)

    def scores(j):
        return _dot_nt(q, k_ref[pl.ds(pl.multiple_of(j * t, t), t), :])

    def update(s_ref, j, diagonal=False):
        s = s_ref[...]
        if diagonal:
            qc = lax.broadcasted_iota(jnp.int32, s.shape, 0) // CHUNK
            kc = lax.broadcasted_iota(jnp.int32, s.shape, 1) // CHUNK
            s = jnp.where(kc <= qc, s, NEG_BIG)
        m_old = m_ref[...]
        m_new = jnp.maximum(m_old, jnp.max(s, axis=-1, keepdims=True))
        p = jnp.exp2(s - jnp.tile(m_new, (1, t // LANES)))
        alpha = jnp.tile(jnp.exp2(m_old - m_new), (1, acc_ref.shape[1] // LANES))
        pv = _dot(p.astype(BF16), v_ref[pl.ds(pl.multiple_of(j * t, t), t), :])
        acc_ref[...] = alpha * acc_ref[...] + pv
        m_ref[...] = m_new

    sa_ref[...] = scores(0)

    def pair(jj, carry):
        j = 2 * jj
        sb_ref[...] = scores(j + 1)
        update(sa_ref, j)
        sa_ref[...] = scores(j + 2)
        update(sb_ref, j + 1)
        return carry

    lax.fori_loop(0, i // 2, pair, 0)

    @pl.when(i % 2 == 0)
    def _():
        update(sa_ref, i, diagonal=True)

    @pl.when(i % 2 == 1)
    def _():
        sb_ref[...] = scores(i)
        update(sa_ref, i - 1)
        update(sb_ref, i, diagonal=True)

    acc = acc_ref[...]
    o_ref[...] = (acc[:, :V_HEAD] / acc[:, V_HEAD:]).astype(BF16)


def _attn_prompt(q, k, v, lp):
    t = _tile(lp, 1024, LANES)
    return pl.pallas_call(
        functools.partial(_flash_body, t=t),
        grid=(MLA_HEADS, lp // t),
        in_specs=[
            pl.BlockSpec((None, t, 2 * LANES), lambda h, i: (h, i, 0)),
            pl.BlockSpec((None, lp, 2 * LANES), lambda h, i: (h, 0, 0)),
            pl.BlockSpec((None, lp, V_HEAD + LANES), lambda h, i: (h, 0, 0)),
        ],
        out_specs=pl.BlockSpec((t, V_HEAD), lambda h, i: (i, h)),
        out_shape=jax.ShapeDtypeStruct((lp, MLA_HEADS * V_HEAD), BF16),
        scratch_shapes=[pltpu.VMEM((t, t), F32), pltpu.VMEM((t, t), F32),
                        pltpu.VMEM((t, LANES), F32), pltpu.VMEM((t, V_HEAD + LANES), F32)],
        compiler_params=_params("parallel", "arbitrary"),
        name="attn_prompt",
    )(q, k, v)


def _attn_sample_body(q_ref, cc_ref, cn_ref, pct_ref, pn_ref, wk_ref, wv_ref, o_ref, c_sc, p_sc, *, past, s, lk):
    c_sc[:past] = cc_ref[...].astype(BF16)
    c_sc[past:past + s] = cn_ref[...].astype(BF16)
    if lk > past + s:
        c_sc[past + s:] = jnp.zeros((lk - past - s, c_sc.shape[1]), BF16)
    c = c_sc[...]
    valid = lax.broadcasted_iota(jnp.int32, (s, lk), 1) < past + s
    ones = jnp.ones((8, QK_NOPE), BF16)
    n_ch = 4 if lk % 64 == 0 else 1
    ch = lk // n_ch
    q_pe = jnp.concatenate([q_ref[hd][:, QK_NOPE:] for hd in range(MLA_HEADS)], axis=0)
    pct = pct_ref[...].astype(BF16)
    pn = pn_ref[...]
    kp_new = jnp.concatenate([(pn + pltpu.roll(pn, QK_ROPE, 1)).astype(BF16),
                              jnp.zeros((lk - past - s, LANES), BF16)], axis=0)
    pe = jnp.concatenate([_dot(q_pe, jnp.concatenate([pct, pct], axis=0)), _dot_nt(q_pe, kp_new)], axis=1)
    inv_l = []
    for pair in range(MLA_HEADS // 2):
        wk = wk_ref[:, pair * 2 * QK_NOPE:(pair + 1) * 2 * QK_NOPE]
        kk = jnp.concatenate([_dot(c[r * ch:(r + 1) * ch], wk) for r in range(n_ch)])
        for hd in (2 * pair, 2 * pair + 1):
            kn = kk[:, (hd % 2) * QK_NOPE:(hd % 2 + 1) * QK_NOPE]
            ms = _dot_nt(ones, (kn * kn).astype(BF16))[0:1] / QK_NOPE
            sc = _dot_nt(q_ref[hd][:, :QK_NOPE], kn.astype(BF16)) * lax.rsqrt(ms + EPS) + pe[hd * s:(hd + 1) * s]
            sc = jnp.where(valid, sc, NEG_BIG)
            p = jnp.exp2(sc - jnp.max(sc, axis=-1, keepdims=True))
            inv_l.append(1.0 / jnp.sum(p, axis=-1, keepdims=True))
            p_sc[hd * s:(hd + 1) * s] = p.astype(BF16)
    ctx = _dot(p_sc[...], c)
    for hd in range(MLA_HEADS):
        ctx_h = (ctx[hd * s:(hd + 1) * s] * inv_l[hd]).astype(BF16)
        o_ref[:, hd * V_HEAD:(hd + 1) * V_HEAD] = _dot(ctx_h, wv_ref[:, hd * V_HEAD:(hd + 1) * V_HEAD]).astype(BF16)


def _attn_sample(q, ckv_cache, ckv, kpe_cache, kpe, wk, wv, e, lp, b, s):
    past, kv_lora = ckv_cache.shape[2], ckv_cache.shape[3]
    lk = past + -(-s // LANES) * LANES
    row0 = lp // s
    return pl.pallas_call(
        functools.partial(_attn_sample_body, past=past, s=s, lk=lk),
        grid=(b,),
        in_specs=[
            pl.BlockSpec((MLA_HEADS, s, 2 * LANES), lambda i: (0, row0 + i, 0)),
            pl.BlockSpec((None, None, past, kv_lora), lambda i: (e, i, 0, 0)),
            pl.BlockSpec((s, kv_lora), lambda i: (row0 + i, 0)),
            pl.BlockSpec((None, None, QK_ROPE, past), lambda i: (e, i, 0, 0)),
            pl.BlockSpec((s, LANES), lambda i: (row0 + i, 0)),
            pl.BlockSpec((None,) + wk.shape[1:], lambda i: (e, 0, 0), pipeline_mode=pl.Buffered(1)),
            pl.BlockSpec((None,) + wv.shape[1:], lambda i: (e, 0, 0), pipeline_mode=pl.Buffered(1)),
        ],
        out_specs=pl.BlockSpec((s, MLA_HEADS * V_HEAD), lambda i: (i, 0)),
        out_shape=jax.ShapeDtypeStruct((b * s, MLA_HEADS * V_HEAD), BF16),
        scratch_shapes=[pltpu.VMEM((lk, kv_lora), BF16), pltpu.VMEM((MLA_HEADS * s, lk), BF16)],
        compiler_params=_params("parallel"),
        name="attn_sample",
    )(q, ckv_cache, ckv, kpe_cache, kpe, wk, wv)


def _pool_segment(ext_ref, pooled_ref, row0, n, pos0, gdim):
    pos = (lax.broadcasted_iota(jnp.int32, (n, 1), 0) + pos0).astype(F32)
    for gi, w in enumerate(POOL_WINDOWS):
        cols = slice(gi * gdim, (gi + 1) * gdim)
        u = ext_ref[HIST_ROWS:HIST_ROWS + n, cols]
        acc = u
        for k in range(1, w):
            acc = acc + ext_ref[HIST_ROWS - k:HIST_ROWS - k + n, cols]
        cnt = jnp.minimum(jnp.float32(w), pos + 1.0)
        pooled_ref[row0:row0 + n, cols] = (acc / cnt - u).astype(BF16)


def _pool_project(pooled_ref, pw_ref, sc_ref, o_ref, gdim):
    for gi in range(len(POOL_WINDOWS)):
        cols = slice(gi * gdim, (gi + 1) * gdim)
        o_ref[:, cols] = (_dot(pooled_ref[:, cols], pw_ref[gi]) * sc_ref[:, cols]).astype(BF16)


def _pool_prompt_body(u_ref, pw_ref, sc_ref, o_ref, ext_ref, pooled_ref, *, tm, gdim):
    i = pl.program_id(0)

    @pl.when(i == 0)
    def _():
        ext_ref[:HIST_ROWS] = jnp.zeros((HIST_ROWS, ext_ref.shape[1]), F32)

    ext_ref[HIST_ROWS:] = u_ref[...]
    _pool_segment(ext_ref, pooled_ref, 0, tm, i * tm, gdim)
    _pool_project(pooled_ref, pw_ref, sc_ref, o_ref, gdim)
    ext_ref[:HIST_ROWS] = ext_ref[tm:tm + HIST_ROWS]


def _pool_sample_body(u_ref, hist_ref, pw_ref, sc_ref, o_ref, ext_ref, pooled_ref, *, nb, s, past, gdim):
    for bi in range(nb):
        ext = ext_ref.at[bi]
        ext[0:1] = jnp.zeros((1, ext_ref.shape[2]), F32)
        ext[1:HIST_ROWS] = hist_ref[bi]
        ext[HIST_ROWS:] = u_ref[bi * s:(bi + 1) * s]
        _pool_segment(ext, pooled_ref, bi * s, s, past, gdim)
    _pool_project(pooled_ref, pw_ref, sc_ref, o_ref, gdim)


def _pool(u, state_pool, pw, scale, e, lp, b, s, past):
    m, pool_dim = u.shape
    ng = len(POOL_WINDOWS)
    gdim = pool_dim // ng
    w_spec = pl.BlockSpec((None, ng, gdim, gdim), lambda i: (e, 0, 0, 0))
    s_spec = pl.BlockSpec((None, 1, pool_dim), lambda i: (e, 0, 0))
    tm = _tile(lp, 512, HIST_ROWS)
    out_p = pl.pallas_call(
        functools.partial(_pool_prompt_body, tm=tm, gdim=gdim),
        grid=(lp // tm,),
        in_specs=[pl.BlockSpec((tm, pool_dim), lambda i: (i, 0)), w_spec, s_spec],
        out_specs=pl.BlockSpec((tm, pool_dim), lambda i: (i, 0)),
        out_shape=jax.ShapeDtypeStruct((lp, pool_dim), BF16),
        scratch_shapes=[pltpu.VMEM((HIST_ROWS + tm, pool_dim), F32), pltpu.VMEM((tm, pool_dim), BF16)],
        compiler_params=_params("arbitrary"),
        name="pool_prompt",
    )(u, pw, scale)
    nb = math.gcd(b, max(1, 256 // s))
    row0 = lp // (nb * s)
    out_s = pl.pallas_call(
        functools.partial(_pool_sample_body, nb=nb, s=s, past=past, gdim=gdim),
        grid=(b // nb,),
        in_specs=[
            pl.BlockSpec((nb * s, pool_dim), lambda i: (row0 + i, 0)),
            pl.BlockSpec((None, nb, POOL_HIST, pool_dim), lambda i: (e, i, 0, 0)),
            w_spec, s_spec,
        ],
        out_specs=pl.BlockSpec((nb * s, pool_dim), lambda i: (i, 0)),
        out_shape=jax.ShapeDtypeStruct((b * s, pool_dim), BF16),
        scratch_shapes=[pltpu.VMEM((nb, HIST_ROWS + s, pool_dim), F32), pltpu.VMEM((nb * s, pool_dim), BF16)],
        compiler_params=_params("parallel"),
        name="pool_sample",
    )(u, state_pool, pw, scale)
    return out_p, out_s


def _even_out_body(x_ref, pp_ref, ps_ref, ap_ref, as_ref, w_ref, o_ref, *, pool_dim, n_prompt_tiles):
    def project(p_ref, a_ref):
        o_ref[...] = x_ref[...] + _dot(p_ref[...], w_ref[:pool_dim]) + _dot(a_ref[...], w_ref[pool_dim:])

    is_prompt = pl.program_id(0) < n_prompt_tiles
    pl.when(is_prompt)(lambda: project(pp_ref, ap_ref))
    pl.when(jnp.logical_not(is_prompt))(lambda: project(ps_ref, as_ref))


def _even_out(x, pool_p, pool_s, attn_p, attn_s, wout, e):
    m, d = x.shape
    lp, pool_dim = pool_p.shape
    attn_dim = attn_p.shape[1]
    tm = _tile(math.gcd(lp, m - lp), 512)
    npt = lp // tm
    prompt_rows = lambda w: pl.BlockSpec((tm, w), lambda i: (jnp.minimum(i, npt - 1), 0))
    sample_rows = lambda w: pl.BlockSpec((tm, w), lambda i: (jnp.maximum(i - npt, 0), 0))
    return pl.pallas_call(
        functools.partial(_even_out_body, pool_dim=pool_dim, n_prompt_tiles=npt),
        grid=(m // tm,),
        in_specs=[
            pl.BlockSpec((tm, d), lambda i: (i, 0)),
            prompt_rows(pool_dim), sample_rows(pool_dim), prompt_rows(attn_dim), sample_rows(attn_dim),
            pl.BlockSpec((None,) + wout.shape[1:], lambda i: (e, 0, 0), pipeline_mode=pl.Buffered(1)),
        ],
        out_specs=pl.BlockSpec((tm, d), lambda i: (i, 0)),
        out_shape=jax.ShapeDtypeStruct((m, d), F32),
        compiler_params=_params("arbitrary"),
        name="even_out",
    )(x, pool_p, pool_s, attn_p, attn_s, wout)


def _odd_in_body(x_ref, g_ref, w_ref, vn_ref, u_ref, v_ref, vf_ref, *, gate):
    h = _rms(x_ref[...], g_ref[...]).astype(BF16)
    v = _rms(jax.nn.gelu(_dot(h, w_ref[:, gate:])), vn_ref[...])
    v_ref[...] = v.astype(BF16)
    vf_ref[...] = v
    n_chunks = 4 if gate % (4 * LANES) == 0 else 1
    cw = gate // n_chunks
    for c in range(n_chunks):
        u_ref[:, c * cw:(c + 1) * cw] = jax.nn.gelu(_dot(h, w_ref[:, c * cw:(c + 1) * cw])).astype(BF16)


def _odd_in(x, g, layer, win, vn, o, lp):
    m, d = x.shape
    gate = win.shape[-1] // 2
    tm = _tile(math.gcd(lp, m - lp), 512)
    first = lp // tm
    return pl.pallas_call(
        functools.partial(_odd_in_body, gate=gate),
        grid=(m // tm,),
        in_specs=[
            pl.BlockSpec((tm, d), lambda i: (i, 0)),
            pl.BlockSpec((None, 1, d), lambda i: (layer, 0, 0)),
            pl.BlockSpec((None, d, 2 * gate), lambda i: (o, 0, 0), pipeline_mode=pl.Buffered(1)),
            pl.BlockSpec((None, 1, gate), lambda i: (o, 0, 0)),
        ],
        out_specs=[
            pl.BlockSpec((tm, gate), lambda i: (i, 0)),
            pl.BlockSpec((tm, gate), lambda i: (i, 0)),
            pl.BlockSpec((tm, gate), lambda i: (jnp.maximum(i - first, 0), 0)),
        ],
        out_shape=[
            jax.ShapeDtypeStruct((m, gate), BF16),
            jax.ShapeDtypeStruct((m, gate), BF16),
            jax.ShapeDtypeStruct((m - lp, gate), F32),
        ],
        compiler_params=_params("arbitrary"),
        name="odd_in",
    )(x, g, win, vn)


def _odd_out_body(x_ref, u_ref, v_ref, ws_ref, b_ref, w_ref, o_ref, us_ref, *, tm, n_prompt_tiles, s):
    is_prompt = pl.program_id(0) < n_prompt_tiles
    ii = lax.broadcasted_iota(jnp.int32, (GMLP_CHUNK, GMLP_CHUNK), 0)
    jj = lax.broadcasted_iota(jnp.int32, (GMLP_CHUNK, GMLP_CHUNK), 1)
    same_stream = jnp.where((ii // s) == (jj // s), 1, 0) + jnp.where(is_prompt, 1, 0)
    keep = (jj <= ii) & (same_stream > 0)
    gdim = u_ref.shape[1] // GMLP_GROUPS
    for gi in range(GMLP_GROUPS):
        cols = slice(gi * gdim, (gi + 1) * gdim)
        wmat = jnp.where(keep, ws_ref[gi], 0.0).astype(BF16)
        bias = b_ref[:, gi:gi + 1]
        for ci in range(tm // GMLP_CHUNK):
            rows = slice(ci * GMLP_CHUNK, (ci + 1) * GMLP_CHUNK)
            sg = _dot(wmat, v_ref[rows, cols]) + bias
            us_ref[rows, cols] = (u_ref[rows, cols].astype(F32) * sg).astype(BF16)
    o_ref[...] = x_ref[...] + _dot(us_ref[...], w_ref[...])


def _odd_out(x, u, v, ws_sel, b_sel, wout, o, lp, s):
    m, d = x.shape
    gate = u.shape[1]
    tm = _tile(math.gcd(lp, m - lp), 512, GMLP_CHUNK)
    npt = lp // tm
    sel = lambda i: jnp.where(i >= npt, 1, 0)
    return pl.pallas_call(
        functools.partial(_odd_out_body, tm=tm, n_prompt_tiles=npt, s=s),
        grid=(m // tm,),
        in_specs=[
            pl.BlockSpec((tm, d), lambda i: (i, 0)),
            pl.BlockSpec((tm, gate), lambda i: (i, 0)),
            pl.BlockSpec((tm, gate), lambda i: (i, 0)),
            pl.BlockSpec((None, None, GMLP_GROUPS, GMLP_CHUNK, GMLP_CHUNK), lambda i: (sel(i), o, 0, 0, 0)),
            pl.BlockSpec((None, None, GMLP_CHUNK, GMLP_GROUPS), lambda i: (sel(i), o, 0, 0)),
            pl.BlockSpec((None,) + wout.shape[1:], lambda i: (o, 0, 0), pipeline_mode=pl.Buffered(1)),
        ],
        out_specs=pl.BlockSpec((tm, d), lambda i: (i, 0)),
        out_shape=jax.ShapeDtypeStruct((m, d), F32),
        scratch_shapes=[pltpu.VMEM((tm, gate), BF16)],
        compiler_params=_params("parallel"),
        name="odd_out",
    )(x, u, v, ws_sel, b_sel, wout)


def kernel(x_prompt, x_sample, cache_mla_ckv, cache_mla_kpe, state_pool, norm_ffn1, norm_mix, norm_ffn2, ffn1_w_gate, ffn1_w_up, ffn1_w_down, ffn2_w_gate, ffn2_w_up, ffn2_w_down, ev_w_in, ev_q_a_norm, ev_kv_a_norm, ev_w_qb, ev_w_kvb, ev_q_nope_norm, ev_q_pe_norm, ev_k_nope_norm, ev_k_pe_norm, ev_pool_w, ev_pool_scale, ev_w_out, od_w_in, od_v_norm, od_w_s, od_b_s, od_w_out):
    bp, lp, d = x_prompt.shape
    b, s, _ = x_sample.shape
    depth = norm_mix.shape[0]
    n_even, n_odd = ev_w_in.shape[0], od_w_in.shape[0]
    past = cache_mla_ckv.shape[2]
    q_lora, kv_lora = ev_q_a_norm.shape[1], ev_kv_a_norm.shape[1]
    pool_dim = ev_pool_scale.shape[1]
    gate = od_v_norm.shape[1]
    m = lp + b * s
    assert bp == 1 and lp % GMLP_CHUNK == 0 and (b * s) % GMLP_CHUNK == 0 and GMLP_CHUNK % s == 0
    assert past % CHUNK == 0 and s <= CHUNK, "every cached and new key must be visible to every sample query"
    assert s >= POOL_HIST and lp >= POOL_HIST and lp % s == 0
    assert ev_w_in.shape[2] == pool_dim + q_lora + kv_lora + QK_ROPE

    vec = lambda a: a[:, None, :]
    bf = lambda a: a.astype(BF16)
    pad_lanes = lambda a: jnp.pad(a, [(0, 0)] * (a.ndim - 1) + [(0, (-a.shape[-1]) % LANES)])

    w_ffn = ((ffn1_w_gate, ffn1_w_up, ffn1_w_down), (ffn2_w_gate, ffn2_w_up, ffn2_w_down))
    g_ffn = (vec(norm_ffn1), vec(norm_ffn2))
    g_mix = vec(norm_mix)
    w_in_e = bf(pad_lanes(ev_w_in))
    wq = ev_w_qb.reshape(n_even, q_lora, MLA_HEADS, QK_HEAD)
    w_qb = bf(jnp.concatenate([wq[..., :QK_NOPE].reshape(n_even, q_lora, -1),
                               wq[..., QK_NOPE:].reshape(n_even, q_lora, -1)], axis=-1))
    w_kvb, w_out_e, pool_w = bf(ev_w_kvb), bf(ev_w_out), bf(ev_pool_w)
    wkv = w_kvb.reshape(n_even, kv_lora, MLA_HEADS, QK_NOPE + V_HEAD)
    w_k = wkv[..., :QK_NOPE].reshape(n_even, kv_lora, -1)
    w_v = wkv[..., QK_NOPE:].reshape(n_even, kv_lora, -1)
    qpn = vec(jnp.tile(ev_q_pe_norm, (1, LANES // QK_ROPE)))
    kpn = vec(pad_lanes(ev_k_pe_norm))
    kpe_cache_t = jnp.swapaxes(cache_mla_kpe, 2, 3)
    w_in_o, w_out_o = bf(od_w_in), bf(od_w_out)
    rep = GMLP_CHUNK // s
    ws_sel = jnp.stack([od_w_s, jnp.tile(od_w_s[:, :, :s, :s], (1, 1, rep, rep))])
    b_sel = jnp.stack([od_b_s, jnp.tile(od_b_s[:, :, :s], (1, 1, rep))]).transpose(0, 1, 3, 2)

    cos, sin = _rope_tables(m, lp, past, s)
    x = jnp.concatenate([x_prompt.reshape(lp, d), x_sample.reshape(b * s, d)], axis=0)
    new_ckv, new_kpe, new_u, new_v = [], [], [], []
    w_cur = tuple(bf(w[0]) for w in w_ffn[0])
    for layer in range(depth):
        x, w_cur = _ffn(x, g_ffn[0], layer, w_cur, w_ffn[1], layer)
        if layer % 2 == 0:
            e = layer // 2
            u, q, ckv, kpe = _even_in(x, g_mix, layer, w_in_e, vec(ev_q_a_norm), vec(ev_kv_a_norm), w_qb,
                                      vec(ev_q_nope_norm), vec(ev_k_nope_norm), qpn, kpn, cos, sin, e,
                                      pool_dim, q_lora, kv_lora)
            k_p, v_p = _kv_prompt(ckv, kpe, w_kvb, e, lp)
            attn_p = _attn_prompt(q, k_p, v_p, lp)
            attn_s = _attn_sample(q, cache_mla_ckv, ckv, kpe_cache_t, kpe, w_k, w_v, e, lp, b, s)
            pool_p, pool_s = _pool(u, state_pool, pool_w, vec(ev_pool_scale), e, lp, b, s, past)
            x = _even_out(x, pool_p, pool_s, attn_p, attn_s, w_out_e, e)
            new_ckv.append((ckv[:lp], ckv[lp:]))
            new_kpe.append((kpe[:lp, :QK_ROPE], kpe[lp:, :QK_ROPE]))
            new_u.append((u[lp - POOL_HIST:lp], u[lp:].reshape(b, s, pool_dim)[:, s - POOL_HIST:]))
        else:
            o = layer // 2
            u, v, v_f32 = _odd_in(x, g_mix, layer, w_in_o, vec(od_v_norm), o, lp)
            x = _odd_out(x, u, v, ws_sel, b_sel, w_out_o, o, lp, s)
            new_v.append(v_f32)
        if layer + 1 < depth:
            x, w_cur = _ffn(x, g_ffn[1], layer, w_cur, w_ffn[0], layer + 1)
        else:
            y_p, (y_s,) = _ffn(x, g_ffn[1], layer, w_cur, split_rows=lp)

    stack = lambda pairs, k, shape: jnp.stack([p[k] for p in pairs]).reshape((n_even,) + shape)
    return (
        y_p.reshape(1, lp, d),
        y_s.reshape(b, s, d),
        stack(new_ckv, 0, (1, lp, kv_lora)),
        stack(new_kpe, 0, (1, lp, QK_ROPE)),
        stack(new_u, 0, (1, POOL_HIST, pool_dim)),
        stack(new_ckv, 1, (b, s, kv_lora)),
        stack(new_kpe, 1, (b, s, QK_ROPE)),
        stack(new_u, 1, (b, POOL_HIST, pool_dim)),
        jnp.stack(new_v).reshape(n_odd, b, s, gate),
    )
```

```python
import functools
import math

import jax
import jax.numpy as jnp
from jax import lax
from jax.experimental import pallas as pl
from jax.experimental.pallas import tpu as pltpu

F32 = jnp.float32
BF16 = jnp.bfloat16

EPS = 1e-6
CHUNK = 64
POOL_WINDOWS = (2, 4, 8, 16)
POOL_HIST = max(POOL_WINDOWS) - 1
HIST_ROWS = 16
MLA_HEADS = 8
QK_NOPE = 128
QK_ROPE = 64
V_HEAD = 128
QK_HEAD = QK_NOPE + QK_ROPE
ATTN_SCALE = QK_HEAD ** -0.5
Q_SCALE = ATTN_SCALE * math.log2(math.e)
ROPE_THETA = 10000.0
GMLP_CHUNK = 128
GMLP_GROUPS = 8
LANES = 128
NEG_BIG = -1e30
VMEM_LIMIT = 60 * 1024 * 1024


def _params(*sem):
    return pltpu.CompilerParams(dimension_semantics=sem, vmem_limit_bytes=VMEM_LIMIT)


def _tile(n, pref, mult=8):
    if n <= pref:
        return n
    for t in range(pref - pref % mult, 0, -mult):
        if n % t == 0:
            return t
    raise ValueError(f"no tile for {n}")


def _const_spec(shape):
    nd = len(shape)
    return pl.BlockSpec(shape, lambda *_: (0,) * nd, pipeline_mode=pl.Buffered(1))


def _rms(x, g):
    ms = jnp.mean(x * x, axis=-1, keepdims=True)
    return x * lax.rsqrt(ms + EPS) * g


def _dot(a, b):
    return jnp.dot(a, b, preferred_element_type=F32)


def _dot_nt(a, b):
    return lax.dot_general(a, b, (((1,), (1,)), ((), ())), preferred_element_type=F32)


def _ffn_body(x_ref, g_ref, wg_ref, wu_ref, wd_ref, *rest, convert_next, n_first):
    *rest, h_ref = rest
    if convert_next:
        (*nxt, o_ref, cg_ref, cu_ref, cd_ref) = rest

        @pl.when(pl.program_id(0) < convert_next)
        def _():
            for src, dst in zip(nxt, (cg_ref, cu_ref, cd_ref)):
                dst[...] = src[...].astype(BF16)

        outs = (o_ref,)
    else:
        outs = tuple(rest)

    def run(o_ref):
        @pl.when(pl.program_id(1) == 0)
        def _():
            x = x_ref[...]
            h_ref[...] = _rms(x, g_ref[...]).astype(BF16)
            o_ref[...] = x

        h = h_ref[...]
        a = _dot(h, wg_ref[...])
        b = _dot(h, wu_ref[...])
        act = (a * jax.nn.sigmoid(a) * b * 0.5).astype(BF16)
        o_ref[...] += _dot(act, wd_ref[...])

    if len(outs) == 1:
        run(outs[0])
    else:
        first = pl.program_id(0) < n_first
        pl.when(first)(lambda: run(outs[0]))
        pl.when(jnp.logical_not(first))(lambda: run(outs[1]))


def _ffn(x, g, layer, w, w_next=None, next_layer=None, split_rows=None):
    m, d = x.shape
    f = w[0].shape[-1]
    tm = _tile(m if split_rows is None else math.gcd(split_rows, m - split_rows), 1024)
    tf = _tile(f, 512 if split_rows is None else 256, LANES)
    n_i = m // tm
    in_specs = [
        pl.BlockSpec((tm, d), lambda i, j: (i, 0)),
        pl.BlockSpec((None, 1, d), lambda i, j: (layer, 0, 0)),
        pl.BlockSpec((d, tf), lambda i, j: (0, j)),
        pl.BlockSpec((d, tf), lambda i, j: (0, j)),
        pl.BlockSpec((tf, d), lambda i, j: (j, 0)),
    ]
    n_first = None
    if split_rows is None:
        out_specs = [pl.BlockSpec((tm, d), lambda i, j: (i, 0))]
        out_shape = [jax.ShapeDtypeStruct((m, d), F32)]
    else:
        assert w_next is None
        n_first = split_rows // tm
        out_specs = [pl.BlockSpec((tm, d), lambda i, j: (jnp.minimum(i, n_first - 1), 0)),
                     pl.BlockSpec((tm, d), lambda i, j: (jnp.maximum(i - n_first, 0), 0),
                                  pipeline_mode=pl.Buffered(1))]
        out_shape = [jax.ShapeDtypeStruct((split_rows, d), F32), jax.ShapeDtypeStruct((m - split_rows, d), F32)]
    nr = 0
    if w_next is not None:
        nr = max(r for r in (1, 2, 4, 8, 16) if r <= n_i and d % (r * LANES) == 0)
        db = d // nr
        n_j = f // tf
        blk = lambda i: jnp.minimum(i, nr - 1)
        col = lambda i, j: jnp.where(i < nr, j, n_j - 1)
        in_specs += [
            pl.BlockSpec((None, db, tf), lambda i, j: (next_layer, blk(i), col(i, j))),
            pl.BlockSpec((None, db, tf), lambda i, j: (next_layer, blk(i), col(i, j))),
            pl.BlockSpec((None, tf, db), lambda i, j: (next_layer, col(i, j), blk(i))),
        ]
        out_specs += [
            pl.BlockSpec((db, tf), lambda i, j: (blk(i), col(i, j))),
            pl.BlockSpec((db, tf), lambda i, j: (blk(i), col(i, j))),
            pl.BlockSpec((tf, db), lambda i, j: (col(i, j), blk(i))),
        ]
        out_shape += [jax.ShapeDtypeStruct((d, f), BF16), jax.ShapeDtypeStruct((d, f), BF16),
                      jax.ShapeDtypeStruct((f, d), BF16)]
    out = pl.pallas_call(
        functools.partial(_ffn_body, convert_next=nr, n_first=n_first),
        grid=(n_i, f // tf),
        in_specs=in_specs,
        out_specs=out_specs,
        out_shape=out_shape,
        scratch_shapes=[pltpu.VMEM((tm, d), BF16)],
        compiler_params=_params("arbitrary", "arbitrary"),
        name="ffn",
    )(x, g, *w, *(w_next or ()))
    return out[0], tuple(out[1:])


def _rope_tab_body(inv_ref, sgn_ref, cos_ref, sin_ref, *, tm, lp, past, s):
    r = lax.broadcasted_iota(jnp.int32, (tm, LANES), 0) + pl.program_id(0) * tm
    t = r - lp
    t = (t & (s - 1)) if s & (s - 1) == 0 else lax.rem(t, s)
    pos = jnp.where(r >= lp, past + t, r).astype(F32)
    ang = pos * inv_ref[...]
    cos_ref[...] = jnp.cos(ang)
    sin_ref[...] = jnp.sin(ang) * sgn_ref[...]


def _rope_tables(m, lp, past, s):
    half = QK_ROPE // 2
    inv = ROPE_THETA ** (-jnp.arange(0, QK_ROPE, 2, dtype=F32) / QK_ROPE)
    inv = jnp.tile(inv, LANES // half)[None, :]
    sgn = jnp.tile(jnp.concatenate([-jnp.ones((half,), F32), jnp.ones((half,), F32)]), LANES // QK_ROPE)[None, :]
    tm = _tile(m, 512)
    return pl.pallas_call(
        functools.partial(_rope_tab_body, tm=tm, lp=lp, past=past, s=s),
        grid=(m // tm,),
        in_specs=[_const_spec((1, LANES)), _const_spec((1, LANES))],
        out_specs=[pl.BlockSpec((tm, LANES), lambda i: (i, 0))] * 2,
        out_shape=[jax.ShapeDtypeStruct((m, LANES), F32)] * 2,
        compiler_params=_params("parallel"),
        name="rope_tables",
    )(inv, sgn)


def _swap_halves(x):
    lane = lax.broadcasted_iota(jnp.int32, x.shape, 1)
    left = pltpu.roll(x, LANES - QK_ROPE // 2, 1)
    right = pltpu.roll(x, QK_ROPE // 2, 1)
    return jnp.where((lane & (QK_ROPE - 1)) < QK_ROPE // 2, left, right)


def _even_in_body(x_ref, g_ref, win_ref, qan_ref, kvan_ref, wqb_ref, qnn_ref, knn_ref, qpn_ref, kpn_ref,
                  cos_ref, sin_ref, u_ref, q_ref, ckv_ref, kpe_ref, *, pool_dim, q_lora, kv_lora):
    h = _rms(x_ref[...], g_ref[...]).astype(BF16)
    o1, o2 = pool_dim + q_lora, pool_dim + q_lora + kv_lora
    cos, sin = cos_ref[...], sin_ref[...]
    lane = lax.broadcasted_iota(jnp.int32, cos.shape, 1)
    lo = lane < QK_ROPE

    def rope(t):
        return t * cos + _swap_halves(t) * sin

    qn = _rms(_dot(h, win_ref[:, pool_dim:o1]), qan_ref[...]).astype(BF16)
    q = _dot(qn, wqb_ref[...])
    zk = _dot(h, win_ref[:, o1:])
    ckv_ref[...] = _rms(zk[:, :kv_lora], kvan_ref[...])
    kp = zk[:, kv_lora:]
    kp = kp * lax.rsqrt(jnp.sum(kp * kp, axis=-1, keepdims=True) / QK_ROPE + EPS) * kpn_ref[...]
    kpe_ref[...] = rope(kp)
    u_ref[...] = _dot(h, win_ref[:, :pool_dim])

    nope_w = MLA_HEADS * QK_NOPE
    k_gain = knn_ref[...] * Q_SCALE
    for hd in range(MLA_HEADS):
        qh = _rms(q[:, hd * QK_NOPE:(hd + 1) * QK_NOPE], qnn_ref[...]) * k_gain
        q_ref[hd, :, :QK_NOPE] = qh.astype(BF16)
    for j in range(MLA_HEADS // 2):
        t = q[:, nope_w + j * LANES:nope_w + (j + 1) * LANES]
        tt = t * t
        s_lo = jnp.sum(jnp.where(lo, tt, 0.0), axis=-1, keepdims=True)
        s_hi = jnp.sum(jnp.where(lo, 0.0, tt), axis=-1, keepdims=True)
        inv = jnp.where(lo, lax.rsqrt(s_lo / QK_ROPE + EPS), lax.rsqrt(s_hi / QK_ROPE + EPS))
        t = rope(t * inv * qpn_ref[...]) * Q_SCALE
        q_ref[2 * j, :, QK_NOPE:] = jnp.where(lo, t, 0.0).astype(BF16)
        q_ref[2 * j + 1, :, QK_NOPE:] = jnp.where(lo, 0.0, t).astype(BF16)


def _even_in(x, g, layer, win, qan, kvan, wqb, qnn, knn, qpn, kpn, cos, sin, e, pool_dim, q_lora, kv_lora):
    m, d = x.shape
    tm = _tile(m, 512)
    n_in = win.shape[-1]
    n_q = wqb.shape[-1]
    row = lambda w: pl.BlockSpec((tm, w), lambda i: (i, 0))
    vec = lambda w, idx: pl.BlockSpec((None, 1, w), lambda i: (idx, 0, 0))
    return pl.pallas_call(
        functools.partial(_even_in_body, pool_dim=pool_dim, q_lora=q_lora, kv_lora=kv_lora),
        grid=(m // tm,),
        in_specs=[
            row(d), vec(d, layer),
            pl.BlockSpec((None, d, n_in), lambda i: (e, 0, 0), pipeline_mode=pl.Buffered(1)),
            vec(q_lora, e), vec(kv_lora, e),
            pl.BlockSpec((None, q_lora, n_q), lambda i: (e, 0, 0), pipeline_mode=pl.Buffered(1)),
            vec(QK_NOPE, e), vec(QK_NOPE, e), vec(LANES, e), vec(LANES, e),
            row(LANES), row(LANES),
        ],
        out_specs=[
            row(pool_dim),
            pl.BlockSpec((MLA_HEADS, tm, 2 * LANES), lambda i: (0, i, 0)),
            row(kv_lora), row(LANES),
        ],
        out_shape=[
            jax.ShapeDtypeStruct((m, pool_dim), F32),
            jax.ShapeDtypeStruct((MLA_HEADS, m, 2 * LANES), BF16),
            jax.ShapeDtypeStruct((m, kv_lora), F32),
            jax.ShapeDtypeStruct((m, LANES), F32),
        ],
        compiler_params=_params("parallel"),
        name="even_in",
    )(x, g, win, qan, kvan, wqb, qnn, knn, qpn, kpn, cos, sin)


def _kv_body(ckv_ref, kpe_ref, w_ref, k_ref, v_ref):
    c = ckv_ref[...].astype(BF16)
    kp = kpe_ref[...]
    kp_both = (kp + pltpu.roll(kp, QK_ROPE, 1)).astype(BF16)
    hw = QK_NOPE + V_HEAD
    for hd in range(MLA_HEADS):
        kv = _dot(c, w_ref[:, hd * hw:(hd + 1) * hw])
        k_ref[hd, :, :QK_NOPE] = _rms(kv[:, :QK_NOPE], 1.0).astype(BF16)
        k_ref[hd, :, QK_NOPE:] = kp_both
        v_ref[hd, :, :V_HEAD] = kv[:, QK_NOPE:].astype(BF16)
        v_ref[hd, :, V_HEAD:] = jnp.ones((c.shape[0], LANES), BF16)


def _kv_prompt(ckv, kpe, wkvb, e, lp):
    kv_lora = ckv.shape[-1]
    tm = _tile(lp, 512)
    return pl.pallas_call(
        _kv_body,
        grid=(lp // tm,),
        in_specs=[
            pl.BlockSpec((tm, kv_lora), lambda i: (i, 0)),
            pl.BlockSpec((tm, LANES), lambda i: (i, 0)),
            pl.BlockSpec((None,) + wkvb.shape[1:], lambda i: (e, 0, 0), pipeline_mode=pl.Buffered(1)),
        ],
        out_specs=[
            pl.BlockSpec((MLA_HEADS, tm, 2 * LANES), lambda i: (0, i, 0)),
            pl.BlockSpec((MLA_HEADS, tm, V_HEAD + LANES), lambda i: (0, i, 0)),
        ],
        out_shape=[
            jax.ShapeDtypeStruct((MLA_HEADS, lp, 2 * LANES), BF16),
            jax.ShapeDtypeStruct((MLA_HEADS, lp, V_HEAD + LANES), BF16),
        ],
        compiler_params=_params("parallel"),
        name="kv_prompt",
    )(ckv, kpe, wkvb)


def _flash_body(q_ref, k_ref, v_ref, o_ref, sa_ref, sb_ref, m_ref, acc_ref, *, t):
    i = pl.program_id(1)
    q = q_ref[...]
    m_ref[...] = jnp.full(m_ref.shape, NEG_BIG, F32)
    acc_ref[...] = jnp.zeros(acc_ref.shape, F32)

    def scores(j):
        return _dot_nt(q, k_ref[pl.ds(pl.multiple_of(j * t, t), t), :])

    def update(s_ref, j, diagonal=False):
        s = s_ref[...]
        if diagonal:
            qc = lax.broadcasted_iota(jnp.int32, s.shape, 0) // CHUNK
            kc = lax.broadcasted_iota(jnp.int32, s.shape, 1) // CHUNK
            s = jnp.where(kc <= qc, s, NEG_BIG)
        m_old = m_ref[...]
        m_new = jnp.maximum(m_old, jnp.max(s, axis=-1, keepdims=True))
        p = jnp.exp2(s - jnp.tile(m_new, (1, t // LANES)))
        alpha = jnp.tile(jnp.exp2(m_old - m_new), (1, acc_ref.shape[1] // LANES))
        pv = _dot(p.astype(BF16), v_ref[pl.ds(pl.multiple_of(j * t, t), t), :])
        acc_ref[...] = alpha * acc_ref[...] + pv
        m_ref[...] = m_new

    sa_ref[...] = scores(0)

    def pair(jj, carry):
        j = 2 * jj
        sb_ref[...] = scores(j + 1)
        update(sa_ref, j)
        sa_ref[...] = scores(j + 2)
        update(sb_ref, j + 1)
        return carry

    lax.fori_loop(0, i // 2, pair, 0)

    @pl.when(i % 2 == 0)
    def _():
        update(sa_ref, i, diagonal=True)

    @pl.when(i % 2 == 1)
    def _():
        sb_ref[...] = scores(i)
        update(sa_ref, i - 1)
        update(sb_ref, i, diagonal=True)

    acc = acc_ref[...]
    o_ref[...] = (acc[:, :V_HEAD] / acc[:, V_HEAD:]).astype(BF16)


def _attn_prompt(q, k, v, lp):
    t = _tile(lp, 1024, LANES)
    return pl.pallas_call(
        functools.partial(_flash_body, t=t),
        grid=(MLA_HEADS, lp // t),
        in_specs=[
            pl.BlockSpec((None, t, 2 * LANES), lambda h, i: (h, i, 0)),
            pl.BlockSpec((None, lp, 2 * LANES), lambda h, i: (h, 0, 0)),
            pl.BlockSpec((None, lp, V_HEAD + LANES), lambda h, i: (h, 0, 0)),
        ],
        out_specs=pl.BlockSpec((t, V_HEAD), lambda h, i: (i, h)),
        out_shape=jax.ShapeDtypeStruct((lp, MLA_HEADS * V_HEAD), BF16),
        scratch_shapes=[pltpu.VMEM((t, t), F32), pltpu.VMEM((t, t), F32),
                        pltpu.VMEM((t, LANES), F32), pltpu.VMEM((t, V_HEAD + LANES), F32)],
        compiler_params=_params("parallel", "arbitrary"),
        name="attn_prompt",
    )(q, k, v)


def _attn_sample_body(q_ref, cc_ref, cn_ref, pct_ref, pn_ref, wk_ref, wv_ref, o_ref, c_sc, p_sc, *, past, s, lk):
    c_sc[:past] = cc_ref[...].astype(BF16)
    c_sc[past:past + s] = cn_ref[...].astype(BF16)
    if lk > past + s:
        c_sc[past + s:] = jnp.zeros((lk - past - s, c_sc.shape[1]), BF16)
    c = c_sc[...]
    valid = lax.broadcasted_iota(jnp.int32, (s, lk), 1) < past + s
    ones = jnp.ones((8, QK_NOPE), BF16)
    n_ch = 4 if lk % 64 == 0 else 1
    ch = lk // n_ch
    q_pe = jnp.concatenate([q_ref[hd][:, QK_NOPE:] for hd in range(MLA_HEADS)], axis=0)
    pct = pct_ref[...].astype(BF16)
    pn = pn_ref[...]
    kp_new = jnp.concatenate([(pn + pltpu.roll(pn, QK_ROPE, 1)).astype(BF16),
                              jnp.zeros((lk - past - s, LANES), BF16)], axis=0)
    pe = jnp.concatenate([_dot(q_pe, jnp.concatenate([pct, pct], axis=0)), _dot_nt(q_pe, kp_new)], axis=1)
    inv_l = []
    for pair in range(MLA_HEADS // 2):
        wk = wk_ref[:, pair * 2 * QK_NOPE:(pair + 1) * 2 * QK_NOPE]
        kk = jnp.concatenate([_dot(c[r * ch:(r + 1) * ch], wk) for r in range(n_ch)])
        for hd in (2 * pair, 2 * pair + 1):
            kn = kk[:, (hd % 2) * QK_NOPE:(hd % 2 + 1) * QK_NOPE]
            ms = _dot_nt(ones, (kn * kn).astype(BF16))[0:1] / QK_NOPE
            sc = _dot_nt(q_ref[hd][:, :QK_NOPE], kn.astype(BF16)) * lax.rsqrt(ms + EPS) + pe[hd * s:(hd + 1) * s]
            sc = jnp.where(valid, sc, NEG_BIG)
            p = jnp.exp2(sc - jnp.max(sc, axis=-1, keepdims=True))
            inv_l.append(1.0 / jnp.sum(p, axis=-1, keepdims=True))
            p_sc[hd * s:(hd + 1) * s] = p.astype(BF16)
    ctx = _dot(p_sc[...], c)
    for hd in range(MLA_HEADS):
        ctx_h = (ctx[hd * s:(hd + 1) * s] * inv_l[hd]).astype(BF16)
        o_ref[:, hd * V_HEAD:(hd + 1) * V_HEAD] = _dot(ctx_h, wv_ref[:, hd * V_HEAD:(hd + 1) * V_HEAD]).astype(BF16)


def _attn_sample(q, ckv_cache, ckv, kpe_cache, kpe, wk, wv, e, lp, b, s):
    past, kv_lora = ckv_cache.shape[2], ckv_cache.shape[3]
    lk = past + -(-s // LANES) * LANES
    row0 = lp // s
    return pl.pallas_call(
        functools.partial(_attn_sample_body, past=past, s=s, lk=lk),
        grid=(b,),
        in_specs=[
            pl.BlockSpec((MLA_HEADS, s, 2 * LANES), lambda i: (0, row0 + i, 0)),
            pl.BlockSpec((None, None, past, kv_lora), lambda i: (e, i, 0, 0)),
            pl.BlockSpec((s, kv_lora), lambda i: (row0 + i, 0)),
            pl.BlockSpec((None, None, QK_ROPE, past), lambda i: (e, i, 0, 0)),
            pl.BlockSpec((s, LANES), lambda i: (row0 + i, 0)),
            pl.BlockSpec((None,) + wk.shape[1:], lambda i: (e, 0, 0), pipeline_mode=pl.Buffered(1)),
            pl.BlockSpec((None,) + wv.shape[1:], lambda i: (e, 0, 0), pipeline_mode=pl.Buffered(1)),
        ],
        out_specs=pl.BlockSpec((s, MLA_HEADS * V_HEAD), lambda i: (i, 0)),
        out_shape=jax.ShapeDtypeStruct((b * s, MLA_HEADS * V_HEAD), BF16),
        scratch_shapes=[pltpu.VMEM((lk, kv_lora), BF16), pltpu.VMEM((MLA_HEADS * s, lk), BF16)],
        compiler_params=_params("parallel"),
        name="attn_sample",
    )(q, ckv_cache, ckv, kpe_cache, kpe, wk, wv)


def _pool_segment(ext_ref, pooled_ref, row0, n, pos0, gdim):
    pos = (lax.broadcasted_iota(jnp.int32, (n, 1), 0) + pos0).astype(F32)
    for gi, w in enumerate(POOL_WINDOWS):
        cols = slice(gi * gdim, (gi + 1) * gdim)
        u = ext_ref[HIST_ROWS:HIST_ROWS + n, cols]
        acc = u
        for k in range(1, w):
            acc = acc + ext_ref[HIST_ROWS - k:HIST_ROWS - k + n, cols]
        cnt = jnp.minimum(jnp.float32(w), pos + 1.0)
        pooled_ref[row0:row0 + n, cols] = (acc / cnt - u).astype(BF16)


def _pool_project(pooled_ref, pw_ref, sc_ref, o_ref, gdim):
    for gi in range(len(POOL_WINDOWS)):
        cols = slice(gi * gdim, (gi + 1) * gdim)
        o_ref[:, cols] = (_dot(pooled_ref[:, cols], pw_ref[gi]) * sc_ref[:, cols]).astype(BF16)


def _pool_prompt_body(u_ref, pw_ref, sc_ref, o_ref, ext_ref, pooled_ref, *, tm, gdim):
    i = pl.program_id(0)

    @pl.when(i == 0)
    def _():
        ext_ref[:HIST_ROWS] = jnp.zeros((HIST_ROWS, ext_ref.shape[1]), F32)

    ext_ref[HIST_ROWS:] = u_ref[...]
    _pool_segment(ext_ref, pooled_ref, 0, tm, i * tm, gdim)
    _pool_project(pooled_ref, pw_ref, sc_ref, o_ref, gdim)
    ext_ref[:HIST_ROWS] = ext_ref[tm:tm + HIST_ROWS]


def _pool_sample_body(u_ref, hist_ref, pw_ref, sc_ref, o_ref, ext_ref, pooled_ref, *, nb, s, past, gdim):
    for bi in range(nb):
        ext = ext_ref.at[bi]
        ext[0:1] = jnp.zeros((1, ext_ref.shape[2]), F32)
        ext[1:HIST_ROWS] = hist_ref[bi]
        ext[HIST_ROWS:] = u_ref[bi * s:(bi + 1) * s]
        _pool_segment(ext, pooled_ref, bi * s, s, past, gdim)
    _pool_project(pooled_ref, pw_ref, sc_ref, o_ref, gdim)


def _pool(u, state_pool, pw, scale, e, lp, b, s, past):
    m, pool_dim = u.shape
    ng = len(POOL_WINDOWS)
    gdim = pool_dim // ng
    w_spec = pl.BlockSpec((None, ng, gdim, gdim), lambda i: (e, 0, 0, 0))
    s_spec = pl.BlockSpec((None, 1, pool_dim), lambda i: (e, 0, 0))
    tm = _tile(lp, 512, HIST_ROWS)
    out_p = pl.pallas_call(
        functools.partial(_pool_prompt_body, tm=tm, gdim=gdim),
        grid=(lp // tm,),
        in_specs=[pl.BlockSpec((tm, pool_dim), lambda i: (i, 0)), w_spec, s_spec],
        out_specs=pl.BlockSpec((tm, pool_dim), lambda i: (i, 0)),
        out_shape=jax.ShapeDtypeStruct((lp, pool_dim), BF16),
        scratch_shapes=[pltpu.VMEM((HIST_ROWS + tm, pool_dim), F32), pltpu.VMEM((tm, pool_dim), BF16)],
        compiler_params=_params("arbitrary"),
        name="pool_prompt",
    )(u, pw, scale)
    nb = math.gcd(b, max(1, 256 // s))
    row0 = lp // (nb * s)
    out_s = pl.pallas_call(
        functools.partial(_pool_sample_body, nb=nb, s=s, past=past, gdim=gdim),
        grid=(b // nb,),
        in_specs=[
            pl.BlockSpec((nb * s, pool_dim), lambda i: (row0 + i, 0)),
            pl.BlockSpec((None, nb, POOL_HIST, pool_dim), lambda i: (e, i, 0, 0)),
            w_spec, s_spec,
        ],
        out_specs=pl.BlockSpec((nb * s, pool_dim), lambda i: (i, 0)),
        out_shape=jax.ShapeDtypeStruct((b * s, pool_dim), BF16),
        scratch_shapes=[pltpu.VMEM((nb, HIST_ROWS + s, pool_dim), F32), pltpu.VMEM((nb * s, pool_dim), BF16)],
        compiler_params=_params("parallel"),
        name="pool_sample",
    )(u, state_pool, pw, scale)
    return out_p, out_s


def _even_out_body(x_ref, pp_ref, ps_ref, ap_ref, as_ref, w_ref, o_ref, *, pool_dim, n_prompt_tiles):
    def project(p_ref, a_ref):
        o_ref[...] = x_ref[...] + _dot(p_ref[...], w_ref[:pool_dim]) + _dot(a_ref[...], w_ref[pool_dim:])

    is_prompt = pl.program_id(0) < n_prompt_tiles
    pl.when(is_prompt)(lambda: project(pp_ref, ap_ref))
    pl.when(jnp.logical_not(is_prompt))(lambda: project(ps_ref, as_ref))


def _even_out(x, pool_p, pool_s, attn_p, attn_s, wout, e):
    m, d = x.shape
    lp, pool_dim = pool_p.shape
    attn_dim = attn_p.shape[1]
    tm = _tile(math.gcd(lp, m - lp), 512)
    npt = lp // tm
    prompt_rows = lambda w: pl.BlockSpec((tm, w), lambda i: (jnp.minimum(i, npt - 1), 0))
    sample_rows = lambda w: pl.BlockSpec((tm, w), lambda i: (jnp.maximum(i - npt, 0), 0))
    return pl.pallas_call(
        functools.partial(_even_out_body, pool_dim=pool_dim, n_prompt_tiles=npt),
        grid=(m // tm,),
        in_specs=[
            pl.BlockSpec((tm, d), lambda i: (i, 0)),
            prompt_rows(pool_dim), sample_rows(pool_dim), prompt_rows(attn_dim), sample_rows(attn_dim),
            pl.BlockSpec((None,) + wout.shape[1:], lambda i: (e, 0, 0), pipeline_mode=pl.Buffered(1)),
        ],
        out_specs=pl.BlockSpec((tm, d), lambda i: (i, 0)),
        out_shape=jax.ShapeDtypeStruct((m, d), F32),
        compiler_params=_params("arbitrary"),
        name="even_out",
    )(x, pool_p, pool_s, attn_p, attn_s, wout)


def _odd_in_body(x_ref, g_ref, w_ref, vn_ref, u_ref, v_ref, vf_ref, *, gate):
    h = _rms(x_ref[...], g_ref[...]).astype(BF16)
    v = _rms(jax.nn.gelu(_dot(h, w_ref[:, gate:])), vn_ref[...])
    v_ref[...] = v.astype(BF16)
    vf_ref[...] = v
    n_chunks = 4 if gate % (4 * LANES) == 0 else 1
    cw = gate // n_chunks
    for c in range(n_chunks):
        u_ref[:, c * cw:(c + 1) * cw] = jax.nn.gelu(_dot(h, w_ref[:, c * cw:(c + 1) * cw])).astype(BF16)


def _odd_in(x, g, layer, win, vn, o, lp):
    m, d = x.shape
    gate = win.shape[-1] // 2
    tm = _tile(math.gcd(lp, m - lp), 512)
    first = lp // tm
    return pl.pallas_call(
        functools.partial(_odd_in_body, gate=gate),
        grid=(m // tm,),
        in_specs=[
            pl.BlockSpec((tm, d), lambda i: (i, 0)),
            pl.BlockSpec((None, 1, d), lambda i: (layer, 0, 0)),
            pl.BlockSpec((None, d, 2 * gate), lambda i: (o, 0, 0), pipeline_mode=pl.Buffered(1)),
            pl.BlockSpec((None, 1, gate), lambda i: (o, 0, 0)),
        ],
        out_specs=[
            pl.BlockSpec((tm, gate), lambda i: (i, 0)),
            pl.BlockSpec((tm, gate), lambda i: (i, 0)),
            pl.BlockSpec((tm, gate), lambda i: (jnp.maximum(i - first, 0), 0)),
        ],
        out_shape=[
            jax.ShapeDtypeStruct((m, gate), BF16),
            jax.ShapeDtypeStruct((m, gate), BF16),
            jax.ShapeDtypeStruct((m - lp, gate), F32),
        ],
        compiler_params=_params("arbitrary"),
        name="odd_in",
    )(x, g, win, vn)


def _odd_out_body(x_ref, u_ref, v_ref, ws_ref, b_ref, w_ref, o_ref, us_ref, *, tm, n_prompt_tiles, s):
    is_prompt = pl.program_id(0) < n_prompt_tiles
    ii = lax.broadcasted_iota(jnp.int32, (GMLP_CHUNK, GMLP_CHUNK), 0)
    jj = lax.broadcasted_iota(jnp.int32, (GMLP_CHUNK, GMLP_CHUNK), 1)
    same_stream = jnp.where((ii // s) == (jj // s), 1, 0) + jnp.where(is_prompt, 1, 0)
    keep = (jj <= ii) & (same_stream > 0)
    gdim = u_ref.shape[1] // GMLP_GROUPS
    for gi in range(GMLP_GROUPS):
        cols = slice(gi * gdim, (gi + 1) * gdim)
        wmat = jnp.where(keep, ws_ref[gi], 0.0).astype(BF16)
        bias = b_ref[:, gi:gi + 1]
        for ci in range(tm // GMLP_CHUNK):
            rows = slice(ci * GMLP_CHUNK, (ci + 1) * GMLP_CHUNK)
            sg = _dot(wmat, v_ref[rows, cols]) + bias
            us_ref[rows, cols] = (u_ref[rows, cols].astype(F32) * sg).astype(BF16)
    o_ref[...] = x_ref[...] + _dot(us_ref[...], w_ref[...])


def _odd_out(x, u, v, ws_sel, b_sel, wout, o, lp, s):
    m, d = x.shape
    gate = u.shape[1]
    tm = _tile(math.gcd(lp, m - lp), 512, GMLP_CHUNK)
    npt = lp // tm
    sel = lambda i: jnp.where(i >= npt, 1, 0)
    return pl.pallas_call(
        functools.partial(_odd_out_body, tm=tm, n_prompt_tiles=npt, s=s),
        grid=(m // tm,),
        in_specs=[
            pl.BlockSpec((tm, d), lambda i: (i, 0)),
            pl.BlockSpec((tm, gate), lambda i: (i, 0)),
            pl.BlockSpec((tm, gate), lambda i: (i, 0)),
            pl.BlockSpec((None, None, GMLP_GROUPS, GMLP_CHUNK, GMLP_CHUNK), lambda i: (sel(i), o, 0, 0, 0)),
            pl.BlockSpec((None, None, GMLP_CHUNK, GMLP_GROUPS), lambda i: (sel(i), o, 0, 0)),
            pl.BlockSpec((None,) + wout.shape[1:], lambda i: (o, 0, 0), pipeline_mode=pl.Buffered(1)),
        ],
        out_specs=pl.BlockSpec((tm, d), lambda i: (i, 0)),
        out_shape=jax.ShapeDtypeStruct((m, d), F32),
        scratch_shapes=[pltpu.VMEM((tm, gate), BF16)],
        compiler_params=_params("parallel"),
        name="odd_out",
    )(x, u, v, ws_sel, b_sel, wout)


def kernel(x_prompt, x_sample, cache_mla_ckv, cache_mla_kpe, state_pool, norm_ffn1, norm_mix, norm_ffn2, ffn1_w_gate, ffn1_w_up, ffn1_w_down, ffn2_w_gate, ffn2_w_up, ffn2_w_down, ev_w_in, ev_q_a_norm, ev_kv_a_norm, ev_w_qb, ev_w_kvb, ev_q_nope_norm, ev_q_pe_norm, ev_k_nope_norm, ev_k_pe_norm, ev_pool_w, ev_pool_scale, ev_w_out, od_w_in, od_v_norm, od_w_s, od_b_s, od_w_out):
    bp, lp, d = x_prompt.shape
    b, s, _ = x_sample.shape
    depth = norm_mix.shape[0]
    n_even, n_odd = ev_w_in.shape[0], od_w_in.shape[0]
    past = cache_mla_ckv.shape[2]
    q_lora, kv_lora = ev_q_a_norm.shape[1], ev_kv_a_norm.shape[1]
    pool_dim = ev_pool_scale.shape[1]
    gate = od_v_norm.shape[1]
    m = lp + b * s
    assert bp == 1 and lp % GMLP_CHUNK == 0 and (b * s) % GMLP_CHUNK == 0 and GMLP_CHUNK % s == 0
    assert past % CHUNK == 0 and s <= CHUNK, "every cached and new key must be visible to every sample query"
    assert s >= POOL_HIST and lp >= POOL_HIST and lp % s == 0
    assert ev_w_in.shape[2] == pool_dim + q_lora + kv_lora + QK_ROPE

    vec = lambda a: a[:, None, :]
    bf = lambda a: a.astype(BF16)
    pad_lanes = lambda a: jnp.pad(a, [(0, 0)] * (a.ndim - 1) + [(0, (-a.shape[-1]) % LANES)])

    w_ffn = ((ffn1_w_gate, ffn1_w_up, ffn1_w_down), (ffn2_w_gate, ffn2_w_up, ffn2_w_down))
    g_ffn = (vec(norm_ffn1), vec(norm_ffn2))
    g_mix = vec(norm_mix)
    w_in_e = bf(pad_lanes(ev_w_in))
    wq = ev_w_qb.reshape(n_even, q_lora, MLA_HEADS, QK_HEAD)
    w_qb = bf(jnp.concatenate([wq[..., :QK_NOPE].reshape(n_even, q_lora, -1),
                               wq[..., QK_NOPE:].reshape(n_even, q_lora, -1)], axis=-1))
    w_kvb, w_out_e, pool_w = bf(ev_w_kvb), bf(ev_w_out), bf(ev_pool_w)
    wkv = w_kvb.reshape(n_even, kv_lora, MLA_HEADS, QK_NOPE + V_HEAD)
    w_k = wkv[..., :QK_NOPE].reshape(n_even, kv_lora, -1)
    w_v = wkv[..., QK_NOPE:].reshape(n_even, kv_lora, -1)
    qpn = vec(jnp.tile(ev_q_pe_norm, (1, LANES // QK_ROPE)))
    kpn = vec(pad_lanes(ev_k_pe_norm))
    kpe_cache_t = jnp.swapaxes(cache_mla_kpe, 2, 3)
    w_in_o, w_out_o = bf(od_w_in), bf(od_w_out)
    rep = GMLP_CHUNK // s
    ws_sel = jnp.stack([od_w_s, jnp.tile(od_w_s[:, :, :s, :s], (1, 1, rep, rep))])
    b_sel = jnp.stack([od_b_s, jnp.tile(od_b_s[:, :, :s], (1, 1, rep))]).transpose(0, 1, 3, 2)

    cos, sin = _rope_tables(m, lp, past, s)
    x = jnp.concatenate([x_prompt.reshape(lp, d), x_sample.reshape(b * s, d)], axis=0)
    new_ckv, new_kpe, new_u, new_v = [], [], [], []
    w_cur = tuple(bf(w[0]) for w in w_ffn[0])
    for layer in range(depth):
        x, w_cur = _ffn(x, g_ffn[0], layer, w_cur, w_ffn[1], layer)
        if layer % 2 == 0:
            e = layer // 2
            u, q, ckv, kpe = _even_in(x, g_mix, layer, w_in_e, vec(ev_q_a_norm), vec(ev_kv_a_norm), w_qb,
                                      vec(ev_q_nope_norm), vec(ev_k_nope_norm), qpn, kpn, cos, sin, e,
                                      pool_dim, q_lora, kv_lora)
            k_p, v_p = _kv_prompt(ckv, kpe, w_kvb, e, lp)
            attn_p = _attn_prompt(q, k_p, v_p, lp)
            attn_s = _attn_sample(q, cache_mla_ckv, ckv, kpe_cache_t, kpe, w_k, w_v, e, lp, b, s)
            pool_p, pool_s = _pool(u, state_pool, pool_w, vec(ev_pool_scale), e, lp, b, s, past)
            x = _even_out(x, pool_p, pool_s, attn_p, attn_s, w_out_e, e)
            new_ckv.append((ckv[:lp], ckv[lp:]))
            new_kpe.append((kpe[:lp, :QK_ROPE], kpe[lp:, :QK_ROPE]))
            new_u.append((u[lp - POOL_HIST:lp], u[lp:].reshape(b, s, pool_dim)[:, s - POOL_HIST:]))
        else:
            o = layer // 2
            u, v, v_f32 = _odd_in(x, g_mix, layer, w_in_o, vec(od_v_norm), o, lp)
            x = _odd_out(x, u, v, ws_sel, b_sel, w_out_o, o, lp, s)
            new_v.append(v_f32)
        if layer + 1 < depth:
            x, w_cur = _ffn(x, g_ffn[1], layer, w_cur, w_ffn[0], layer + 1)
        else:
            y_p, (y_s,) = _ffn(x, g_ffn[1], layer, w_cur, split_rows=lp)

    stack = lambda pairs, k, shape: jnp.stack([p[k] for p in pairs]).reshape((n_even,) + shape)
    return (
        y_p.reshape(1, lp, d),
        y_s.reshape(b, s, d),
        stack(new_ckv, 0, (1, lp, kv_lora)),
        stack(new_kpe, 0, (1, lp, QK_ROPE)),
        stack(new_u, 0, (1, POOL_HIST, pool_dim)),
        stack(new_ckv, 1, (b, s, kv_lora)),
        stack(new_kpe, 1, (b, s, QK_ROPE)),
        stack(new_u, 1, (b, POOL_HIST, pool_dim)),
        jnp.stack(new_v).reshape(n_odd, b, s, gate),
    )
```

```python
import functools
import math

import jax
import jax.numpy as jnp
from jax import lax
from jax.experimental import pallas as pl
from jax.experimental.pallas import tpu as pltpu

F32 = jnp.float32
BF16 = jnp.bfloat16

EPS = 1e-6
CHUNK = 64
POOL_WINDOWS = (2, 4, 8, 16)
POOL_HIST = max(POOL_WINDOWS) - 1
HIST_ROWS = 16
MLA_HEADS = 8
QK_NOPE = 128
QK_ROPE = 64
V_HEAD = 128
QK_HEAD = QK_NOPE + QK_ROPE
ATTN_SCALE = QK_HEAD ** -0.5
Q_SCALE = ATTN_SCALE * math.log2(math.e)
ROPE_THETA = 10000.0
GMLP_CHUNK = 128
GMLP_GROUPS = 8
LANES = 128
NEG_BIG = -1e30
VMEM_LIMIT = 60 * 1024 * 1024


def _params(*sem):
    return pltpu.CompilerParams(dimension_semantics=sem, vmem_limit_bytes=VMEM_LIMIT)


def _tile(n, pref, mult=8):
    if n <= pref:
        return n
    for t in range(pref - pref % mult, 0, -mult):
        if n % t == 0:
            return t
    raise ValueError(f"no tile for {n}")


def _const_spec(shape):
    nd = len(shape)
    return pl.BlockSpec(shape, lambda *_: (0,) * nd, pipeline_mode=pl.Buffered(1))


def _rms(x, g):
    ms = jnp.mean(x * x, axis=-1, keepdims=True)
    return x * lax.rsqrt(ms + EPS) * g


def _dot(a, b):
    return jnp.dot(a, b, preferred_element_type=F32)


def _dot_nt(a, b):
    return lax.dot_general(a, b, (((1,), (1,)), ((), ())), preferred_element_type=F32)


def _ffn_body(x_ref, g_ref, wg_ref, wu_ref, wd_ref, *rest, convert_next, n_first):
    *rest, h_ref = rest
    if convert_next:
        (*nxt, o_ref, cg_ref, cu_ref, cd_ref) = rest

        @pl.when(pl.program_id(0) < convert_next)
        def _():
            for src, dst in zip(nxt, (cg_ref, cu_ref, cd_ref)):
                dst[...] = src[...].astype(BF16)

        outs = (o_ref,)
    else:
        outs = tuple(rest)

    def run(o_ref):
        @pl.when(pl.program_id(1) == 0)
        def _():
            x = x_ref[...]
            h_ref[...] = _rms(x, g_ref[...]).astype(BF16)
            o_ref[...] = x

        h = h_ref[...]
        a = _dot(h, wg_ref[...])
        b = _dot(h, wu_ref[...])
        act = (a * jax.nn.sigmoid(a) * b * 0.5).astype(BF16)
        o_ref[...] += _dot(act, wd_ref[...])

    if len(outs) == 1:
        run(outs[0])
    else:
        first = pl.program_id(0) < n_first
        pl.when(first)(lambda: run(outs[0]))
        pl.when(jnp.logical_not(first))(lambda: run(outs[1]))


def _ffn(x, g, layer, w, w_next=None, next_layer=None, split_rows=None):
    m, d = x.shape
    f = w[0].shape[-1]
    tm = _tile(m if split_rows is None else math.gcd(split_rows, m - split_rows), 1024)
    tf = _tile(f, 512 if split_rows is None else 256, LANES)
    n_i = m // tm
    in_specs = [
        pl.BlockSpec((tm, d), lambda i, j: (i, 0)),
        pl.BlockSpec((None, 1, d), lambda i, j: (layer, 0, 0)),
        pl.BlockSpec((d, tf), lambda i, j: (0, j)),
        pl.BlockSpec((d, tf), lambda i, j: (0, j)),
        pl.BlockSpec((tf, d), lambda i, j: (j, 0)),
    ]
    n_first = None
    if split_rows is None:
        out_specs = [pl.BlockSpec((tm, d), lambda i, j: (i, 0))]
        out_shape = [jax.ShapeDtypeStruct((m, d), F32)]
    else:
        assert w_next is None
        n_first = split_rows // tm
        out_specs = [pl.BlockSpec((tm, d), lambda i, j: (jnp.minimum(i, n_first - 1), 0)),
                     pl.BlockSpec((tm, d), lambda i, j: (jnp.maximum(i - n_first, 0), 0),
                                  pipeline_mode=pl.Buffered(1))]
        out_shape = [jax.ShapeDtypeStruct((split_rows, d), F32), jax.ShapeDtypeStruct((m - split_rows, d), F32)]
    nr = 0
    if w_next is not None:
        nr = max(r for r in (1, 2, 4, 8, 16) if r <= n_i and d % (r * LANES) == 0)
        db = d // nr
        n_j = f // tf
        blk = lambda i: jnp.minimum(i, nr - 1)
        col = lambda i, j: jnp.where(i < nr, j, n_j - 1)
        in_specs += [
            pl.BlockSpec((None, db, tf), lambda i, j: (next_layer, blk(i), col(i, j))),
            pl.BlockSpec((None, db, tf), lambda i, j: (next_layer, blk(i), col(i, j))),
            pl.BlockSpec((None, tf, db), lambda i, j: (next_layer, col(i, j), blk(i))),
        ]
        out_specs += [
            pl.BlockSpec((db, tf), lambda i, j: (blk(i), col(i, j))),
            pl.BlockSpec((db, tf), lambda i, j: (blk(i), col(i, j))),
            pl.BlockSpec((tf, db), lambda i, j: (col(i, j), blk(i))),
        ]
        out_shape += [jax.ShapeDtypeStruct((d, f), BF16), jax.ShapeDtypeStruct((d, f), BF16),
                      jax.ShapeDtypeStruct((f, d), BF16)]
    out = pl.pallas_call(
        functools.partial(_ffn_body, convert_next=nr, n_first=n_first),
        grid=(n_i, f // tf),
        in_specs=in_specs,
        out_specs=out_specs,
        out_shape=out_shape,
        scratch_shapes=[pltpu.VMEM((tm, d), BF16)],
        compiler_params=_params("arbitrary", "arbitrary"),
        name="ffn",
    )(x, g, *w, *(w_next or ()))
    return out[0], tuple(out[1:])


def _rope_tab_body(inv_ref, sgn_ref, cos_ref, sin_ref, *, tm, lp, past, s):
    r = lax.broadcasted_iota(jnp.int32, (tm, LANES), 0) + pl.program_id(0) * tm
    t = r - lp
    t = (t & (s - 1)) if s & (s - 1) == 0 else lax.rem(t, s)
    pos = jnp.where(r >= lp, past + t, r).astype(F32)
    ang = pos * inv_ref[...]
    cos_ref[...] = jnp.cos(ang)
    sin_ref[...] = jnp.sin(ang) * sgn_ref[...]


def _rope_tables(m, lp, past, s):
    half = QK_ROPE // 2
    inv = ROPE_THETA ** (-jnp.arange(0, QK_ROPE, 2, dtype=F32) / QK_ROPE)
    inv = jnp.tile(inv, LANES // half)[None, :]
    sgn = jnp.tile(jnp.concatenate([-jnp.ones((half,), F32), jnp.ones((half,), F32)]), LANES // QK_ROPE)[None, :]
    tm = _tile(m, 512)
    return pl.pallas_call(
        functools.partial(_rope_tab_body, tm=tm, lp=lp, past=past, s=s),
        grid=(m // tm,),
        in_specs=[_const_spec((1, LANES)), _const_spec((1, LANES))],
        out_specs=[pl.BlockSpec((tm, LANES), lambda i: (i, 0))] * 2,
        out_shape=[jax.ShapeDtypeStruct((m, LANES), F32)] * 2,
        compiler_params=_params("parallel"),
        name="rope_tables",
    )(inv, sgn)


def _swap_halves(x):
    lane = lax.broadcasted_iota(jnp.int32, x.shape, 1)
    left = pltpu.roll(x, LANES - QK_ROPE // 2, 1)
    right = pltpu.roll(x, QK_ROPE // 2, 1)
    return jnp.where((lane & (QK_ROPE - 1)) < QK_ROPE // 2, left, right)


def _even_in_body(x_ref, g_ref, win_ref, qan_ref, kvan_ref, wqb_ref, qnn_ref, knn_ref, qpn_ref, kpn_ref,
                  cos_ref, sin_ref, u_ref, q_ref, ckv_ref, kpe_ref, *, pool_dim, q_lora, kv_lora):
    h = _rms(x_ref[...], g_ref[...]).astype(BF16)
    o1, o2 = pool_dim + q_lora, pool_dim + q_lora + kv_lora
    cos, sin = cos_ref[...], sin_ref[...]
    lane = lax.broadcasted_iota(jnp.int32, cos.shape, 1)
    lo = lane < QK_ROPE

    def rope(t):
        return t * cos + _swap_halves(t) * sin

    qn = _rms(_dot(h, win_ref[:, pool_dim:o1]), qan_ref[...]).astype(BF16)
    q = _dot(qn, wqb_ref[...])
    zk = _dot(h, win_ref[:, o1:])
    ckv_ref[...] = _rms(zk[:, :kv_lora], kvan_ref[...])
    kp = zk[:, kv_lora:]
    kp = kp * lax.rsqrt(jnp.sum(kp * kp, axis=-1, keepdims=True) / QK_ROPE + EPS) * kpn_ref[...]
    kpe_ref[...] = rope(kp)
    u_ref[...] = _dot(h, win_ref[:, :pool_dim])

    nope_w = MLA_HEADS * QK_NOPE
    k_gain = knn_ref[...] * Q_SCALE
    for hd in range(MLA_HEADS):
        qh = _rms(q[:, hd * QK_NOPE:(hd + 1) * QK_NOPE], qnn_ref[...]) * k_gain
        q_ref[hd, :, :QK_NOPE] = qh.astype(BF16)
    for j in range(MLA_HEADS // 2):
        t = q[:, nope_w + j * LANES:nope_w + (j + 1) * LANES]
        tt = t * t
        s_lo = jnp.sum(jnp.where(lo, tt, 0.0), axis=-1, keepdims=True)
        s_hi = jnp.sum(jnp.where(lo, 0.0, tt), axis=-1, keepdims=True)
        inv = jnp.where(lo, lax.rsqrt(s_lo / QK_ROPE + EPS), lax.rsqrt(s_hi / QK_ROPE + EPS))
        t = rope(t * inv * qpn_ref[...]) * Q_SCALE
        q_ref[2 * j, :, QK_NOPE:] = jnp.where(lo, t, 0.0).astype(BF16)
        q_ref[2 * j + 1, :, QK_NOPE:] = jnp.where(lo, 0.0, t).astype(BF16)


def _even_in(x, g, layer, win, qan, kvan, wqb, qnn, knn, qpn, kpn, cos, sin, e, pool_dim, q_lora, kv_lora):
    m, d = x.shape
    tm = _tile(m, 512)
    n_in = win.shape[-1]
    n_q = wqb.shape[-1]
    row = lambda w: pl.BlockSpec((tm, w), lambda i: (i, 0))
    vec = lambda w, idx: pl.BlockSpec((None, 1, w), lambda i: (idx, 0, 0))
    return pl.pallas_call(
        functools.partial(_even_in_body, pool_dim=pool_dim, q_lora=q_lora, kv_lora=kv_lora),
        grid=(m // tm,),
        in_specs=[
            row(d), vec(d, layer),
            pl.BlockSpec((None, d, n_in), lambda i: (e, 0, 0), pipeline_mode=pl.Buffered(1)),
            vec(q_lora, e), vec(kv_lora, e),
            pl.BlockSpec((None, q_lora, n_q), lambda i: (e, 0, 0), pipeline_mode=pl.Buffered(1)),
            vec(QK_NOPE, e), vec(QK_NOPE, e), vec(LANES, e), vec(LANES, e),
            row(LANES), row(LANES),
        ],
        out_specs=[
            row(pool_dim),
            pl.BlockSpec((MLA_HEADS, tm, 2 * LANES), lambda i: (0, i, 0)),
            row(kv_lora), row(LANES),
        ],
        out_shape=[
            jax.ShapeDtypeStruct((m, pool_dim), F32),
            jax.ShapeDtypeStruct((MLA_HEADS, m, 2 * LANES), BF16),
            jax.ShapeDtypeStruct((m, kv_lora), F32),
            jax.ShapeDtypeStruct((m, LANES), F32),
        ],
        compiler_params=_params("parallel"),
        name="even_in",
    )(x, g, win, qan, kvan, wqb, qnn, knn, qpn, kpn, cos, sin)


def _kv_body(ckv_ref, kpe_ref, w_ref, k_ref, v_ref):
    c = ckv_ref[...].astype(BF16)
    kp = kpe_ref[...]
    kp_both = (kp + pltpu.roll(kp, QK_ROPE, 1)).astype(BF16)
    hw = QK_NOPE + V_HEAD
    for hd in range(MLA_HEADS):
        kv = _dot(c, w_ref[:, hd * hw:(hd + 1) * hw])
        k_ref[hd, :, :QK_NOPE] = _rms(kv[:, :QK_NOPE], 1.0).astype(BF16)
        k_ref[hd, :, QK_NOPE:] = kp_both
        v_ref[hd, :, :V_HEAD] = kv[:, QK_NOPE:].astype(BF16)
        v_ref[hd, :, V_HEAD:] = jnp.ones((c.shape[0], LANES), BF16)


def _kv_prompt(ckv, kpe, wkvb, e, lp):
    kv_lora = ckv.shape[-1]
    tm = _tile(lp, 512)
    return pl.pallas_call(
        _kv_body,
        grid=(lp // tm,),
        in_specs=[
            pl.BlockSpec((tm, kv_lora), lambda i: (i, 0)),
            pl.BlockSpec((tm, LANES), lambda i: (i, 0)),
            pl.BlockSpec((None,) + wkvb.shape[1:], lambda i: (e, 0, 0), pipeline_mode=pl.Buffered(1)),
        ],
        out_specs=[
            pl.BlockSpec((MLA_HEADS, tm, 2 * LANES), lambda i: (0, i, 0)),
            pl.BlockSpec((MLA_HEADS, tm, V_HEAD + LANES), lambda i: (0, i, 0)),
        ],
        out_shape=[
            jax.ShapeDtypeStruct((MLA_HEADS, lp, 2 * LANES), BF16),
            jax.ShapeDtypeStruct((MLA_HEADS, lp, V_HEAD + LANES), BF16),
        ],
        compiler_params=_params("parallel"),
        name="kv_prompt",
    )(ckv, kpe, wkvb)


def _flash_body(q_ref, k_ref, v_ref, o_ref, sa_ref, sb_ref, m_ref, acc_ref, *, t):
    i = pl.program_id(1)
    q = q_ref[...]
    m_ref[...] = jnp.full(m_ref.shape, NEG_BIG, F32)
    acc_ref[...] = jnp.zeros(acc_ref.shape, F32)

    def scores(j):
        return _dot_nt(q, k_ref[pl.ds(pl.multiple_of(j * t, t), t), :])

    def update(s_ref, j, diagonal=False):
        s = s_ref[...]
        if diagonal:
            qc = lax.broadcasted_iota(jnp.int32, s.shape, 0) // CHUNK
            kc = lax.broadcasted_iota(jnp.int32, s.shape, 1) // CHUNK
            s = jnp.where(kc <= qc, s, NEG_BIG)
        m_old = m_ref[...]
        m_new = jnp.maximum(m_old, jnp.max(s, axis=-1, keepdims=True))
        p = jnp.exp2(s - jnp.tile(m_new, (1, t // LANES)))
        alpha = jnp.tile(jnp.exp2(m_old - m_new), (1, acc_ref.shape[1] // LANES))
        pv = _dot(p.astype(BF16), v_ref[pl.ds(pl.multiple_of(j * t, t), t), :])
        acc_ref[...] = alpha * acc_ref[...] + pv
        m_ref[...] = m_new

    sa_ref[...] = scores(0)

    def pair(jj, carry):
        j = 2 * jj
        sb_ref[...] = scores(j + 1)
        update(sa_ref, j)
        sa_ref[...] = scores(j + 2)
        update(sb_ref, j + 1)
        return carry

    lax.fori_loop(0, i // 2, pair, 0)

    @pl.when(i % 2 == 0)
    def _():
        update(sa_ref, i, diagonal=True)

    @pl.when(i % 2 == 1)
    def _():
        sb_ref[...] = scores(i)
        update(sa_ref, i - 1)
        update(sb_ref, i, diagonal=True)

    acc = acc_ref[...]
    o_ref[...] = (acc[:, :V_HEAD] / acc[:, V_HEAD:]).astype(BF16)


def _attn_prompt(q, k, v, lp):
    t = _tile(lp, 1024, LANES)
    return pl.pallas_call(
        functools.partial(_flash_body, t=t),
        grid=(MLA_HEADS, lp // t),
        in_specs=[
            pl.BlockSpec((None, t, 2 * LANES), lambda h, i: (h, i, 0)),
            pl.BlockSpec((None, lp, 2 * LANES), lambda h, i: (h, 0, 0)),
            pl.BlockSpec((None, lp, V_HEAD + LANES), lambda h, i: (h, 0, 0)),
        ],
        out_specs=pl.BlockSpec((t, V_HEAD), lambda h, i: (i, h)),
        out_shape=jax.ShapeDtypeStruct((lp, MLA_HEADS * V_HEAD), BF16),
        scratch_shapes=[pltpu.VMEM((t, t), F32), pltpu.VMEM((t, t), F32),
                        pltpu.VMEM((t, LANES), F32), pltpu.VMEM((t, V_HEAD + LANES), F32)],
        compiler_params=_params("parallel", "arbitrary"),
        name="attn_prompt",
    )(q, k, v)


def _attn_sample_body(q_ref, cc_ref, cn_ref, pct_ref, pn_ref, wk_ref, wv_ref, o_ref, c_sc, p_sc, *, past, s, lk):
    c_sc[:past] = cc_ref[...].astype(BF16)
    c_sc[past:past + s] = cn_ref[...].astype(BF16)
    if lk > past + s:
        c_sc[past + s:] = jnp.zeros((lk - past - s, c_sc.shape[1]), BF16)
    c = c_sc[...]
    valid = lax.broadcasted_iota(jnp.int32, (s, lk), 1) < past + s
    ones = jnp.ones((8, QK_NOPE), BF16)
    n_ch = 4 if lk % 64 == 0 else 1
    ch = lk // n_ch
    q_pe = jnp.concatenate([q_ref[hd][:, QK_NOPE:] for hd in range(MLA_HEADS)], axis=0)
    pct = pct_ref[...].astype(BF16)
    pn = pn_ref[...]
    kp_new = jnp.concatenate([(pn + pltpu.roll(pn, QK_ROPE, 1)).astype(BF16),
                              jnp.zeros((lk - past - s, LANES), BF16)], axis=0)
    pe = jnp.concatenate([_dot(q_pe, jnp.concatenate([pct, pct], axis=0)), _dot_nt(q_pe, kp_new)], axis=1)
    inv_l = []
    for pair in range(MLA_HEADS // 2):
        wk = wk_ref[:, pair * 2 * QK_NOPE:(pair + 1) * 2 * QK_NOPE]
        kk = jnp.concatenate([_dot(c[r * ch:(r + 1) * ch], wk) for r in range(n_ch)])
        for hd in (2 * pair, 2 * pair + 1):
            kn = kk[:, (hd % 2) * QK_NOPE:(hd % 2 + 1) * QK_NOPE]
            ms = _dot_nt(ones, (kn * kn).astype(BF16))[0:1] / QK_NOPE
            sc = _dot_nt(q_ref[hd][:, :QK_NOPE], kn.astype(BF16)) * lax.rsqrt(ms + EPS) + pe[hd * s:(hd + 1) * s]
            sc = jnp.where(valid, sc, NEG_BIG)
            p = jnp.exp2(sc - jnp.max(sc, axis=-1, keepdims=True))
            inv_l.append(1.0 / jnp.sum(p, axis=-1, keepdims=True))
            p_sc[hd * s:(hd + 1) * s] = p.astype(BF16)
    ctx = _dot(p_sc[...], c)
    for hd in range(MLA_HEADS):
        ctx_h = (ctx[hd * s:(hd + 1) * s] * inv_l[hd]).astype(BF16)
        o_ref[:, hd * V_HEAD:(hd + 1) * V_HEAD] = _dot(ctx_h, wv_ref[:, hd * V_HEAD:(hd + 1) * V_HEAD]).astype(BF16)


def _attn_sample(q, ckv_cache, ckv, kpe_cache, kpe, wk, wv, e, lp, b, s):
    past, kv_lora = ckv_cache.shape[2], ckv_cache.shape[3]
    lk = past + -(-s // LANES) * LANES
    row0 = lp // s
    return pl.pallas_call(
        functools.partial(_attn_sample_body, past=past, s=s, lk=lk),
        grid=(b,),
        in_specs=[
            pl.BlockSpec((MLA_HEADS, s, 2 * LANES), lambda i: (0, row0 + i, 0)),
            pl.BlockSpec((None, None, past, kv_lora), lambda i: (e, i, 0, 0)),
            pl.BlockSpec((s, kv_lora), lambda i: (row0 + i, 0)),
            pl.BlockSpec((None, None, QK_ROPE, past), lambda i: (e, i, 0, 0)),
            pl.BlockSpec((s, LANES), lambda i: (row0 + i, 0)),
            pl.BlockSpec((None,) + wk.shape[1:], lambda i: (e, 0, 0), pipeline_mode=pl.Buffered(1)),
            pl.BlockSpec((None,) + wv.shape[1:], lambda i: (e, 0, 0), pipeline_mode=pl.Buffered(1)),
        ],
        out_specs=pl.BlockSpec((s, MLA_HEADS * V_HEAD), lambda i: (i, 0)),
        out_shape=jax.ShapeDtypeStruct((b * s, MLA_HEADS * V_HEAD), BF16),
        scratch_shapes=[pltpu.VMEM((lk, kv_lora), BF16), pltpu.VMEM((MLA_HEADS * s, lk), BF16)],
        compiler_params=_params("parallel"),
        name="attn_sample",
    )(q, ckv_cache, ckv, kpe_cache, kpe, wk, wv)


def _pool_segment(ext_ref, pooled_ref, row0, n, pos0, gdim):
    pos = (lax.broadcasted_iota(jnp.int32, (n, 1), 0) + pos0).astype(F32)
    for gi, w in enumerate(POOL_WINDOWS):
        cols = slice(gi * gdim, (gi + 1) * gdim)
        u = ext_ref[HIST_ROWS:HIST_ROWS + n, cols]
        acc = u
        for k in range(1, w):
            acc = acc + ext_ref[HIST_ROWS - k:HIST_ROWS - k + n, cols]
        cnt = jnp.minimum(jnp.float32(w), pos + 1.0)
        pooled_ref[row0:row0 + n, cols] = (acc / cnt - u).astype(BF16)


def _pool_project(pooled_ref, pw_ref, sc_ref, o_ref, gdim):
    for gi in range(len(POOL_WINDOWS)):
        cols = slice(gi * gdim, (gi + 1) * gdim)
        o_ref[:, cols] = (_dot(pooled_ref[:, cols], pw_ref[gi]) * sc_ref[:, cols]).astype(BF16)


def _even_out_body(x_ref, u_ref, hist_ref, ap_ref, as_ref, pw_ref, sc_ref, w_ref, o_ref,
                   ext_ref, exts_ref, pooled_ref, pool_ref, *, tm, n_prompt_tiles, s, past, pool_dim):
    i = pl.program_id(0)
    gdim = pool_dim // len(POOL_WINDOWS)

    def project(a_ref):
        _pool_project(pooled_ref, pw_ref, sc_ref, pool_ref, gdim)
        o_ref[...] = x_ref[...] + _dot(pool_ref[...], w_ref[:pool_dim]) + _dot(a_ref[...], w_ref[pool_dim:])

    @pl.when(i == 0)
    def _():
        ext_ref[:HIST_ROWS] = jnp.zeros((HIST_ROWS, pool_dim), F32)

    @pl.when(i < n_prompt_tiles)
    def _():
        ext_ref[HIST_ROWS:] = u_ref[...]
        _pool_segment(ext_ref, pooled_ref, 0, tm, i * tm, gdim)
        ext_ref[:HIST_ROWS] = ext_ref[tm:tm + HIST_ROWS]
        project(ap_ref)

    @pl.when(i >= n_prompt_tiles)
    def _():
        for bi in range(tm // s):
            ext = exts_ref.at[bi]
            ext[0:1] = jnp.zeros((1, pool_dim), F32)
            ext[1:HIST_ROWS] = hist_ref[bi]
            ext[HIST_ROWS:] = u_ref[bi * s:(bi + 1) * s]
            _pool_segment(ext, pooled_ref, bi * s, s, past, gdim)
        project(as_ref)


def _even_out(x, u, state_pool, attn_p, attn_s, pw, scale, wout, e, lp, s, past):
    m, d = x.shape
    pool_dim, attn_dim = u.shape[1], attn_p.shape[1]
    ng = len(POOL_WINDOWS)
    tm = _tile(math.gcd(lp, m - lp), 512, math.lcm(HIST_ROWS, s))
    npt = lp // tm
    return pl.pallas_call(
        functools.partial(_even_out_body, tm=tm, n_prompt_tiles=npt, s=s, past=past, pool_dim=pool_dim),
        grid=(m // tm,),
        in_specs=[
            pl.BlockSpec((tm, d), lambda i: (i, 0)),
            pl.BlockSpec((tm, pool_dim), lambda i: (i, 0)),
            pl.BlockSpec((None, tm // s, POOL_HIST, pool_dim), lambda i: (e, jnp.maximum(i - npt, 0), 0, 0)),
            pl.BlockSpec((tm, attn_dim), lambda i: (jnp.minimum(i, npt - 1), 0)),
            pl.BlockSpec((tm, attn_dim), lambda i: (jnp.maximum(i - npt, 0), 0)),
            pl.BlockSpec((None, ng, pool_dim // ng, pool_dim // ng), lambda i: (e, 0, 0, 0)),
            pl.BlockSpec((None, 1, pool_dim), lambda i: (e, 0, 0)),
            pl.BlockSpec((None,) + wout.shape[1:], lambda i: (e, 0, 0), pipeline_mode=pl.Buffered(1)),
        ],
        out_specs=pl.BlockSpec((tm, d), lambda i: (i, 0)),
        out_shape=jax.ShapeDtypeStruct((m, d), F32),
        scratch_shapes=[pltpu.VMEM((HIST_ROWS + tm, pool_dim), F32),
                        pltpu.VMEM((tm // s, HIST_ROWS + s, pool_dim), F32),
                        pltpu.VMEM((tm, pool_dim), BF16), pltpu.VMEM((tm, pool_dim), BF16)],
        compiler_params=_params("arbitrary"),
        name="even_out",
    )(x, u, state_pool, attn_p, attn_s, pw, scale, wout)


def _odd_in_body(x_ref, g_ref, w_ref, vn_ref, u_ref, v_ref, vf_ref, *, gate):
    h = _rms(x_ref[...], g_ref[...]).astype(BF16)
    v = _rms(jax.nn.gelu(_dot(h, w_ref[:, gate:])), vn_ref[...])
    v_ref[...] = v.astype(BF16)
    vf_ref[...] = v
    n_chunks = 4 if gate % (4 * LANES) == 0 else 1
    cw = gate // n_chunks
    for c in range(n_chunks):
        u_ref[:, c * cw:(c + 1) * cw] = jax.nn.gelu(_dot(h, w_ref[:, c * cw:(c + 1) * cw])).astype(BF16)


def _odd_in(x, g, layer, win, vn, o, lp):
    m, d = x.shape
    gate = win.shape[-1] // 2
    tm = _tile(math.gcd(lp, m - lp), 512)
    first = lp // tm
    return pl.pallas_call(
        functools.partial(_odd_in_body, gate=gate),
        grid=(m // tm,),
        in_specs=[
            pl.BlockSpec((tm, d), lambda i: (i, 0)),
            pl.BlockSpec((None, 1, d), lambda i: (layer, 0, 0)),
            pl.BlockSpec((None, d, 2 * gate), lambda i: (o, 0, 0), pipeline_mode=pl.Buffered(1)),
            pl.BlockSpec((None, 1, gate), lambda i: (o, 0, 0)),
        ],
        out_specs=[
            pl.BlockSpec((tm, gate), lambda i: (i, 0)),
            pl.BlockSpec((tm, gate), lambda i: (i, 0)),
            pl.BlockSpec((tm, gate), lambda i: (jnp.maximum(i - first, 0), 0)),
        ],
        out_shape=[
            jax.ShapeDtypeStruct((m, gate), BF16),
            jax.ShapeDtypeStruct((m, gate), BF16),
            jax.ShapeDtypeStruct((m - lp, gate), F32),
        ],
        compiler_params=_params("arbitrary"),
        name="odd_in",
    )(x, g, win, vn)


def _odd_out_body(x_ref, u_ref, v_ref, ws_ref, b_ref, w_ref, o_ref, us_ref, *, tm, n_prompt_tiles, s):
    is_prompt = pl.program_id(0) < n_prompt_tiles
    ii = lax.broadcasted_iota(jnp.int32, (GMLP_CHUNK, GMLP_CHUNK), 0)
    jj = lax.broadcasted_iota(jnp.int32, (GMLP_CHUNK, GMLP_CHUNK), 1)
    same_stream = jnp.where((ii // s) == (jj // s), 1, 0) + jnp.where(is_prompt, 1, 0)
    keep = (jj <= ii) & (same_stream > 0)
    gdim = u_ref.shape[1] // GMLP_GROUPS
    for gi in range(GMLP_GROUPS):
        cols = slice(gi * gdim, (gi + 1) * gdim)
        wmat = jnp.where(keep, ws_ref[gi], 0.0).astype(BF16)
        bias = b_ref[:, gi:gi + 1]
        for ci in range(tm // GMLP_CHUNK):
            rows = slice(ci * GMLP_CHUNK, (ci + 1) * GMLP_CHUNK)
            sg = _dot(wmat, v_ref[rows, cols]) + bias
            us_ref[rows, cols] = (u_ref[rows, cols].astype(F32) * sg).astype(BF16)
    o_ref[...] = x_ref[...] + _dot(us_ref[...], w_ref[...])


def _odd_out(x, u, v, ws_sel, b_sel, wout, o, lp, s):
    m, d = x.shape
    gate = u.shape[1]
    tm = _tile(math.gcd(lp, m - lp), 512, GMLP_CHUNK)
    npt = lp // tm
    sel = lambda i: jnp.where(i >= npt, 1, 0)
    return pl.pallas_call(
        functools.partial(_odd_out_body, tm=tm, n_prompt_tiles=npt, s=s),
        grid=(m // tm,),
        in_specs=[
            pl.BlockSpec((tm, d), lambda i: (i, 0)),
            pl.BlockSpec((tm, gate), lambda i: (i, 0)),
            pl.BlockSpec((tm, gate), lambda i: (i, 0)),
            pl.BlockSpec((None, None, GMLP_GROUPS, GMLP_CHUNK, GMLP_CHUNK), lambda i: (sel(i), o, 0, 0, 0)),
            pl.BlockSpec((None, None, GMLP_CHUNK, GMLP_GROUPS), lambda i: (sel(i), o, 0, 0)),
            pl.BlockSpec((None,) + wout.shape[1:], lambda i: (o, 0, 0), pipeline_mode=pl.Buffered(1)),
        ],
        out_specs=pl.BlockSpec((tm, d), lambda i: (i, 0)),
        out_shape=jax.ShapeDtypeStruct((m, d), F32),
        scratch_shapes=[pltpu.VMEM((tm, gate), BF16)],
        compiler_params=_params("parallel"),
        name="odd_out",
    )(x, u, v, ws_sel, b_sel, wout)


def kernel(x_prompt, x_sample, cache_mla_ckv, cache_mla_kpe, state_pool, norm_ffn1, norm_mix, norm_ffn2, ffn1_w_gate, ffn1_w_up, ffn1_w_down, ffn2_w_gate, ffn2_w_up, ffn2_w_down, ev_w_in, ev_q_a_norm, ev_kv_a_norm, ev_w_qb, ev_w_kvb, ev_q_nope_norm, ev_q_pe_norm, ev_k_nope_norm, ev_k_pe_norm, ev_pool_w, ev_pool_scale, ev_w_out, od_w_in, od_v_norm, od_w_s, od_b_s, od_w_out):
    bp, lp, d = x_prompt.shape
    b, s, _ = x_sample.shape
    depth = norm_mix.shape[0]
    n_even, n_odd = ev_w_in.shape[0], od_w_in.shape[0]
    past = cache_mla_ckv.shape[2]
    q_lora, kv_lora = ev_q_a_norm.shape[1], ev_kv_a_norm.shape[1]
    pool_dim = ev_pool_scale.shape[1]
    gate = od_v_norm.shape[1]
    m = lp + b * s
    assert bp == 1 and lp % GMLP_CHUNK == 0 and (b * s) % GMLP_CHUNK == 0 and GMLP_CHUNK % s == 0
    assert past % CHUNK == 0 and s <= CHUNK, "every cached and new key must be visible to every sample query"
    assert s >= POOL_HIST and lp >= POOL_HIST and lp % s == 0
    assert ev_w_in.shape[2] == pool_dim + q_lora + kv_lora + QK_ROPE

    vec = lambda a: a[:, None, :]
    bf = lambda a: a.astype(BF16)
    pad_lanes = lambda a: jnp.pad(a, [(0, 0)] * (a.ndim - 1) + [(0, (-a.shape[-1]) % LANES)])

    w_ffn = ((ffn1_w_gate, ffn1_w_up, ffn1_w_down), (ffn2_w_gate, ffn2_w_up, ffn2_w_down))
    g_ffn = (vec(norm_ffn1), vec(norm_ffn2))
    g_mix = vec(norm_mix)
    w_in_e = bf(pad_lanes(ev_w_in))
    wq = ev_w_qb.reshape(n_even, q_lora, MLA_HEADS, QK_HEAD)
    w_qb = bf(jnp.concatenate([wq[..., :QK_NOPE].reshape(n_even, q_lora, -1),
                               wq[..., QK_NOPE:].reshape(n_even, q_lora, -1)], axis=-1))
    w_kvb, w_out_e, pool_w = bf(ev_w_kvb), bf(ev_w_out), bf(ev_pool_w)
    wkv = w_kvb.reshape(n_even, kv_lora, MLA_HEADS, QK_NOPE + V_HEAD)
    w_k = wkv[..., :QK_NOPE].reshape(n_even, kv_lora, -1)
    w_v = wkv[..., QK_NOPE:].reshape(n_even, kv_lora, -1)
    qpn = vec(jnp.tile(ev_q_pe_norm, (1, LANES // QK_ROPE)))
    kpn = vec(pad_lanes(ev_k_pe_norm))
    kpe_cache_t = jnp.swapaxes(cache_mla_kpe, 2, 3)
    w_in_o, w_out_o = bf(od_w_in), bf(od_w_out)
    rep = GMLP_CHUNK // s
    ws_sel = jnp.stack([od_w_s, jnp.tile(od_w_s[:, :, :s, :s], (1, 1, rep, rep))])
    b_sel = jnp.stack([od_b_s, jnp.tile(od_b_s[:, :, :s], (1, 1, rep))]).transpose(0, 1, 3, 2)

    cos, sin = _rope_tables(m, lp, past, s)
    x = jnp.concatenate([x_prompt.reshape(lp, d), x_sample.reshape(b * s, d)], axis=0)
    new_ckv, new_kpe, new_u, new_v = [], [], [], []
    w_cur = tuple(bf(w[0]) for w in w_ffn[0])
    for layer in range(depth):
        x, w_cur = _ffn(x, g_ffn[0], layer, w_cur, w_ffn[1], layer)
        if layer % 2 == 0:
            e = layer // 2
            u, q, ckv, kpe = _even_in(x, g_mix, layer, w_in_e, vec(ev_q_a_norm), vec(ev_kv_a_norm), w_qb,
                                      vec(ev_q_nope_norm), vec(ev_k_nope_norm), qpn, kpn, cos, sin, e,
                                      pool_dim, q_lora, kv_lora)
            k_p, v_p = _kv_prompt(ckv, kpe, w_kvb, e, lp)
            attn_p = _attn_prompt(q, k_p, v_p, lp)
            attn_s = _attn_sample(q, cache_mla_ckv, ckv, kpe_cache_t, kpe, w_k, w_v, e, lp, b, s)
            x = _even_out(x, u, state_pool, attn_p, attn_s, pool_w, vec(ev_pool_scale), w_out_e, e, lp, s, past)
            new_ckv.append((ckv[:lp], ckv[lp:]))
            new_kpe.append((kpe[:lp, :QK_ROPE], kpe[lp:, :QK_ROPE]))
            new_u.append((u[lp - POOL_HIST:lp], u[lp:].reshape(b, s, pool_dim)[:, s - POOL_HIST:]))
        else:
            o = layer // 2
            u, v, v_f32 = _odd_in(x, g_mix, layer, w_in_o, vec(od_v_norm), o, lp)
            x = _odd_out(x, u, v, ws_sel, b_sel, w_out_o, o, lp, s)
            new_v.append(v_f32)
        if layer + 1 < depth:
            x, w_cur = _ffn(x, g_ffn[1], layer, w_cur, w_ffn[0], layer + 1)
        else:
            y_p, (y_s,) = _ffn(x, g_ffn[1], layer, w_cur, split_rows=lp)

    stack = lambda pairs, k, shape: jnp.stack([p[k] for p in pairs]).reshape((n_even,) + shape)
    return (
        y_p.reshape(1, lp, d),
        y_s.reshape(b, s, d),
        stack(new_ckv, 0, (1, lp, kv_lora)),
        stack(new_kpe, 0, (1, lp, QK_ROPE)),
        stack(new_u, 0, (1, POOL_HIST, pool_dim)),
        stack(new_ckv, 1, (b, s, kv_lora)),
        stack(new_kpe, 1, (b, s, QK_ROPE)),
        stack(new_u, 1, (b, POOL_HIST, pool_dim)),
        jnp.stack(new_v).reshape(n_odd, b, s, gate),
    )
```

```python
import functools
import math

import jax
import jax.numpy as jnp
from jax import lax
from jax.experimental import pallas as pl
from jax.experimental.pallas import tpu as pltpu

F32 = jnp.float32
BF16 = jnp.bfloat16

EPS = 1e-6
CHUNK = 64
POOL_WINDOWS = (2, 4, 8, 16)
POOL_HIST = max(POOL_WINDOWS) - 1
HIST_ROWS = 16
MLA_HEADS = 8
QK_NOPE = 128
QK_ROPE = 64
V_HEAD = 128
QK_HEAD = QK_NOPE + QK_ROPE
ATTN_SCALE = QK_HEAD ** -0.5
Q_SCALE = ATTN_SCALE * math.log2(math.e)
ROPE_THETA = 10000.0
GMLP_CHUNK = 128
GMLP_GROUPS = 8
LANES = 128
NEG_BIG = -1e30
VMEM_LIMIT = 60 * 1024 * 1024


def _params(*sem):
    return pltpu.CompilerParams(dimension_semantics=sem, vmem_limit_bytes=VMEM_LIMIT)


def _tile(n, pref, mult=8):
    if n <= pref:
        return n
    for t in range(pref - pref % mult, 0, -mult):
        if n % t == 0:
            return t
    raise ValueError(f"no tile for {n}")


def _const_spec(shape):
    nd = len(shape)
    return pl.BlockSpec(shape, lambda *_: (0,) * nd, pipeline_mode=pl.Buffered(1))


def _rms(x, g):
    ms = jnp.mean(x * x, axis=-1, keepdims=True)
    return x * lax.rsqrt(ms + EPS) * g


def _row_specs(x, tm):
    if not isinstance(x, tuple):
        return (x,), [pl.BlockSpec((tm, x.shape[1]), lambda i: (i, 0))], None
    npt = x[0].shape[0] // tm
    d = x[0].shape[1]
    return x, [pl.BlockSpec((tm, d), lambda i: (jnp.minimum(i, npt - 1), 0)),
               pl.BlockSpec((tm, d), lambda i: (jnp.maximum(i - npt, 0), 0))], npt


def _read_rows(x_refs, n_prompt_tiles):
    if len(x_refs) == 1:
        return x_refs[0][...]
    return jnp.where(pl.program_id(0) < n_prompt_tiles, x_refs[0][...], x_refs[1][...])


def _dot(a, b):
    return jnp.dot(a, b, preferred_element_type=F32)


def _dot_nt(a, b):
    return lax.dot_general(a, b, (((1,), (1,)), ((), ())), preferred_element_type=F32)


def _ffn_body(x_ref, g_ref, wg_ref, wu_ref, wd_ref, *rest, convert_next, n_first):
    *rest, h_ref = rest
    if convert_next:
        (*nxt, o_ref, cg_ref, cu_ref, cd_ref) = rest

        @pl.when(pl.program_id(0) < convert_next)
        def _():
            for src, dst in zip(nxt, (cg_ref, cu_ref, cd_ref)):
                dst[...] = src[...].astype(BF16)

        outs = (o_ref,)
    else:
        outs = tuple(rest)

    def run(o_ref):
        @pl.when(pl.program_id(1) == 0)
        def _():
            x = x_ref[...]
            h_ref[...] = _rms(x, g_ref[...]).astype(BF16)
            o_ref[...] = x

        h = h_ref[...]
        a = _dot(h, wg_ref[...])
        b = _dot(h, wu_ref[...])
        act = (a * jax.nn.sigmoid(a) * b * 0.5).astype(BF16)
        o_ref[...] += _dot(act, wd_ref[...])

    if len(outs) == 1:
        run(outs[0])
    else:
        first = pl.program_id(0) < n_first
        pl.when(first)(lambda: run(outs[0]))
        pl.when(jnp.logical_not(first))(lambda: run(outs[1]))


def _ffn(x, g, layer, w, w_next=None, next_layer=None, split_rows=None):
    m, d = x.shape
    f = w[0].shape[-1]
    tm = _tile(m if split_rows is None else math.gcd(split_rows, m - split_rows), 1024)
    tf = _tile(f, 512 if split_rows is None else 256, LANES)
    n_i = m // tm
    in_specs = [
        pl.BlockSpec((tm, d), lambda i, j: (i, 0)),
        pl.BlockSpec((None, 1, d), lambda i, j: (layer, 0, 0)),
        pl.BlockSpec((d, tf), lambda i, j: (0, j)),
        pl.BlockSpec((d, tf), lambda i, j: (0, j)),
        pl.BlockSpec((tf, d), lambda i, j: (j, 0)),
    ]
    n_first = None
    if split_rows is None:
        out_specs = [pl.BlockSpec((tm, d), lambda i, j: (i, 0))]
        out_shape = [jax.ShapeDtypeStruct((m, d), F32)]
    else:
        assert w_next is None
        n_first = split_rows // tm
        out_specs = [pl.BlockSpec((tm, d), lambda i, j: (jnp.minimum(i, n_first - 1), 0)),
                     pl.BlockSpec((tm, d), lambda i, j: (jnp.maximum(i - n_first, 0), 0),
                                  pipeline_mode=pl.Buffered(1))]
        out_shape = [jax.ShapeDtypeStruct((split_rows, d), F32), jax.ShapeDtypeStruct((m - split_rows, d), F32)]
    nr = 0
    if w_next is not None:
        nr = max(r for r in (1, 2, 4, 8, 16) if r <= n_i and d % (r * LANES) == 0)
        db = d // nr
        n_j = f // tf
        blk = lambda i: jnp.minimum(i, nr - 1)
        col = lambda i, j: jnp.where(i < nr, j, n_j - 1)
        in_specs += [
            pl.BlockSpec((None, db, tf), lambda i, j: (next_layer, blk(i), col(i, j))),
            pl.BlockSpec((None, db, tf), lambda i, j: (next_layer, blk(i), col(i, j))),
            pl.BlockSpec((None, tf, db), lambda i, j: (next_layer, col(i, j), blk(i))),
        ]
        out_specs += [
            pl.BlockSpec((db, tf), lambda i, j: (blk(i), col(i, j))),
            pl.BlockSpec((db, tf), lambda i, j: (blk(i), col(i, j))),
            pl.BlockSpec((tf, db), lambda i, j: (col(i, j), blk(i))),
        ]
        out_shape += [jax.ShapeDtypeStruct((d, f), BF16), jax.ShapeDtypeStruct((d, f), BF16),
                      jax.ShapeDtypeStruct((f, d), BF16)]
    out = pl.pallas_call(
        functools.partial(_ffn_body, convert_next=nr, n_first=n_first),
        grid=(n_i, f // tf),
        in_specs=in_specs,
        out_specs=out_specs,
        out_shape=out_shape,
        scratch_shapes=[pltpu.VMEM((tm, d), BF16)],
        compiler_params=_params("arbitrary", "arbitrary"),
        name="ffn",
    )(x, g, *w, *(w_next or ()))
    return out[0], tuple(out[1:])


def _rope_tab_body(inv_ref, sgn_ref, cos_ref, sin_ref, *, tm, lp, past, s):
    r = lax.broadcasted_iota(jnp.int32, (tm, LANES), 0) + pl.program_id(0) * tm
    t = r - lp
    t = (t & (s - 1)) if s & (s - 1) == 0 else lax.rem(t, s)
    pos = jnp.where(r >= lp, past + t, r).astype(F32)
    ang = pos * inv_ref[...]
    cos_ref[...] = jnp.cos(ang)
    sin_ref[...] = jnp.sin(ang) * sgn_ref[...]


def _rope_tables(m, lp, past, s):
    half = QK_ROPE // 2
    inv = ROPE_THETA ** (-jnp.arange(0, QK_ROPE, 2, dtype=F32) / QK_ROPE)
    inv = jnp.tile(inv, LANES // half)[None, :]
    sgn = jnp.tile(jnp.concatenate([-jnp.ones((half,), F32), jnp.ones((half,), F32)]), LANES // QK_ROPE)[None, :]
    tm = _tile(m, 512)
    return pl.pallas_call(
        functools.partial(_rope_tab_body, tm=tm, lp=lp, past=past, s=s),
        grid=(m // tm,),
        in_specs=[_const_spec((1, LANES)), _const_spec((1, LANES))],
        out_specs=[pl.BlockSpec((tm, LANES), lambda i: (i, 0))] * 2,
        out_shape=[jax.ShapeDtypeStruct((m, LANES), F32)] * 2,
        compiler_params=_params("parallel"),
        name="rope_tables",
    )(inv, sgn)


def _swap_halves(x):
    lane = lax.broadcasted_iota(jnp.int32, x.shape, 1)
    left = pltpu.roll(x, LANES - QK_ROPE // 2, 1)
    right = pltpu.roll(x, QK_ROPE // 2, 1)
    return jnp.where((lane & (QK_ROPE - 1)) < QK_ROPE // 2, left, right)


def _even_in_body(*refs, n_x, n_prompt_tiles, pool_dim, q_lora, kv_lora):
    x_refs, refs = refs[:n_x], refs[n_x:]
    (g_ref, win_ref, qan_ref, kvan_ref, wqb_ref, qnn_ref, knn_ref, qpn_ref, kpn_ref, cos_ref, sin_ref,
     u_ref, q_ref, ckv_ref, kpe_ref) = refs
    h = _rms(_read_rows(x_refs, n_prompt_tiles), g_ref[...]).astype(BF16)
    o1, o2 = pool_dim + q_lora, pool_dim + q_lora + kv_lora
    cos, sin = cos_ref[...], sin_ref[...]
    lane = lax.broadcasted_iota(jnp.int32, cos.shape, 1)
    lo = lane < QK_ROPE

    def rope(t):
        return t * cos + _swap_halves(t) * sin

    qn = _rms(_dot(h, win_ref[:, pool_dim:o1]), qan_ref[...]).astype(BF16)
    q = _dot(qn, wqb_ref[...])
    zk = _dot(h, win_ref[:, o1:])
    ckv_ref[...] = _rms(zk[:, :kv_lora], kvan_ref[...])
    kp = zk[:, kv_lora:]
    kp = kp * lax.rsqrt(jnp.sum(kp * kp, axis=-1, keepdims=True) / QK_ROPE + EPS) * kpn_ref[...]
    kpe_ref[...] = rope(kp)
    u_ref[...] = _dot(h, win_ref[:, :pool_dim])

    nope_w = MLA_HEADS * QK_NOPE
    k_gain = knn_ref[...] * Q_SCALE
    for hd in range(MLA_HEADS):
        qh = _rms(q[:, hd * QK_NOPE:(hd + 1) * QK_NOPE], qnn_ref[...]) * k_gain
        q_ref[hd, :, :QK_NOPE] = qh.astype(BF16)
    for j in range(MLA_HEADS // 2):
        t = q[:, nope_w + j * LANES:nope_w + (j + 1) * LANES]
        tt = t * t
        s_lo = jnp.sum(jnp.where(lo, tt, 0.0), axis=-1, keepdims=True)
        s_hi = jnp.sum(jnp.where(lo, 0.0, tt), axis=-1, keepdims=True)
        inv = jnp.where(lo, lax.rsqrt(s_lo / QK_ROPE + EPS), lax.rsqrt(s_hi / QK_ROPE + EPS))
        t = rope(t * inv * qpn_ref[...]) * Q_SCALE
        q_ref[2 * j, :, QK_NOPE:] = jnp.where(lo, t, 0.0).astype(BF16)
        q_ref[2 * j + 1, :, QK_NOPE:] = jnp.where(lo, 0.0, t).astype(BF16)


def _even_in(x, g, layer, win, qan, kvan, wqb, qnn, knn, qpn, kpn, cos, sin, e, pool_dim, q_lora, kv_lora):
    rows = tuple(a.shape[0] for a in x) if isinstance(x, tuple) else (x.shape[0],)
    m, d = sum(rows), (x[0] if isinstance(x, tuple) else x).shape[1]
    tm = _tile(math.gcd(*rows), 512)
    xs, x_specs, npt = _row_specs(x, tm)
    n_in = win.shape[-1]
    n_q = wqb.shape[-1]
    row = lambda w: pl.BlockSpec((tm, w), lambda i: (i, 0))
    vec = lambda w, idx: pl.BlockSpec((None, 1, w), lambda i: (idx, 0, 0))
    return pl.pallas_call(
        functools.partial(_even_in_body, n_x=len(xs), n_prompt_tiles=npt, pool_dim=pool_dim, q_lora=q_lora,
                          kv_lora=kv_lora),
        grid=(m // tm,),
        in_specs=x_specs + [
            vec(d, layer),
            pl.BlockSpec((None, d, n_in), lambda i: (e, 0, 0), pipeline_mode=pl.Buffered(1)),
            vec(q_lora, e), vec(kv_lora, e),
            pl.BlockSpec((None, q_lora, n_q), lambda i: (e, 0, 0), pipeline_mode=pl.Buffered(1)),
            vec(QK_NOPE, e), vec(QK_NOPE, e), vec(LANES, e), vec(LANES, e),
            row(LANES), row(LANES),
        ],
        out_specs=[
            row(pool_dim),
            pl.BlockSpec((MLA_HEADS, tm, 2 * LANES), lambda i: (0, i, 0)),
            row(kv_lora), row(LANES),
        ],
        out_shape=[
            jax.ShapeDtypeStruct((m, pool_dim), F32),
            jax.ShapeDtypeStruct((MLA_HEADS, m, 2 * LANES), BF16),
            jax.ShapeDtypeStruct((m, kv_lora), F32),
            jax.ShapeDtypeStruct((m, LANES), F32),
        ],
        compiler_params=_params("arbitrary"),
        name="even_in",
    )(*xs, g, win, qan, kvan, wqb, qnn, knn, qpn, kpn, cos, sin)


def _kv_body(ckv_ref, kpe_ref, w_ref, k_ref, v_ref):
    c = ckv_ref[...].astype(BF16)
    kp = kpe_ref[...]
    kp_both = (kp + pltpu.roll(kp, QK_ROPE, 1)).astype(BF16)
    hw = QK_NOPE + V_HEAD
    for hd in range(MLA_HEADS):
        kv = _dot(c, w_ref[:, hd * hw:(hd + 1) * hw])
        k_ref[hd, :, :QK_NOPE] = _rms(kv[:, :QK_NOPE], 1.0).astype(BF16)
        k_ref[hd, :, QK_NOPE:] = kp_both
        v_ref[hd, :, :V_HEAD] = kv[:, QK_NOPE:].astype(BF16)
        v_ref[hd, :, V_HEAD:] = jnp.ones((c.shape[0], LANES), BF16)


def _kv_prompt(ckv, kpe, wkvb, e, lp):
    kv_lora = ckv.shape[-1]
    tm = _tile(lp, 512)
    return pl.pallas_call(
        _kv_body,
        grid=(lp // tm,),
        in_specs=[
            pl.BlockSpec((tm, kv_lora), lambda i: (i, 0)),
            pl.BlockSpec((tm, LANES), lambda i: (i, 0)),
            pl.BlockSpec((None,) + wkvb.shape[1:], lambda i: (e, 0, 0), pipeline_mode=pl.Buffered(1)),
        ],
        out_specs=[
            pl.BlockSpec((MLA_HEADS, tm, 2 * LANES), lambda i: (0, i, 0)),
            pl.BlockSpec((MLA_HEADS, tm, V_HEAD + LANES), lambda i: (0, i, 0)),
        ],
        out_shape=[
            jax.ShapeDtypeStruct((MLA_HEADS, lp, 2 * LANES), BF16),
            jax.ShapeDtypeStruct((MLA_HEADS, lp, V_HEAD + LANES), BF16),
        ],
        compiler_params=_params("parallel"),
        name="kv_prompt",
    )(ckv, kpe, wkvb)


def _flash_body(q_ref, k_ref, v_ref, o_ref, sa_ref, sb_ref, m_ref, acc_ref, *, t):
    i = pl.program_id(1)
    q = q_ref[...]
    m_ref[...] = jnp.full(m_ref.shape, NEG_BIG, F32)
    acc_ref[...] = jnp.zeros(acc_ref.shape, F32)

    def scores(j):
        return _dot_nt(q, k_ref[pl.ds(pl.multiple_of(j * t, t), t), :])

    def update(s_ref, j, diagonal=False):
        s = s_ref[...]
        if diagonal:
            qc = lax.broadcasted_iota(jnp.int32, s.shape, 0) // CHUNK
            kc = lax.broadcasted_iota(jnp.int32, s.shape, 1) // CHUNK
            s = jnp.where(kc <= qc, s, NEG_BIG)
        m_old = m_ref[...]
        m_new = jnp.maximum(m_old, jnp.max(s, axis=-1, keepdims=True))
        p = jnp.exp2(s - jnp.tile(m_new, (1, t // LANES)))
        alpha = jnp.tile(jnp.exp2(m_old - m_new), (1, acc_ref.shape[1] // LANES))
        pv = _dot(p.astype(BF16), v_ref[pl.ds(pl.multiple_of(j * t, t), t), :])
        acc_ref[...] = alpha * acc_ref[...] + pv
        m_ref[...] = m_new

    sa_ref[...] = scores(0)

    def pair(jj, carry):
        j = 2 * jj
        sb_ref[...] = scores(j + 1)
        update(sa_ref, j)
        sa_ref[...] = scores(j + 2)
        update(sb_ref, j + 1)
        return carry

    lax.fori_loop(0, i // 2, pair, 0)

    @pl.when(i % 2 == 0)
    def _():
        update(sa_ref, i, diagonal=True)

    @pl.when(i % 2 == 1)
    def _():
        sb_ref[...] = scores(i)
        update(sa_ref, i - 1)
        update(sb_ref, i, diagonal=True)

    acc = acc_ref[...]
    o_ref[...] = (acc[:, :V_HEAD] / acc[:, V_HEAD:]).astype(BF16)


def _attn_prompt(q, k, v, lp):
    t = _tile(lp, 1024, LANES)
    return pl.pallas_call(
        functools.partial(_flash_body, t=t),
        grid=(MLA_HEADS, lp // t),
        in_specs=[
            pl.BlockSpec((None, t, 2 * LANES), lambda h, i: (h, i, 0)),
            pl.BlockSpec((None, lp, 2 * LANES), lambda h, i: (h, 0, 0)),
            pl.BlockSpec((None, lp, V_HEAD + LANES), lambda h, i: (h, 0, 0)),
        ],
        out_specs=pl.BlockSpec((t, V_HEAD), lambda h, i: (i, h)),
        out_shape=jax.ShapeDtypeStruct((lp, MLA_HEADS * V_HEAD), BF16),
        scratch_shapes=[pltpu.VMEM((t, t), F32), pltpu.VMEM((t, t), F32),
                        pltpu.VMEM((t, LANES), F32), pltpu.VMEM((t, V_HEAD + LANES), F32)],
        compiler_params=_params("parallel", "arbitrary"),
        name="attn_prompt",
    )(q, k, v)


def _attn_sample_body(q_ref, cc_ref, cn_ref, pct_ref, pn_ref, wk_ref, wv_ref, o_ref, c_sc, p_sc, *, past, s, lk):
    c_sc[:past] = cc_ref[...].astype(BF16)
    c_sc[past:past + s] = cn_ref[...].astype(BF16)
    if lk > past + s:
        c_sc[past + s:] = jnp.zeros((lk - past - s, c_sc.shape[1]), BF16)
    c = c_sc[...]
    valid = lax.broadcasted_iota(jnp.int32, (s, lk), 1) < past + s
    ones = jnp.ones((8, QK_NOPE), BF16)
    n_ch = 4 if lk % 64 == 0 else 1
    ch = lk // n_ch
    q_pe = jnp.concatenate([q_ref[hd][:, QK_NOPE:] for hd in range(MLA_HEADS)], axis=0)
    pct = pct_ref[...].astype(BF16)
    pn = pn_ref[...]
    kp_new = jnp.concatenate([(pn + pltpu.roll(pn, QK_ROPE, 1)).astype(BF16),
                              jnp.zeros((lk - past - s, LANES), BF16)], axis=0)
    pe = jnp.concatenate([_dot(q_pe, jnp.concatenate([pct, pct], axis=0)), _dot_nt(q_pe, kp_new)], axis=1)
    inv_l = []
    for pair in range(MLA_HEADS // 2):
        wk = wk_ref[:, pair * 2 * QK_NOPE:(pair + 1) * 2 * QK_NOPE]
        kk = jnp.concatenate([_dot(c[r * ch:(r + 1) * ch], wk) for r in range(n_ch)])
        for hd in (2 * pair, 2 * pair + 1):
            kn = kk[:, (hd % 2) * QK_NOPE:(hd % 2 + 1) * QK_NOPE]
            ms = _dot_nt(ones, (kn * kn).astype(BF16))[0:1] / QK_NOPE
            sc = _dot_nt(q_ref[hd][:, :QK_NOPE], kn.astype(BF16)) * lax.rsqrt(ms + EPS) + pe[hd * s:(hd + 1) * s]
            sc = jnp.where(valid, sc, NEG_BIG)
            p = jnp.exp2(sc - jnp.max(sc, axis=-1, keepdims=True))
            inv_l.append(1.0 / jnp.sum(p, axis=-1, keepdims=True))
            p_sc[hd * s:(hd + 1) * s] = p.astype(BF16)
    ctx = _dot(p_sc[...], c)
    for hd in range(MLA_HEADS):
        ctx_h = (ctx[hd * s:(hd + 1) * s] * inv_l[hd]).astype(BF16)
        o_ref[:, hd * V_HEAD:(hd + 1) * V_HEAD] = _dot(ctx_h, wv_ref[:, hd * V_HEAD:(hd + 1) * V_HEAD]).astype(BF16)


def _attn_sample(q, ckv_cache, ckv, kpe_cache, kpe, wk, wv, e, lp, b, s):
    past, kv_lora = ckv_cache.shape[2], ckv_cache.shape[3]
    lk = past + -(-s // LANES) * LANES
    row0 = lp // s
    return pl.pallas_call(
        functools.partial(_attn_sample_body, past=past, s=s, lk=lk),
        grid=(b,),
        in_specs=[
            pl.BlockSpec((MLA_HEADS, s, 2 * LANES), lambda i: (0, row0 + i, 0)),
            pl.BlockSpec((None, None, past, kv_lora), lambda i: (e, i, 0, 0)),
            pl.BlockSpec((s, kv_lora), lambda i: (row0 + i, 0)),
            pl.BlockSpec((None, None, QK_ROPE, past), lambda i: (e, i, 0, 0)),
            pl.BlockSpec((s, LANES), lambda i: (row0 + i, 0)),
            pl.BlockSpec((None,) + wk.shape[1:], lambda i: (e, 0, 0), pipeline_mode=pl.Buffered(1)),
            pl.BlockSpec((None,) + wv.shape[1:], lambda i: (e, 0, 0), pipeline_mode=pl.Buffered(1)),
        ],
        out_specs=pl.BlockSpec((s, MLA_HEADS * V_HEAD), lambda i: (i, 0)),
        out_shape=jax.ShapeDtypeStruct((b * s, MLA_HEADS * V_HEAD), BF16),
        scratch_shapes=[pltpu.VMEM((lk, kv_lora), BF16), pltpu.VMEM((MLA_HEADS * s, lk), BF16)],
        compiler_params=_params("parallel"),
        name="attn_sample",
    )(q, ckv_cache, ckv, kpe_cache, kpe, wk, wv)


def _pool_segment(ext_ref, pooled_ref, row0, n, pos0, gdim):
    pos = (lax.broadcasted_iota(jnp.int32, (n, 1), 0) + pos0).astype(F32)
    for gi, w in enumerate(POOL_WINDOWS):
        cols = slice(gi * gdim, (gi + 1) * gdim)
        u = ext_ref[HIST_ROWS:HIST_ROWS + n, cols]
        acc = u
        for k in range(1, w):
            acc = acc + ext_ref[HIST_ROWS - k:HIST_ROWS - k + n, cols]
        cnt = jnp.minimum(jnp.float32(w), pos + 1.0)
        pooled_ref[row0:row0 + n, cols] = (acc / cnt - u).astype(BF16)


def _pool_project(pooled_ref, pw_ref, sc_ref, o_ref, gdim):
    for gi in range(len(POOL_WINDOWS)):
        cols = slice(gi * gdim, (gi + 1) * gdim)
        o_ref[:, cols] = (_dot(pooled_ref[:, cols], pw_ref[gi]) * sc_ref[:, cols]).astype(BF16)


def _even_out_body(*refs, n_x, tm, n_prompt_tiles, s, past, pool_dim):
    x_refs, refs = refs[:n_x], refs[n_x:]
    (u_ref, hist_ref, ap_ref, as_ref, pw_ref, sc_ref, w_ref, o_ref, ext_ref, exts_ref, pooled_ref, pool_ref) = refs
    i = pl.program_id(0)
    gdim = pool_dim // len(POOL_WINDOWS)

    def project(a_ref, x_ref):
        _pool_project(pooled_ref, pw_ref, sc_ref, pool_ref, gdim)
        o_ref[...] = x_ref[...] + _dot(pool_ref[...], w_ref[:pool_dim]) + _dot(a_ref[...], w_ref[pool_dim:])

    @pl.when(i == 0)
    def _():
        ext_ref[:HIST_ROWS] = jnp.zeros((HIST_ROWS, pool_dim), F32)

    @pl.when(i < n_prompt_tiles)
    def _():
        ext_ref[HIST_ROWS:] = u_ref[...]
        _pool_segment(ext_ref, pooled_ref, 0, tm, i * tm, gdim)
        ext_ref[:HIST_ROWS] = ext_ref[tm:tm + HIST_ROWS]
        project(ap_ref, x_refs[0])

    @pl.when(i >= n_prompt_tiles)
    def _():
        for bi in range(tm // s):
            ext = exts_ref.at[bi]
            ext[0:1] = jnp.zeros((1, pool_dim), F32)
            ext[1:HIST_ROWS] = hist_ref[bi]
            ext[HIST_ROWS:] = u_ref[bi * s:(bi + 1) * s]
            _pool_segment(ext, pooled_ref, bi * s, s, past, gdim)
        project(as_ref, x_refs[-1])


def _even_out(x, u, state_pool, attn_p, attn_s, pw, scale, wout, e, lp, s, past):
    m, pool_dim = u.shape
    attn_dim = attn_p.shape[1]
    ng = len(POOL_WINDOWS)
    tm = _tile(math.gcd(lp, m - lp), 512, math.lcm(HIST_ROWS, s))
    npt = lp // tm
    xs, x_specs, _ = _row_specs(x, tm)
    d = xs[0].shape[1]
    return pl.pallas_call(
        functools.partial(_even_out_body, n_x=len(xs), tm=tm, n_prompt_tiles=npt, s=s, past=past,
                          pool_dim=pool_dim),
        grid=(m // tm,),
        in_specs=x_specs + [
            pl.BlockSpec((tm, pool_dim), lambda i: (i, 0)),
            pl.BlockSpec((None, tm // s, POOL_HIST, pool_dim), lambda i: (e, jnp.maximum(i - npt, 0), 0, 0)),
            pl.BlockSpec((tm, attn_dim), lambda i: (jnp.minimum(i, npt - 1), 0)),
            pl.BlockSpec((tm, attn_dim), lambda i: (jnp.maximum(i - npt, 0), 0)),
            pl.BlockSpec((None, ng, pool_dim // ng, pool_dim // ng), lambda i: (e, 0, 0, 0)),
            pl.BlockSpec((None, 1, pool_dim), lambda i: (e, 0, 0)),
            pl.BlockSpec((None,) + wout.shape[1:], lambda i: (e, 0, 0), pipeline_mode=pl.Buffered(1)),
        ],
        out_specs=pl.BlockSpec((tm, d), lambda i: (i, 0)),
        out_shape=jax.ShapeDtypeStruct((m, d), F32),
        scratch_shapes=[pltpu.VMEM((HIST_ROWS + tm, pool_dim), F32),
                        pltpu.VMEM((tm // s, HIST_ROWS + s, pool_dim), F32),
                        pltpu.VMEM((tm, pool_dim), BF16), pltpu.VMEM((tm, pool_dim), BF16)],
        compiler_params=_params("arbitrary"),
        name="even_out",
    )(*xs, u, state_pool, attn_p, attn_s, pw, scale, wout)


def _odd_in_body(x_ref, g_ref, w_ref, vn_ref, u_ref, v_ref, vf_ref, *, gate):
    h = _rms(x_ref[...], g_ref[...]).astype(BF16)
    v = _rms(jax.nn.gelu(_dot(h, w_ref[:, gate:])), vn_ref[...])
    v_ref[...] = v.astype(BF16)
    vf_ref[...] = v
    n_chunks = 4 if gate % (4 * LANES) == 0 else 1
    cw = gate // n_chunks
    for c in range(n_chunks):
        u_ref[:, c * cw:(c + 1) * cw] = jax.nn.gelu(_dot(h, w_ref[:, c * cw:(c + 1) * cw])).astype(BF16)


def _odd_in(x, g, layer, win, vn, o, lp):
    m, d = x.shape
    gate = win.shape[-1] // 2
    tm = _tile(math.gcd(lp, m - lp), 512)
    first = lp // tm
    return pl.pallas_call(
        functools.partial(_odd_in_body, gate=gate),
        grid=(m // tm,),
        in_specs=[
            pl.BlockSpec((tm, d), lambda i: (i, 0)),
            pl.BlockSpec((None, 1, d), lambda i: (layer, 0, 0)),
            pl.BlockSpec((None, d, 2 * gate), lambda i: (o, 0, 0), pipeline_mode=pl.Buffered(1)),
            pl.BlockSpec((None, 1, gate), lambda i: (o, 0, 0)),
        ],
        out_specs=[
            pl.BlockSpec((tm, gate), lambda i: (i, 0)),
            pl.BlockSpec((tm, gate), lambda i: (i, 0)),
            pl.BlockSpec((tm, gate), lambda i: (jnp.maximum(i - first, 0), 0)),
        ],
        out_shape=[
            jax.ShapeDtypeStruct((m, gate), BF16),
            jax.ShapeDtypeStruct((m, gate), BF16),
            jax.ShapeDtypeStruct((m - lp, gate), F32),
        ],
        compiler_params=_params("arbitrary"),
        name="odd_in",
    )(x, g, win, vn)


def _odd_out_body(x_ref, u_ref, v_ref, ws_ref, b_ref, w_ref, o_ref, us_ref, *, tm, n_prompt_tiles, s):
    is_prompt = pl.program_id(0) < n_prompt_tiles
    ii = lax.broadcasted_iota(jnp.int32, (GMLP_CHUNK, GMLP_CHUNK), 0)
    jj = lax.broadcasted_iota(jnp.int32, (GMLP_CHUNK, GMLP_CHUNK), 1)
    same_stream = jnp.where((ii // s) == (jj // s), 1, 0) + jnp.where(is_prompt, 1, 0)
    keep = (jj <= ii) & (same_stream > 0)
    gdim = u_ref.shape[1] // GMLP_GROUPS
    for gi in range(GMLP_GROUPS):
        cols = slice(gi * gdim, (gi + 1) * gdim)
        wmat = jnp.where(keep, ws_ref[gi], 0.0).astype(BF16)
        bias = b_ref[:, gi:gi + 1]
        for ci in range(tm // GMLP_CHUNK):
            rows = slice(ci * GMLP_CHUNK, (ci + 1) * GMLP_CHUNK)
            sg = _dot(wmat, v_ref[rows, cols]) + bias
            us_ref[rows, cols] = (u_ref[rows, cols].astype(F32) * sg).astype(BF16)
    o_ref[...] = x_ref[...] + _dot(us_ref[...], w_ref[...])


def _odd_out(x, u, v, ws_sel, b_sel, wout, o, lp, s):
    m, d = x.shape
    gate = u.shape[1]
    tm = _tile(math.gcd(lp, m - lp), 512, GMLP_CHUNK)
    npt = lp // tm
    sel = lambda i: jnp.where(i >= npt, 1, 0)
    return pl.pallas_call(
        functools.partial(_odd_out_body, tm=tm, n_prompt_tiles=npt, s=s),
        grid=(m // tm,),
        in_specs=[
            pl.BlockSpec((tm, d), lambda i: (i, 0)),
            pl.BlockSpec((tm, gate), lambda i: (i, 0)),
            pl.BlockSpec((tm, gate), lambda i: (i, 0)),
            pl.BlockSpec((None, None, GMLP_GROUPS, GMLP_CHUNK, GMLP_CHUNK), lambda i: (sel(i), o, 0, 0, 0)),
            pl.BlockSpec((None, None, GMLP_CHUNK, GMLP_GROUPS), lambda i: (sel(i), o, 0, 0)),
            pl.BlockSpec((None,) + wout.shape[1:], lambda i: (o, 0, 0), pipeline_mode=pl.Buffered(1)),
        ],
        out_specs=pl.BlockSpec((tm, d), lambda i: (i, 0)),
        out_shape=jax.ShapeDtypeStruct((m, d), F32),
        scratch_shapes=[pltpu.VMEM((tm, gate), BF16)],
        compiler_params=_params("parallel"),
        name="odd_out",
    )(x, u, v, ws_sel, b_sel, wout)


def kernel(x_prompt, x_sample, cache_mla_ckv, cache_mla_kpe, state_pool, norm_ffn1, norm_mix, norm_ffn2, ffn1_w_gate, ffn1_w_up, ffn1_w_down, ffn2_w_gate, ffn2_w_up, ffn2_w_down, ev_w_in, ev_q_a_norm, ev_kv_a_norm, ev_w_qb, ev_w_kvb, ev_q_nope_norm, ev_q_pe_norm, ev_k_nope_norm, ev_k_pe_norm, ev_pool_w, ev_pool_scale, ev_w_out, od_w_in, od_v_norm, od_w_s, od_b_s, od_w_out):
    bp, lp, d = x_prompt.shape
    b, s, _ = x_sample.shape
    depth = norm_mix.shape[0]
    n_even, n_odd = ev_w_in.shape[0], od_w_in.shape[0]
    past = cache_mla_ckv.shape[2]
    q_lora, kv_lora = ev_q_a_norm.shape[1], ev_kv_a_norm.shape[1]
    pool_dim = ev_pool_scale.shape[1]
    gate = od_v_norm.shape[1]
    m = lp + b * s
    assert bp == 1 and lp % GMLP_CHUNK == 0 and (b * s) % GMLP_CHUNK == 0 and GMLP_CHUNK % s == 0
    assert past % CHUNK == 0 and s <= CHUNK, "every cached and new key must be visible to every sample query"
    assert s >= POOL_HIST and lp >= POOL_HIST and lp % s == 0
    assert ev_w_in.shape[2] == pool_dim + q_lora + kv_lora + QK_ROPE

    vec = lambda a: a[:, None, :]
    bf = lambda a: a.astype(BF16)
    pad_lanes = lambda a: jnp.pad(a, [(0, 0)] * (a.ndim - 1) + [(0, (-a.shape[-1]) % LANES)])

    w_ffn = ((ffn1_w_gate, ffn1_w_up, ffn1_w_down), (ffn2_w_gate, ffn2_w_up, ffn2_w_down))
    g_ffn = (vec(norm_ffn1), vec(norm_ffn2))
    g_mix = vec(norm_mix)
    w_in_e = bf(pad_lanes(ev_w_in))
    wq = ev_w_qb.reshape(n_even, q_lora, MLA_HEADS, QK_HEAD)
    w_qb = bf(jnp.concatenate([wq[..., :QK_NOPE].reshape(n_even, q_lora, -1),
                               wq[..., QK_NOPE:].reshape(n_even, q_lora, -1)], axis=-1))
    w_kvb, w_out_e, pool_w = bf(ev_w_kvb), bf(ev_w_out), bf(ev_pool_w)
    wkv = w_kvb.reshape(n_even, kv_lora, MLA_HEADS, QK_NOPE + V_HEAD)
    w_k = wkv[..., :QK_NOPE].reshape(n_even, kv_lora, -1)
    w_v = wkv[..., QK_NOPE:].reshape(n_even, kv_lora, -1)
    qpn = vec(jnp.tile(ev_q_pe_norm, (1, LANES // QK_ROPE)))
    kpn = vec(pad_lanes(ev_k_pe_norm))
    kpe_cache_t = jnp.swapaxes(cache_mla_kpe, 2, 3)
    w_in_o, w_out_o = bf(od_w_in), bf(od_w_out)
    rep = GMLP_CHUNK // s
    ws_sel = jnp.stack([od_w_s, jnp.tile(od_w_s[:, :, :s, :s], (1, 1, rep, rep))])
    b_sel = jnp.stack([od_b_s, jnp.tile(od_b_s[:, :, :s], (1, 1, rep))]).transpose(0, 1, 3, 2)

    cos, sin = _rope_tables(m, lp, past, s)
    new_ckv, new_kpe, new_u, new_v = [], [], [], []
    w_cur = tuple(bf(w[0]) for w in w_ffn[0])
    for layer in range(depth):
        if layer == 0:
            x_s, _ = _ffn(x_sample.reshape(b * s, d), g_ffn[0], layer, w_cur)
            x_p, w_cur = _ffn(x_prompt.reshape(lp, d), g_ffn[0], layer, w_cur, w_ffn[1], layer)
            x = (x_p, x_s)
        else:
            x, w_cur = _ffn(x, g_ffn[0], layer, w_cur, w_ffn[1], layer)
        if layer % 2 == 0:
            e = layer // 2
            u, q, ckv, kpe = _even_in(x, g_mix, layer, w_in_e, vec(ev_q_a_norm), vec(ev_kv_a_norm), w_qb,
                                      vec(ev_q_nope_norm), vec(ev_k_nope_norm), qpn, kpn, cos, sin, e,
                                      pool_dim, q_lora, kv_lora)
            k_p, v_p = _kv_prompt(ckv, kpe, w_kvb, e, lp)
            attn_p = _attn_prompt(q, k_p, v_p, lp)
            attn_s = _attn_sample(q, cache_mla_ckv, ckv, kpe_cache_t, kpe, w_k, w_v, e, lp, b, s)
            x = _even_out(x, u, state_pool, attn_p, attn_s, pool_w, vec(ev_pool_scale), w_out_e, e, lp, s, past)
            new_ckv.append((ckv[:lp], ckv[lp:]))
            new_kpe.append((kpe[:lp, :QK_ROPE], kpe[lp:, :QK_ROPE]))
            new_u.append((u[lp - POOL_HIST:lp], u[lp:].reshape(b, s, pool_dim)[:, s - POOL_HIST:]))
        else:
            o = layer // 2
            u, v, v_f32 = _odd_in(x, g_mix, layer, w_in_o, vec(od_v_norm), o, lp)
            x = _odd_out(x, u, v, ws_sel, b_sel, w_out_o, o, lp, s)
            new_v.append(v_f32)
        if layer + 1 < depth:
            x, w_cur = _ffn(x, g_ffn[1], layer, w_cur, w_ffn[0], layer + 1)
        else:
            y_p, (y_s,) = _ffn(x, g_ffn[1], layer, w_cur, split_rows=lp)

    stack = lambda pairs, k, shape: jnp.stack([p[k] for p in pairs]).reshape((n_even,) + shape)
    return (
        y_p.reshape(1, lp, d),
        y_s.reshape(b, s, d),
        stack(new_ckv, 0, (1, lp, kv_lora)),
        stack(new_kpe, 0, (1, lp, QK_ROPE)),
        stack(new_u, 0, (1, POOL_HIST, pool_dim)),
        stack(new_ckv, 1, (b, s, kv_lora)),
        stack(new_kpe, 1, (b, s, QK_ROPE)),
        stack(new_u, 1, (b, POOL_HIST, pool_dim)),
        jnp.stack(new_v).reshape(n_odd, b, s, gate),
    )
```

```python
import functools
import math

import jax
import jax.numpy as jnp
from jax import lax
from jax.experimental import pallas as pl
from jax.experimental.pallas import tpu as pltpu

F32 = jnp.float32
BF16 = jnp.bfloat16

EPS = 1e-6
CHUNK = 64
POOL_WINDOWS = (2, 4, 8, 16)
POOL_HIST = max(POOL_WINDOWS) - 1
HIST_ROWS = 16
MLA_HEADS = 8
QK_NOPE = 128
QK_ROPE = 64
V_HEAD = 128
QK_HEAD = QK_NOPE + QK_ROPE
ATTN_SCALE = QK_HEAD ** -0.5
Q_SCALE = ATTN_SCALE * math.log2(math.e)
ROPE_THETA = 10000.0
GMLP_CHUNK = 128
GMLP_GROUPS = 8
LANES = 128
NEG_BIG = -1e30
VMEM_LIMIT = 60 * 1024 * 1024


def _params(*sem):
    return pltpu.CompilerParams(dimension_semantics=sem, vmem_limit_bytes=VMEM_LIMIT)


def _tile(n, pref, mult=8):
    if n <= pref:
        return n
    for t in range(pref - pref % mult, 0, -mult):
        if n % t == 0:
            return t
    raise ValueError(f"no tile for {n}")


def _const_spec(shape):
    nd = len(shape)
    return pl.BlockSpec(shape, lambda *_: (0,) * nd, pipeline_mode=pl.Buffered(1))


def _rms(x, g):
    ms = jnp.mean(x * x, axis=-1, keepdims=True)
    return x * lax.rsqrt(ms + EPS) * g


def _row_specs(x, tm):
    if not isinstance(x, tuple):
        return (x,), [pl.BlockSpec((tm, x.shape[1]), lambda i: (i, 0))], None
    npt = x[0].shape[0] // tm
    d = x[0].shape[1]
    return x, [pl.BlockSpec((tm, d), lambda i: (jnp.minimum(i, npt - 1), 0)),
               pl.BlockSpec((tm, d), lambda i: (jnp.maximum(i - npt, 0), 0))], npt


def _read_rows(x_refs, n_prompt_tiles):
    if len(x_refs) == 1:
        return x_refs[0][...]
    return jnp.where(pl.program_id(0) < n_prompt_tiles, x_refs[0][...], x_refs[1][...])


def _dot(a, b):
    return jnp.dot(a, b, preferred_element_type=F32)


def _dot_nt(a, b):
    return lax.dot_general(a, b, (((1,), (1,)), ((), ())), preferred_element_type=F32)


def _ffn_body(x_ref, g_ref, wg_ref, wu_ref, wd_ref, *rest, convert_next, n_first):
    *rest, h_ref = rest
    if convert_next:
        (*nxt, o_ref, cg_ref, cu_ref, cd_ref) = rest

        @pl.when(pl.program_id(0) < convert_next)
        def _():
            for src, dst in zip(nxt, (cg_ref, cu_ref, cd_ref)):
                dst[...] = src[...].astype(BF16)

        outs = (o_ref,)
    else:
        outs = tuple(rest)

    def run(o_ref):
        @pl.when(pl.program_id(1) == 0)
        def _():
            x = x_ref[...]
            h_ref[...] = _rms(x, g_ref[...]).astype(BF16)
            o_ref[...] = x

        h = h_ref[...]
        a = _dot(h, wg_ref[...])
        b = _dot(h, wu_ref[...])
        act = (a * jax.nn.sigmoid(a) * b * 0.5).astype(BF16)
        o_ref[...] += _dot(act, wd_ref[...])

    if len(outs) == 1:
        run(outs[0])
    else:
        first = pl.program_id(0) < n_first
        pl.when(first)(lambda: run(outs[0]))
        pl.when(jnp.logical_not(first))(lambda: run(outs[1]))


def _ffn(x, g, layer, w, w_next=None, next_layer=None, split_rows=None):
    m, d = x.shape
    f = w[0].shape[-1]
    tm = _tile(m if split_rows is None else math.gcd(split_rows, m - split_rows), 1024)
    tf = _tile(f, 512 if split_rows is None else 256, LANES)
    n_i = m // tm
    in_specs = [
        pl.BlockSpec((tm, d), lambda i, j: (i, 0)),
        pl.BlockSpec((None, 1, d), lambda i, j: (layer, 0, 0)),
        pl.BlockSpec((d, tf), lambda i, j: (0, j)),
        pl.BlockSpec((d, tf), lambda i, j: (0, j)),
        pl.BlockSpec((tf, d), lambda i, j: (j, 0)),
    ]
    n_first = None
    if split_rows is None:
        out_specs = [pl.BlockSpec((tm, d), lambda i, j: (i, 0))]
        out_shape = [jax.ShapeDtypeStruct((m, d), F32)]
    else:
        assert w_next is None
        n_first = split_rows // tm
        out_specs = [pl.BlockSpec((tm, d), lambda i, j: (jnp.minimum(i, n_first - 1), 0)),
                     pl.BlockSpec((tm, d), lambda i, j: (jnp.maximum(i - n_first, 0), 0),
                                  pipeline_mode=pl.Buffered(1))]
        out_shape = [jax.ShapeDtypeStruct((split_rows, d), F32), jax.ShapeDtypeStruct((m - split_rows, d), F32)]
    nr = 0
    if w_next is not None:
        nr = max(r for r in (1, 2, 4, 8, 16) if r <= n_i and d % (r * LANES) == 0)
        db = d // nr
        n_j = f // tf
        blk = lambda i: jnp.minimum(i, nr - 1)
        col = lambda i, j: jnp.where(i < nr, j, n_j - 1)
        in_specs += [
            pl.BlockSpec((None, db, tf), lambda i, j: (next_layer, blk(i), col(i, j))),
            pl.BlockSpec((None, db, tf), lambda i, j: (next_layer, blk(i), col(i, j))),
            pl.BlockSpec((None, tf, db), lambda i, j: (next_layer, col(i, j), blk(i))),
        ]
        out_specs += [
            pl.BlockSpec((db, tf), lambda i, j: (blk(i), col(i, j))),
            pl.BlockSpec((db, tf), lambda i, j: (blk(i), col(i, j))),
            pl.BlockSpec((tf, db), lambda i, j: (col(i, j), blk(i))),
        ]
        out_shape += [jax.ShapeDtypeStruct((d, f), BF16), jax.ShapeDtypeStruct((d, f), BF16),
                      jax.ShapeDtypeStruct((f, d), BF16)]
    out = pl.pallas_call(
        functools.partial(_ffn_body, convert_next=nr, n_first=n_first),
        grid=(n_i, f // tf),
        in_specs=in_specs,
        out_specs=out_specs,
        out_shape=out_shape,
        scratch_shapes=[pltpu.VMEM((tm, d), BF16)],
        compiler_params=_params("arbitrary", "arbitrary"),
        name="ffn",
    )(x, g, *w, *(w_next or ()))
    return out[0], tuple(out[1:])


def _rope_tab_body(inv_ref, sgn_ref, cos_ref, sin_ref, *, tm, lp, past, s):
    r = lax.broadcasted_iota(jnp.int32, (tm, LANES), 0) + pl.program_id(0) * tm
    t = r - lp
    t = (t & (s - 1)) if s & (s - 1) == 0 else lax.rem(t, s)
    pos = jnp.where(r >= lp, past + t, r).astype(F32)
    ang = pos * inv_ref[...]
    cos_ref[...] = jnp.cos(ang)
    sin_ref[...] = jnp.sin(ang) * sgn_ref[...]


def _rope_tables(m, lp, past, s):
    half = QK_ROPE // 2
    inv = ROPE_THETA ** (-jnp.arange(0, QK_ROPE, 2, dtype=F32) / QK_ROPE)
    inv = jnp.tile(inv, LANES // half)[None, :]
    sgn = jnp.tile(jnp.concatenate([-jnp.ones((half,), F32), jnp.ones((half,), F32)]), LANES // QK_ROPE)[None, :]
    tm = _tile(m, 512)
    return pl.pallas_call(
        functools.partial(_rope_tab_body, tm=tm, lp=lp, past=past, s=s),
        grid=(m // tm,),
        in_specs=[_const_spec((1, LANES)), _const_spec((1, LANES))],
        out_specs=[pl.BlockSpec((tm, LANES), lambda i: (i, 0))] * 2,
        out_shape=[jax.ShapeDtypeStruct((m, LANES), F32)] * 2,
        compiler_params=_params("parallel"),
        name="rope_tables",
    )(inv, sgn)


def _swap_halves(x):
    lane = lax.broadcasted_iota(jnp.int32, x.shape, 1)
    left = pltpu.roll(x, LANES - QK_ROPE // 2, 1)
    right = pltpu.roll(x, QK_ROPE // 2, 1)
    return jnp.where((lane & (QK_ROPE - 1)) < QK_ROPE // 2, left, right)


def _even_in_body(*refs, n_x, n_prompt_tiles, pool_dim, q_lora, kv_lora):
    x_refs, refs = refs[:n_x], refs[n_x:]
    (g_ref, win_ref, qan_ref, kvan_ref, wqb_ref, qnn_ref, knn_ref, qpn_ref, kpn_ref, cos_ref, sin_ref,
     u_ref, q_ref, ckv_ref, kpe_ref) = refs
    h = _rms(_read_rows(x_refs, n_prompt_tiles), g_ref[...]).astype(BF16)
    o1, o2 = pool_dim + q_lora, pool_dim + q_lora + kv_lora
    cos, sin = cos_ref[...], sin_ref[...]
    lane = lax.broadcasted_iota(jnp.int32, cos.shape, 1)
    lo = lane < QK_ROPE

    def rope(t):
        return t * cos + _swap_halves(t) * sin

    qn = _rms(_dot(h, win_ref[:, pool_dim:o1]), qan_ref[...]).astype(BF16)
    q = _dot(qn, wqb_ref[...])
    zk = _dot(h, win_ref[:, o1:])
    ckv_ref[...] = _rms(zk[:, :kv_lora], kvan_ref[...])
    kp = zk[:, kv_lora:]
    kp = kp * lax.rsqrt(jnp.sum(kp * kp, axis=-1, keepdims=True) / QK_ROPE + EPS) * kpn_ref[...]
    kpe_ref[...] = rope(kp)
    u_ref[...] = _dot(h, win_ref[:, :pool_dim])

    nope_w = MLA_HEADS * QK_NOPE
    k_gain = knn_ref[...] * Q_SCALE
    for hd in range(MLA_HEADS):
        qh = _rms(q[:, hd * QK_NOPE:(hd + 1) * QK_NOPE], qnn_ref[...]) * k_gain
        q_ref[hd, :, :QK_NOPE] = qh.astype(BF16)
    for j in range(MLA_HEADS // 2):
        t = q[:, nope_w + j * LANES:nope_w + (j + 1) * LANES]
        tt = t * t
        s_lo = jnp.sum(jnp.where(lo, tt, 0.0), axis=-1, keepdims=True)
        s_hi = jnp.sum(jnp.where(lo, 0.0, tt), axis=-1, keepdims=True)
        inv = jnp.where(lo, lax.rsqrt(s_lo / QK_ROPE + EPS), lax.rsqrt(s_hi / QK_ROPE + EPS))
        t = rope(t * inv * qpn_ref[...]) * Q_SCALE
        q_ref[2 * j, :, QK_NOPE:] = jnp.where(lo, t, 0.0).astype(BF16)
        q_ref[2 * j + 1, :, QK_NOPE:] = jnp.where(lo, 0.0, t).astype(BF16)


def _even_in(x, g, layer, win, qan, kvan, wqb, qnn, knn, qpn, kpn, cos, sin, e, pool_dim, q_lora, kv_lora):
    rows = tuple(a.shape[0] for a in x) if isinstance(x, tuple) else (x.shape[0],)
    m, d = sum(rows), (x[0] if isinstance(x, tuple) else x).shape[1]
    tm = _tile(math.gcd(*rows), 512)
    xs, x_specs, npt = _row_specs(x, tm)
    n_in = win.shape[-1]
    n_q = wqb.shape[-1]
    row = lambda w: pl.BlockSpec((tm, w), lambda i: (i, 0))
    vec = lambda w, idx: pl.BlockSpec((None, 1, w), lambda i: (idx, 0, 0))
    return pl.pallas_call(
        functools.partial(_even_in_body, n_x=len(xs), n_prompt_tiles=npt, pool_dim=pool_dim, q_lora=q_lora,
                          kv_lora=kv_lora),
        grid=(m // tm,),
        in_specs=x_specs + [
            vec(d, layer),
            pl.BlockSpec((None, d, n_in), lambda i: (e, 0, 0), pipeline_mode=pl.Buffered(1)),
            vec(q_lora, e), vec(kv_lora, e),
            pl.BlockSpec((None, q_lora, n_q), lambda i: (e, 0, 0), pipeline_mode=pl.Buffered(1)),
            vec(QK_NOPE, e), vec(QK_NOPE, e), vec(LANES, e), vec(LANES, e),
            row(LANES), row(LANES),
        ],
        out_specs=[
            row(pool_dim),
            pl.BlockSpec((MLA_HEADS, tm, 2 * LANES), lambda i: (0, i, 0)),
            row(kv_lora), row(LANES),
        ],
        out_shape=[
            jax.ShapeDtypeStruct((m, pool_dim), F32),
            jax.ShapeDtypeStruct((MLA_HEADS, m, 2 * LANES), BF16),
            jax.ShapeDtypeStruct((m, kv_lora), F32),
            jax.ShapeDtypeStruct((m, LANES), F32),
        ],
        compiler_params=_params("arbitrary"),
        name="even_in",
    )(*xs, g, win, qan, kvan, wqb, qnn, knn, qpn, kpn, cos, sin)


def _kv_body(ckv_ref, kpe_ref, w_ref, k_ref, v_ref):
    c = ckv_ref[...].astype(BF16)
    kp = kpe_ref[...]
    kp_both = (kp + pltpu.roll(kp, QK_ROPE, 1)).astype(BF16)
    hw = QK_NOPE + V_HEAD
    for hd in range(MLA_HEADS):
        kv = _dot(c, w_ref[:, hd * hw:(hd + 1) * hw])
        k_ref[hd, :, :QK_NOPE] = _rms(kv[:, :QK_NOPE], 1.0).astype(BF16)
        k_ref[hd, :, QK_NOPE:] = kp_both
        v_ref[hd, :, :V_HEAD] = kv[:, QK_NOPE:].astype(BF16)
        v_ref[hd, :, V_HEAD:] = jnp.ones((c.shape[0], LANES), BF16)


def _kv_prompt(ckv, kpe, wkvb, e, lp):
    kv_lora = ckv.shape[-1]
    tm = _tile(lp, 512)
    return pl.pallas_call(
        _kv_body,
        grid=(lp // tm,),
        in_specs=[
            pl.BlockSpec((tm, kv_lora), lambda i: (i, 0)),
            pl.BlockSpec((tm, LANES), lambda i: (i, 0)),
            pl.BlockSpec((None,) + wkvb.shape[1:], lambda i: (e, 0, 0), pipeline_mode=pl.Buffered(1)),
        ],
        out_specs=[
            pl.BlockSpec((MLA_HEADS, tm, 2 * LANES), lambda i: (0, i, 0)),
            pl.BlockSpec((MLA_HEADS, tm, V_HEAD + LANES), lambda i: (0, i, 0)),
        ],
        out_shape=[
            jax.ShapeDtypeStruct((MLA_HEADS, lp, 2 * LANES), BF16),
            jax.ShapeDtypeStruct((MLA_HEADS, lp, V_HEAD + LANES), BF16),
        ],
        compiler_params=_params("parallel"),
        name="kv_prompt",
    )(ckv, kpe, wkvb)


def _flash_body(q_ref, k_ref, v_ref, *rest, t, n_side):
    side_in, (o_ref, *side_out), (sa_ref, sb_ref, m_ref, acc_ref) = rest[:n_side], rest[n_side:-4], rest[-4:]
    for src, dst in zip(side_in, side_out):
        dst[...] = src[...].astype(BF16)
    i = pl.program_id(1)
    q = q_ref[...]
    m_ref[...] = jnp.full(m_ref.shape, NEG_BIG, F32)
    acc_ref[...] = jnp.zeros(acc_ref.shape, F32)

    def scores(j):
        return _dot_nt(q, k_ref[pl.ds(pl.multiple_of(j * t, t), t), :])

    def update(s_ref, j, diagonal=False):
        s = s_ref[...]
        if diagonal:
            qc = lax.broadcasted_iota(jnp.int32, s.shape, 0) // CHUNK
            kc = lax.broadcasted_iota(jnp.int32, s.shape, 1) // CHUNK
            s = jnp.where(kc <= qc, s, NEG_BIG)
        m_old = m_ref[...]
        m_new = jnp.maximum(m_old, jnp.max(s, axis=-1, keepdims=True))
        p = jnp.exp2(s - jnp.tile(m_new, (1, t // LANES)))
        alpha = jnp.tile(jnp.exp2(m_old - m_new), (1, acc_ref.shape[1] // LANES))
        pv = _dot(p.astype(BF16), v_ref[pl.ds(pl.multiple_of(j * t, t), t), :])
        acc_ref[...] = alpha * acc_ref[...] + pv
        m_ref[...] = m_new

    sa_ref[...] = scores(0)

    def pair(jj, carry):
        j = 2 * jj
        sb_ref[...] = scores(j + 1)
        update(sa_ref, j)
        sa_ref[...] = scores(j + 2)
        update(sb_ref, j + 1)
        return carry

    lax.fori_loop(0, i // 2, pair, 0)

    @pl.when(i % 2 == 0)
    def _():
        update(sa_ref, i, diagonal=True)

    @pl.when(i % 2 == 1)
    def _():
        sb_ref[...] = scores(i)
        update(sa_ref, i - 1)
        update(sb_ref, i, diagonal=True)

    acc = acc_ref[...]
    o_ref[...] = (acc[:, :V_HEAD] / acc[:, V_HEAD:]).astype(BF16)


def _attn_prompt(q, k, v, lp, side=()):
    t = _tile(lp, 1024, LANES)
    nq = lp // t
    steps = MLA_HEADS * nq
    side2d = [a.reshape(-1, a.shape[-1]) for a in side]
    assert all(a.shape[0] % (steps * 16) == 0 for a in side2d)
    side_spec = lambda a: pl.BlockSpec((a.shape[0] // steps, a.shape[1]), lambda h, i: (h * nq + i, 0))
    out = pl.pallas_call(
        functools.partial(_flash_body, t=t, n_side=len(side)),
        grid=(MLA_HEADS, nq),
        in_specs=[
            pl.BlockSpec((None, t, 2 * LANES), lambda h, i: (h, i, 0)),
            pl.BlockSpec((None, lp, 2 * LANES), lambda h, i: (h, 0, 0)),
            pl.BlockSpec((None, lp, V_HEAD + LANES), lambda h, i: (h, 0, 0)),
        ] + [side_spec(a) for a in side2d],
        out_specs=[pl.BlockSpec((t, V_HEAD), lambda h, i: (i, h))] + [side_spec(a) for a in side2d],
        out_shape=[jax.ShapeDtypeStruct((lp, MLA_HEADS * V_HEAD), BF16)]
        + [jax.ShapeDtypeStruct(a.shape, BF16) for a in side2d],
        scratch_shapes=[pltpu.VMEM((t, t), F32), pltpu.VMEM((t, t), F32),
                        pltpu.VMEM((t, LANES), F32), pltpu.VMEM((t, V_HEAD + LANES), F32)],
        compiler_params=_params("arbitrary", "arbitrary"),
        name="attn_prompt",
    )(q, k, v, *side2d)
    return out[0], [o.reshape(a.shape) for o, a in zip(out[1:], side)]


def _attn_sample_body(q_ref, cc_ref, cn_ref, pct_ref, pn_ref, wk_ref, wv_ref, o_ref, c_sc, p_sc, *, past, s, lk):
    c_sc[:past] = cc_ref[...].astype(BF16)
    c_sc[past:past + s] = cn_ref[...].astype(BF16)
    if lk > past + s:
        c_sc[past + s:] = jnp.zeros((lk - past - s, c_sc.shape[1]), BF16)
    c = c_sc[...]
    valid = lax.broadcasted_iota(jnp.int32, (s, lk), 1) < past + s
    ones = jnp.ones((8, QK_NOPE), BF16)
    n_ch = 4 if lk % 64 == 0 else 1
    ch = lk // n_ch
    q_pe = jnp.concatenate([q_ref[hd][:, QK_NOPE:] for hd in range(MLA_HEADS)], axis=0)
    pct = pct_ref[...].astype(BF16)
    pn = pn_ref[...]
    kp_new = jnp.concatenate([(pn + pltpu.roll(pn, QK_ROPE, 1)).astype(BF16),
                              jnp.zeros((lk - past - s, LANES), BF16)], axis=0)
    pe = jnp.concatenate([_dot(q_pe, jnp.concatenate([pct, pct], axis=0)), _dot_nt(q_pe, kp_new)], axis=1)
    inv_l = []
    for pair in range(MLA_HEADS // 2):
        wk = wk_ref[:, pair * 2 * QK_NOPE:(pair + 1) * 2 * QK_NOPE]
        kk = jnp.concatenate([_dot(c[r * ch:(r + 1) * ch], wk) for r in range(n_ch)])
        for hd in (2 * pair, 2 * pair + 1):
            kn = kk[:, (hd % 2) * QK_NOPE:(hd % 2 + 1) * QK_NOPE]
            ms = _dot_nt(ones, (kn * kn).astype(BF16))[0:1] / QK_NOPE
            sc = _dot_nt(q_ref[hd][:, :QK_NOPE], kn.astype(BF16)) * lax.rsqrt(ms + EPS) + pe[hd * s:(hd + 1) * s]
            sc = jnp.where(valid, sc, NEG_BIG)
            p = jnp.exp2(sc - jnp.max(sc, axis=-1, keepdims=True))
            inv_l.append(1.0 / jnp.sum(p, axis=-1, keepdims=True))
            p_sc[hd * s:(hd + 1) * s] = p.astype(BF16)
    ctx = _dot(p_sc[...], c)
    for hd in range(MLA_HEADS):
        ctx_h = (ctx[hd * s:(hd + 1) * s] * inv_l[hd]).astype(BF16)
        o_ref[:, hd * V_HEAD:(hd + 1) * V_HEAD] = _dot(ctx_h, wv_ref[:, hd * V_HEAD:(hd + 1) * V_HEAD]).astype(BF16)


def _attn_sample(q, ckv_cache, ckv, kpe_cache, kpe, wk, wv, e, lp, b, s):
    past, kv_lora = ckv_cache.shape[2], ckv_cache.shape[3]
    lk = past + -(-s // LANES) * LANES
    row0 = lp // s
    return pl.pallas_call(
        functools.partial(_attn_sample_body, past=past, s=s, lk=lk),
        grid=(b,),
        in_specs=[
            pl.BlockSpec((MLA_HEADS, s, 2 * LANES), lambda i: (0, row0 + i, 0)),
            pl.BlockSpec((None, None, past, kv_lora), lambda i: (e, i, 0, 0)),
            pl.BlockSpec((s, kv_lora), lambda i: (row0 + i, 0)),
            pl.BlockSpec((None, None, QK_ROPE, past), lambda i: (e, i, 0, 0)),
            pl.BlockSpec((s, LANES), lambda i: (row0 + i, 0)),
            pl.BlockSpec((None,) + wk.shape[1:], lambda i: (e, 0, 0), pipeline_mode=pl.Buffered(1)),
            pl.BlockSpec((None,) + wv.shape[1:], lambda i: (e, 0, 0), pipeline_mode=pl.Buffered(1)),
        ],
        out_specs=pl.BlockSpec((s, MLA_HEADS * V_HEAD), lambda i: (i, 0)),
        out_shape=jax.ShapeDtypeStruct((b * s, MLA_HEADS * V_HEAD), BF16),
        scratch_shapes=[pltpu.VMEM((lk, kv_lora), BF16), pltpu.VMEM((MLA_HEADS * s, lk), BF16)],
        compiler_params=_params("parallel"),
        name="attn_sample",
    )(q, ckv_cache, ckv, kpe_cache, kpe, wk, wv)


def _pool_segment(ext_ref, pooled_ref, row0, n, pos0, gdim):
    pos = (lax.broadcasted_iota(jnp.int32, (n, 1), 0) + pos0).astype(F32)
    for gi, w in enumerate(POOL_WINDOWS):
        cols = slice(gi * gdim, (gi + 1) * gdim)
        u = ext_ref[HIST_ROWS:HIST_ROWS + n, cols]
        acc = u
        for k in range(1, w):
            acc = acc + ext_ref[HIST_ROWS - k:HIST_ROWS - k + n, cols]
        cnt = jnp.minimum(jnp.float32(w), pos + 1.0)
        pooled_ref[row0:row0 + n, cols] = (acc / cnt - u).astype(BF16)


def _pool_project(pooled_ref, pw_ref, sc_ref, o_ref, gdim):
    for gi in range(len(POOL_WINDOWS)):
        cols = slice(gi * gdim, (gi + 1) * gdim)
        o_ref[:, cols] = (_dot(pooled_ref[:, cols], pw_ref[gi]) * sc_ref[:, cols]).astype(BF16)


def _even_out_body(*refs, n_x, tm, n_prompt_tiles, s, past, pool_dim):
    x_refs, refs = refs[:n_x], refs[n_x:]
    (u_ref, hist_ref, ap_ref, as_ref, pw_ref, sc_ref, w_ref, o_ref, ext_ref, exts_ref, pooled_ref, pool_ref) = refs
    i = pl.program_id(0)
    gdim = pool_dim // len(POOL_WINDOWS)

    def project(a_ref, x_ref):
        _pool_project(pooled_ref, pw_ref, sc_ref, pool_ref, gdim)
        o_ref[...] = x_ref[...] + _dot(pool_ref[...], w_ref[:pool_dim]) + _dot(a_ref[...], w_ref[pool_dim:])

    @pl.when(i == 0)
    def _():
        ext_ref[:HIST_ROWS] = jnp.zeros((HIST_ROWS, pool_dim), F32)

    @pl.when(i < n_prompt_tiles)
    def _():
        ext_ref[HIST_ROWS:] = u_ref[...]
        _pool_segment(ext_ref, pooled_ref, 0, tm, i * tm, gdim)
        ext_ref[:HIST_ROWS] = ext_ref[tm:tm + HIST_ROWS]
        project(ap_ref, x_refs[0])

    @pl.when(i >= n_prompt_tiles)
    def _():
        for bi in range(tm // s):
            ext = exts_ref.at[bi]
            ext[0:1] = jnp.zeros((1, pool_dim), F32)
            ext[1:HIST_ROWS] = hist_ref[bi]
            ext[HIST_ROWS:] = u_ref[bi * s:(bi + 1) * s]
            _pool_segment(ext, pooled_ref, bi * s, s, past, gdim)
        project(as_ref, x_refs[-1])


def _even_out(x, u, state_pool, attn_p, attn_s, pw, scale, wout, e, lp, s, past):
    m, pool_dim = u.shape
    attn_dim = attn_p.shape[1]
    ng = len(POOL_WINDOWS)
    tm = _tile(math.gcd(lp, m - lp), 512, math.lcm(HIST_ROWS, s))
    npt = lp // tm
    xs, x_specs, _ = _row_specs(x, tm)
    d = xs[0].shape[1]
    return pl.pallas_call(
        functools.partial(_even_out_body, n_x=len(xs), tm=tm, n_prompt_tiles=npt, s=s, past=past,
                          pool_dim=pool_dim),
        grid=(m // tm,),
        in_specs=x_specs + [
            pl.BlockSpec((tm, pool_dim), lambda i: (i, 0)),
            pl.BlockSpec((None, tm // s, POOL_HIST, pool_dim), lambda i: (e, jnp.maximum(i - npt, 0), 0, 0)),
            pl.BlockSpec((tm, attn_dim), lambda i: (jnp.minimum(i, npt - 1), 0)),
            pl.BlockSpec((tm, attn_dim), lambda i: (jnp.maximum(i - npt, 0), 0)),
            pl.BlockSpec((None, ng, pool_dim // ng, pool_dim // ng), lambda i: (e, 0, 0, 0)),
            pl.BlockSpec((None, 1, pool_dim), lambda i: (e, 0, 0)),
            pl.BlockSpec((None,) + wout.shape[1:], lambda i: (e, 0, 0), pipeline_mode=pl.Buffered(1)),
        ],
        out_specs=pl.BlockSpec((tm, d), lambda i: (i, 0)),
        out_shape=jax.ShapeDtypeStruct((m, d), F32),
        scratch_shapes=[pltpu.VMEM((HIST_ROWS + tm, pool_dim), F32),
                        pltpu.VMEM((tm // s, HIST_ROWS + s, pool_dim), F32),
                        pltpu.VMEM((tm, pool_dim), BF16), pltpu.VMEM((tm, pool_dim), BF16)],
        compiler_params=_params("arbitrary"),
        name="even_out",
    )(*xs, u, state_pool, attn_p, attn_s, pw, scale, wout)


def _odd_in_body(x_ref, g_ref, w_ref, vn_ref, u_ref, v_ref, vf_ref, *, gate):
    h = _rms(x_ref[...], g_ref[...]).astype(BF16)
    v = _rms(jax.nn.gelu(_dot(h, w_ref[:, gate:])), vn_ref[...])
    v_ref[...] = v.astype(BF16)
    vf_ref[...] = v
    n_chunks = 4 if gate % (4 * LANES) == 0 else 1
    cw = gate // n_chunks
    for c in range(n_chunks):
        u_ref[:, c * cw:(c + 1) * cw] = jax.nn.gelu(_dot(h, w_ref[:, c * cw:(c + 1) * cw])).astype(BF16)


def _odd_in(x, g, layer, win, vn, o, lp):
    m, d = x.shape
    gate = win.shape[-1] // 2
    tm = _tile(math.gcd(lp, m - lp), 512)
    first = lp // tm
    return pl.pallas_call(
        functools.partial(_odd_in_body, gate=gate),
        grid=(m // tm,),
        in_specs=[
            pl.BlockSpec((tm, d), lambda i: (i, 0)),
            pl.BlockSpec((None, 1, d), lambda i: (layer, 0, 0)),
            pl.BlockSpec((None, d, 2 * gate), lambda i: (o, 0, 0), pipeline_mode=pl.Buffered(1)),
            pl.BlockSpec((None, 1, gate), lambda i: (o, 0, 0)),
        ],
        out_specs=[
            pl.BlockSpec((tm, gate), lambda i: (i, 0)),
            pl.BlockSpec((tm, gate), lambda i: (i, 0)),
            pl.BlockSpec((tm, gate), lambda i: (jnp.maximum(i - first, 0), 0)),
        ],
        out_shape=[
            jax.ShapeDtypeStruct((m, gate), BF16),
            jax.ShapeDtypeStruct((m, gate), BF16),
            jax.ShapeDtypeStruct((m - lp, gate), F32),
        ],
        compiler_params=_params("arbitrary"),
        name="odd_in",
    )(x, g, win, vn)


def _odd_out_body(x_ref, u_ref, v_ref, ws_ref, b_ref, w_ref, o_ref, us_ref, *, tm, n_prompt_tiles, s):
    is_prompt = pl.program_id(0) < n_prompt_tiles
    ii = lax.broadcasted_iota(jnp.int32, (GMLP_CHUNK, GMLP_CHUNK), 0)
    jj = lax.broadcasted_iota(jnp.int32, (GMLP_CHUNK, GMLP_CHUNK), 1)
    same_stream = jnp.where((ii // s) == (jj // s), 1, 0) + jnp.where(is_prompt, 1, 0)
    keep = (jj <= ii) & (same_stream > 0)
    gdim = u_ref.shape[1] // GMLP_GROUPS
    for gi in range(GMLP_GROUPS):
        cols = slice(gi * gdim, (gi + 1) * gdim)
        wmat = jnp.where(keep, ws_ref[gi], 0.0).astype(BF16)
        bias = b_ref[:, gi:gi + 1]
        for ci in range(tm // GMLP_CHUNK):
            rows = slice(ci * GMLP_CHUNK, (ci + 1) * GMLP_CHUNK)
            sg = _dot(wmat, v_ref[rows, cols]) + bias
            us_ref[rows, cols] = (u_ref[rows, cols].astype(F32) * sg).astype(BF16)
    o_ref[...] = x_ref[...] + _dot(us_ref[...], w_ref[...])


def _odd_out(x, u, v, ws_sel, b_sel, wout, o, lp, s):
    m, d = x.shape
    gate = u.shape[1]
    tm = _tile(math.gcd(lp, m - lp), 512, GMLP_CHUNK)
    npt = lp // tm
    sel = lambda i: jnp.where(i >= npt, 1, 0)
    return pl.pallas_call(
        functools.partial(_odd_out_body, tm=tm, n_prompt_tiles=npt, s=s),
        grid=(m // tm,),
        in_specs=[
            pl.BlockSpec((tm, d), lambda i: (i, 0)),
            pl.BlockSpec((tm, gate), lambda i: (i, 0)),
            pl.BlockSpec((tm, gate), lambda i: (i, 0)),
            pl.BlockSpec((None, None, GMLP_GROUPS, GMLP_CHUNK, GMLP_CHUNK), lambda i: (sel(i), o, 0, 0, 0)),
            pl.BlockSpec((None, None, GMLP_CHUNK, GMLP_GROUPS), lambda i: (sel(i), o, 0, 0)),
            pl.BlockSpec((None,) + wout.shape[1:], lambda i: (o, 0, 0), pipeline_mode=pl.Buffered(1)),
        ],
        out_specs=pl.BlockSpec((tm, d), lambda i: (i, 0)),
        out_shape=jax.ShapeDtypeStruct((m, d), F32),
        scratch_shapes=[pltpu.VMEM((tm, gate), BF16)],
        compiler_params=_params("parallel"),
        name="odd_out",
    )(x, u, v, ws_sel, b_sel, wout)


def kernel(x_prompt, x_sample, cache_mla_ckv, cache_mla_kpe, state_pool, norm_ffn1, norm_mix, norm_ffn2, ffn1_w_gate, ffn1_w_up, ffn1_w_down, ffn2_w_gate, ffn2_w_up, ffn2_w_down, ev_w_in, ev_q_a_norm, ev_kv_a_norm, ev_w_qb, ev_w_kvb, ev_q_nope_norm, ev_q_pe_norm, ev_k_nope_norm, ev_k_pe_norm, ev_pool_w, ev_pool_scale, ev_w_out, od_w_in, od_v_norm, od_w_s, od_b_s, od_w_out):
    bp, lp, d = x_prompt.shape
    b, s, _ = x_sample.shape
    depth = norm_mix.shape[0]
    n_even, n_odd = ev_w_in.shape[0], od_w_in.shape[0]
    past = cache_mla_ckv.shape[2]
    q_lora, kv_lora = ev_q_a_norm.shape[1], ev_kv_a_norm.shape[1]
    pool_dim = ev_pool_scale.shape[1]
    gate = od_v_norm.shape[1]
    m = lp + b * s
    assert bp == 1 and lp % GMLP_CHUNK == 0 and (b * s) % GMLP_CHUNK == 0 and GMLP_CHUNK % s == 0
    assert past % CHUNK == 0 and s <= CHUNK, "every cached and new key must be visible to every sample query"
    assert s >= POOL_HIST and lp >= POOL_HIST and lp % s == 0
    assert ev_w_in.shape[2] == pool_dim + q_lora + kv_lora + QK_ROPE

    vec = lambda a: a[:, None, :]
    bf = lambda a: a.astype(BF16)
    pad_lanes = lambda a: jnp.pad(a, [(0, 0)] * (a.ndim - 1) + [(0, (-a.shape[-1]) % LANES)])

    w_ffn = ((ffn1_w_gate, ffn1_w_up, ffn1_w_down), (ffn2_w_gate, ffn2_w_up, ffn2_w_down))
    g_ffn = (vec(norm_ffn1), vec(norm_ffn2))
    g_mix = vec(norm_mix)
    w_in_e = bf(pad_lanes(ev_w_in))
    wq = ev_w_qb.reshape(n_even, q_lora, MLA_HEADS, QK_HEAD)
    w_qb = bf(jnp.concatenate([wq[..., :QK_NOPE].reshape(n_even, q_lora, -1),
                               wq[..., QK_NOPE:].reshape(n_even, q_lora, -1)], axis=-1))
    w_kvb, pool_w = bf(ev_w_kvb), bf(ev_pool_w)
    wkv = w_kvb.reshape(n_even, kv_lora, MLA_HEADS, QK_NOPE + V_HEAD)
    w_k = wkv[..., :QK_NOPE].reshape(n_even, kv_lora, -1)
    w_v = wkv[..., QK_NOPE:].reshape(n_even, kv_lora, -1)
    qpn = vec(jnp.tile(ev_q_pe_norm, (1, LANES // QK_ROPE)))
    kpn = vec(pad_lanes(ev_k_pe_norm))
    kpe_cache_t = jnp.swapaxes(cache_mla_kpe, 2, 3)
    rep = GMLP_CHUNK // s
    ws_sel = jnp.stack([od_w_s, jnp.tile(od_w_s[:, :, :s, :s], (1, 1, rep, rep))])
    b_sel = jnp.stack([od_b_s, jnp.tile(od_b_s[:, :, :s], (1, 1, rep))]).transpose(0, 1, 3, 2)

    cos, sin = _rope_tables(m, lp, past, s)
    new_ckv, new_kpe, new_u, new_v = [], [], [], []
    w_cur = tuple(bf(w[0]) for w in w_ffn[0])
    for layer in range(depth):
        if layer == 0:
            x_s, _ = _ffn(x_sample.reshape(b * s, d), g_ffn[0], layer, w_cur)
            x_p, w_cur = _ffn(x_prompt.reshape(lp, d), g_ffn[0], layer, w_cur, w_ffn[1], layer)
            x = (x_p, x_s)
        else:
            x, w_cur = _ffn(x, g_ffn[0], layer, w_cur, w_ffn[1], layer)
        if layer % 2 == 0:
            e = layer // 2
            u, q, ckv, kpe = _even_in(x, g_mix, layer, w_in_e, vec(ev_q_a_norm), vec(ev_kv_a_norm), w_qb,
                                      vec(ev_q_nope_norm), vec(ev_k_nope_norm), qpn, kpn, cos, sin, e,
                                      pool_dim, q_lora, kv_lora)
            k_p, v_p = _kv_prompt(ckv, kpe, w_kvb, e, lp)
            if e == 0:
                attn_p, (w_out_e, w_in_o, w_out_o) = _attn_prompt(q, k_p, v_p, lp, (ev_w_out, od_w_in, od_w_out))
            else:
                attn_p, _ = _attn_prompt(q, k_p, v_p, lp)
            attn_s = _attn_sample(q, cache_mla_ckv, ckv, kpe_cache_t, kpe, w_k, w_v, e, lp, b, s)
            x = _even_out(x, u, state_pool, attn_p, attn_s, pool_w, vec(ev_pool_scale), w_out_e, e, lp, s, past)
            new_ckv.append((ckv[:lp], ckv[lp:]))
            new_kpe.append((kpe[:lp, :QK_ROPE], kpe[lp:, :QK_ROPE]))
            new_u.append((u[lp - POOL_HIST:lp], u[lp:].reshape(b, s, pool_dim)[:, s - POOL_HIST:]))
        else:
            o = layer // 2
            u, v, v_f32 = _odd_in(x, g_mix, layer, w_in_o, vec(od_v_norm), o, lp)
            x = _odd_out(x, u, v, ws_sel, b_sel, w_out_o, o, lp, s)
            new_v.append(v_f32)
        if layer + 1 < depth:
            x, w_cur = _ffn(x, g_ffn[1], layer, w_cur, w_ffn[0], layer + 1)
        else:
            y_p, (y_s,) = _ffn(x, g_ffn[1], layer, w_cur, split_rows=lp)

    stack = lambda pairs, k, shape: jnp.stack([p[k] for p in pairs]).reshape((n_even,) + shape)
    return (
        y_p.reshape(1, lp, d),
        y_s.reshape(b, s, d),
        stack(new_ckv, 0, (1, lp, kv_lora)),
        stack(new_kpe, 0, (1, lp, QK_ROPE)),
        stack(new_u, 0, (1, POOL_HIST, pool_dim)),
        stack(new_ckv, 1, (b, s, kv_lora)),
        stack(new_kpe, 1, (b, s, QK_ROPE)),
        stack(new_u, 1, (b, POOL_HIST, pool_dim)),
        jnp.stack(new_v).reshape(n_odd, b, s, gate),
    )
```

```python
import functools
import math

import jax
import jax.numpy as jnp
from jax import lax
from jax.experimental import pallas as pl
from jax.experimental.pallas import tpu as pltpu

F32 = jnp.float32
BF16 = jnp.bfloat16

EPS = 1e-6
CHUNK = 64
POOL_WINDOWS = (2, 4, 8, 16)
POOL_HIST = max(POOL_WINDOWS) - 1
HIST_ROWS = 16
MLA_HEADS = 8
QK_NOPE = 128
QK_ROPE = 64
V_HEAD = 128
QK_HEAD = QK_NOPE + QK_ROPE
ATTN_SCALE = QK_HEAD ** -0.5
Q_SCALE = ATTN_SCALE * math.log2(math.e)
ROPE_THETA = 10000.0
GMLP_CHUNK = 128
GMLP_GROUPS = 8
LANES = 128
NEG_BIG = -1e30
VMEM_LIMIT = 60 * 1024 * 1024


def _params(*sem):
    return pltpu.CompilerParams(dimension_semantics=sem, vmem_limit_bytes=VMEM_LIMIT)


def _tile(n, pref, mult=8):
    if n <= pref:
        return n
    for t in range(pref - pref % mult, 0, -mult):
        if n % t == 0:
            return t
    raise ValueError(f"no tile for {n}")


def _const_spec(shape):
    nd = len(shape)
    return pl.BlockSpec(shape, lambda *_: (0,) * nd, pipeline_mode=pl.Buffered(1))


def _rms(x, g):
    ms = jnp.mean(x * x, axis=-1, keepdims=True)
    return x * lax.rsqrt(ms + EPS) * g


def _row_specs(x, tm):
    if not isinstance(x, tuple):
        return (x,), [pl.BlockSpec((tm, x.shape[1]), lambda i: (i, 0))], None
    npt = x[0].shape[0] // tm
    d = x[0].shape[1]
    return x, [pl.BlockSpec((tm, d), lambda i: (jnp.minimum(i, npt - 1), 0)),
               pl.BlockSpec((tm, d), lambda i: (jnp.maximum(i - npt, 0), 0))], npt


def _read_rows(x_refs, n_prompt_tiles):
    if len(x_refs) == 1:
        return x_refs[0][...]
    return jnp.where(pl.program_id(0) < n_prompt_tiles, x_refs[0][...], x_refs[1][...])


def _dot(a, b):
    return jnp.dot(a, b, preferred_element_type=F32)


def _dot_nt(a, b):
    return lax.dot_general(a, b, (((1,), (1,)), ((), ())), preferred_element_type=F32)


def _ffn_body(x_ref, g_ref, wg_ref, wu_ref, wd_ref, *rest, convert_next, n_first):
    *rest, h_ref = rest
    if convert_next:
        (*nxt, o_ref, cg_ref, cu_ref, cd_ref) = rest

        @pl.when(pl.program_id(0) < convert_next)
        def _():
            for src, dst in zip(nxt, (cg_ref, cu_ref, cd_ref)):
                dst[...] = src[...].astype(BF16)

        outs = (o_ref,)
    else:
        outs = tuple(rest)

    def run(o_ref):
        @pl.when(pl.program_id(1) == 0)
        def _():
            x = x_ref[...]
            h_ref[...] = _rms(x, g_ref[...]).astype(BF16)
            o_ref[...] = x

        h = h_ref[...]
        a = _dot(h, wg_ref[...])
        b = _dot(h, wu_ref[...])
        act = (a * jax.nn.sigmoid(a) * b * 0.5).astype(BF16)
        o_ref[...] += _dot(act, wd_ref[...])

    if len(outs) == 1:
        run(outs[0])
    else:
        first = pl.program_id(0) < n_first
        pl.when(first)(lambda: run(outs[0]))
        pl.when(jnp.logical_not(first))(lambda: run(outs[1]))


def _ffn(x, g, layer, w, w_next=None, next_layer=None, split_rows=None):
    m, d = x.shape
    f = w[0].shape[-1]
    tm = _tile(m if split_rows is None else math.gcd(split_rows, m - split_rows), 1024)
    tf = _tile(f, 512 if split_rows is None else 256, LANES)
    n_i = m // tm
    in_specs = [
        pl.BlockSpec((tm, d), lambda i, j: (i, 0)),
        pl.BlockSpec((None, 1, d), lambda i, j: (layer, 0, 0)),
        pl.BlockSpec((d, tf), lambda i, j: (0, j)),
        pl.BlockSpec((d, tf), lambda i, j: (0, j)),
        pl.BlockSpec((tf, d), lambda i, j: (j, 0)),
    ]
    n_first = None
    if split_rows is None:
        out_specs = [pl.BlockSpec((tm, d), lambda i, j: (i, 0))]
        out_shape = [jax.ShapeDtypeStruct((m, d), F32)]
    else:
        assert w_next is None
        n_first = split_rows // tm
        out_specs = [pl.BlockSpec((tm, d), lambda i, j: (jnp.minimum(i, n_first - 1), 0)),
                     pl.BlockSpec((tm, d), lambda i, j: (jnp.maximum(i - n_first, 0), 0),
                                  pipeline_mode=pl.Buffered(1))]
        out_shape = [jax.ShapeDtypeStruct((split_rows, d), F32), jax.ShapeDtypeStruct((m - split_rows, d), F32)]
    nr = 0
    if w_next is not None:
        nr = max(r for r in (1, 2, 4, 8, 16) if r <= n_i and d % (r * LANES) == 0)
        db = d // nr
        n_j = f // tf
        blk = lambda i: jnp.minimum(i, nr - 1)
        col = lambda i, j: jnp.where(i < nr, j, n_j - 1)
        in_specs += [
            pl.BlockSpec((None, db, tf), lambda i, j: (next_layer, blk(i), col(i, j))),
            pl.BlockSpec((None, db, tf), lambda i, j: (next_layer, blk(i), col(i, j))),
            pl.BlockSpec((None, tf, db), lambda i, j: (next_layer, col(i, j), blk(i))),
        ]
        out_specs += [
            pl.BlockSpec((db, tf), lambda i, j: (blk(i), col(i, j))),
            pl.BlockSpec((db, tf), lambda i, j: (blk(i), col(i, j))),
            pl.BlockSpec((tf, db), lambda i, j: (col(i, j), blk(i))),
        ]
        out_shape += [jax.ShapeDtypeStruct((d, f), BF16), jax.ShapeDtypeStruct((d, f), BF16),
                      jax.ShapeDtypeStruct((f, d), BF16)]
    out = pl.pallas_call(
        functools.partial(_ffn_body, convert_next=nr, n_first=n_first),
        grid=(n_i, f // tf),
        in_specs=in_specs,
        out_specs=out_specs,
        out_shape=out_shape,
        scratch_shapes=[pltpu.VMEM((tm, d), BF16)],
        compiler_params=_params("arbitrary", "arbitrary"),
        name="ffn",
    )(x, g, *w, *(w_next or ()))
    return out[0], tuple(out[1:])


def _rope_tab_body(inv_ref, sgn_ref, win_ref, cos_ref, sin_ref, wout_ref, *, tm, lp, past, s, n_blocks):
    r = lax.broadcasted_iota(jnp.int32, (tm, LANES), 0) + pl.program_id(0) * tm
    t = r - lp
    t = (t & (s - 1)) if s & (s - 1) == 0 else lax.rem(t, s)
    pos = jnp.where(r >= lp, past + t, r).astype(F32)
    ang = pos * inv_ref[...]
    cos_ref[...] = jnp.cos(ang)
    sin_ref[...] = jnp.sin(ang) * sgn_ref[...]

    @pl.when(pl.program_id(0) < n_blocks)
    def _():
        n_raw = win_ref.shape[1]
        wout_ref[:, :n_raw] = win_ref[...].astype(BF16)
        wout_ref[:, n_raw:] = jnp.zeros((wout_ref.shape[0], wout_ref.shape[1] - n_raw), BF16)


def _rope_tables(m, lp, past, s, w_in):
    half = QK_ROPE // 2
    inv = ROPE_THETA ** (-jnp.arange(0, QK_ROPE, 2, dtype=F32) / QK_ROPE)
    inv = jnp.tile(inv, LANES // half)[None, :]
    sgn = jnp.tile(jnp.concatenate([-jnp.ones((half,), F32), jnp.ones((half,), F32)]), LANES // QK_ROPE)[None, :]
    tm = _tile(m, 512)
    steps = m // tm
    w2d = w_in.reshape(-1, w_in.shape[-1])
    rows, n_raw = w2d.shape
    n_pad = n_raw + (-n_raw) % LANES
    nb = max(r for r in (1, 2, 4, 8, 16, 32, 64) if r <= steps and rows % (r * 16) == 0)
    blk = lambda i: (jnp.minimum(i, nb - 1), 0)
    cos, sin, w_bf = pl.pallas_call(
        functools.partial(_rope_tab_body, tm=tm, lp=lp, past=past, s=s, n_blocks=nb),
        grid=(steps,),
        in_specs=[_const_spec((1, LANES)), _const_spec((1, LANES)), pl.BlockSpec((rows // nb, n_raw), blk)],
        out_specs=[pl.BlockSpec((tm, LANES), lambda i: (i, 0))] * 2 + [pl.BlockSpec((rows // nb, n_pad), blk)],
        out_shape=[jax.ShapeDtypeStruct((m, LANES), F32)] * 2 + [jax.ShapeDtypeStruct((rows, n_pad), BF16)],
        compiler_params=_params("arbitrary"),
        name="rope_tables",
    )(inv, sgn, w2d)
    return cos, sin, w_bf.reshape(w_in.shape[:-1] + (n_pad,))


def _swap_halves(x):
    lane = lax.broadcasted_iota(jnp.int32, x.shape, 1)
    left = pltpu.roll(x, LANES - QK_ROPE // 2, 1)
    right = pltpu.roll(x, QK_ROPE // 2, 1)
    return jnp.where((lane & (QK_ROPE - 1)) < QK_ROPE // 2, left, right)


def _even_in_body(*refs, n_x, n_prompt_tiles, pool_dim, q_lora, kv_lora):
    x_refs, refs = refs[:n_x], refs[n_x:]
    (g_ref, win_ref, qan_ref, kvan_ref, wqb_ref, qnn_ref, knn_ref, qpn_ref, kpn_ref, cos_ref, sin_ref,
     u_ref, q_ref, ckv_ref, kpe_ref) = refs
    h = _rms(_read_rows(x_refs, n_prompt_tiles), g_ref[...]).astype(BF16)
    o1, o2 = pool_dim + q_lora, pool_dim + q_lora + kv_lora
    cos, sin = cos_ref[...], sin_ref[...]
    lane = lax.broadcasted_iota(jnp.int32, cos.shape, 1)
    lo = lane < QK_ROPE

    def rope(t):
        return t * cos + _swap_halves(t) * sin

    qn = _rms(_dot(h, win_ref[:, pool_dim:o1]), qan_ref[...]).astype(BF16)
    q = _dot(qn, wqb_ref[...])
    zk = _dot(h, win_ref[:, o1:])
    ckv_ref[...] = _rms(zk[:, :kv_lora], kvan_ref[...])
    kp = zk[:, kv_lora:]
    kp = kp * lax.rsqrt(jnp.sum(kp * kp, axis=-1, keepdims=True) / QK_ROPE + EPS) * kpn_ref[...]
    kpe_ref[...] = rope(kp)
    u_ref[...] = _dot(h, win_ref[:, :pool_dim])

    nope_w = MLA_HEADS * QK_NOPE
    k_gain = knn_ref[...] * Q_SCALE
    for hd in range(MLA_HEADS):
        qh = _rms(q[:, hd * QK_NOPE:(hd + 1) * QK_NOPE], qnn_ref[...]) * k_gain
        q_ref[hd, :, :QK_NOPE] = qh.astype(BF16)
    for j in range(MLA_HEADS // 2):
        t = q[:, nope_w + j * LANES:nope_w + (j + 1) * LANES]
        tt = t * t
        s_lo = jnp.sum(jnp.where(lo, tt, 0.0), axis=-1, keepdims=True)
        s_hi = jnp.sum(jnp.where(lo, 0.0, tt), axis=-1, keepdims=True)
        inv = jnp.where(lo, lax.rsqrt(s_lo / QK_ROPE + EPS), lax.rsqrt(s_hi / QK_ROPE + EPS))
        t = rope(t * inv * qpn_ref[...]) * Q_SCALE
        q_ref[2 * j, :, QK_NOPE:] = jnp.where(lo, t, 0.0).astype(BF16)
        q_ref[2 * j + 1, :, QK_NOPE:] = jnp.where(lo, 0.0, t).astype(BF16)


def _even_in(x, g, layer, win, qan, kvan, wqb, qnn, knn, qpn, kpn, cos, sin, e, pool_dim, q_lora, kv_lora):
    rows = tuple(a.shape[0] for a in x) if isinstance(x, tuple) else (x.shape[0],)
    m, d = sum(rows), (x[0] if isinstance(x, tuple) else x).shape[1]
    tm = _tile(math.gcd(*rows), 512)
    xs, x_specs, npt = _row_specs(x, tm)
    n_in = win.shape[-1]
    n_q = wqb.shape[-1]
    row = lambda w: pl.BlockSpec((tm, w), lambda i: (i, 0))
    vec = lambda w, idx: pl.BlockSpec((None, 1, w), lambda i: (idx, 0, 0))
    return pl.pallas_call(
        functools.partial(_even_in_body, n_x=len(xs), n_prompt_tiles=npt, pool_dim=pool_dim, q_lora=q_lora,
                          kv_lora=kv_lora),
        grid=(m // tm,),
        in_specs=x_specs + [
            vec(d, layer),
            pl.BlockSpec((None, d, n_in), lambda i: (e, 0, 0), pipeline_mode=pl.Buffered(1)),
            vec(q_lora, e), vec(kv_lora, e),
            pl.BlockSpec((None, q_lora, n_q), lambda i: (e, 0, 0), pipeline_mode=pl.Buffered(1)),
            vec(QK_NOPE, e), vec(QK_NOPE, e), vec(LANES, e), vec(LANES, e),
            row(LANES), row(LANES),
        ],
        out_specs=[
            row(pool_dim),
            pl.BlockSpec((MLA_HEADS, tm, 2 * LANES), lambda i: (0, i, 0)),
            row(kv_lora), row(LANES),
        ],
        out_shape=[
            jax.ShapeDtypeStruct((m, pool_dim), F32),
            jax.ShapeDtypeStruct((MLA_HEADS, m, 2 * LANES), BF16),
            jax.ShapeDtypeStruct((m, kv_lora), F32),
            jax.ShapeDtypeStruct((m, LANES), F32),
        ],
        compiler_params=_params("arbitrary"),
        name="even_in",
    )(*xs, g, win, qan, kvan, wqb, qnn, knn, qpn, kpn, cos, sin)


def _kv_body(ckv_ref, kpe_ref, w_ref, k_ref, v_ref):
    c = ckv_ref[...].astype(BF16)
    kp = kpe_ref[...]
    kp_both = (kp + pltpu.roll(kp, QK_ROPE, 1)).astype(BF16)
    hw = QK_NOPE + V_HEAD
    for hd in range(MLA_HEADS):
        kv = _dot(c, w_ref[:, hd * hw:(hd + 1) * hw])
        k_ref[hd, :, :QK_NOPE] = _rms(kv[:, :QK_NOPE], 1.0).astype(BF16)
        k_ref[hd, :, QK_NOPE:] = kp_both
        v_ref[hd, :, :V_HEAD] = kv[:, QK_NOPE:].astype(BF16)
        v_ref[hd, :, V_HEAD:] = jnp.ones((c.shape[0], LANES), BF16)


def _kv_prompt(ckv, kpe, wkvb, e, lp):
    kv_lora = ckv.shape[-1]
    tm = _tile(lp, 512)
    return pl.pallas_call(
        _kv_body,
        grid=(lp // tm,),
        in_specs=[
            pl.BlockSpec((tm, kv_lora), lambda i: (i, 0)),
            pl.BlockSpec((tm, LANES), lambda i: (i, 0)),
            pl.BlockSpec((None,) + wkvb.shape[1:], lambda i: (e, 0, 0), pipeline_mode=pl.Buffered(1)),
        ],
        out_specs=[
            pl.BlockSpec((MLA_HEADS, tm, 2 * LANES), lambda i: (0, i, 0)),
            pl.BlockSpec((MLA_HEADS, tm, V_HEAD + LANES), lambda i: (0, i, 0)),
        ],
        out_shape=[
            jax.ShapeDtypeStruct((MLA_HEADS, lp, 2 * LANES), BF16),
            jax.ShapeDtypeStruct((MLA_HEADS, lp, V_HEAD + LANES), BF16),
        ],
        compiler_params=_params("parallel"),
        name="kv_prompt",
    )(ckv, kpe, wkvb)


def _flash_body(q_ref, k_ref, v_ref, *rest, t, n_side):
    side_in, (o_ref, *side_out), (sa_ref, sb_ref, m_ref, acc_ref) = rest[:n_side], rest[n_side:-4], rest[-4:]
    for src, dst in zip(side_in, side_out):
        dst[...] = src[...].astype(BF16)
    i = pl.program_id(1)
    q = q_ref[...]
    m_ref[...] = jnp.full(m_ref.shape, NEG_BIG, F32)
    acc_ref[...] = jnp.zeros(acc_ref.shape, F32)

    def scores(j):
        return _dot_nt(q, k_ref[pl.ds(pl.multiple_of(j * t, t), t), :])

    def update(s_ref, j, diagonal=False):
        s = s_ref[...]
        if diagonal:
            qc = lax.broadcasted_iota(jnp.int32, s.shape, 0) // CHUNK
            kc = lax.broadcasted_iota(jnp.int32, s.shape, 1) // CHUNK
            s = jnp.where(kc <= qc, s, NEG_BIG)
        m_old = m_ref[...]
        m_new = jnp.maximum(m_old, jnp.max(s, axis=-1, keepdims=True))
        p = jnp.exp2(s - jnp.tile(m_new, (1, t // LANES)))
        alpha = jnp.tile(jnp.exp2(m_old - m_new), (1, acc_ref.shape[1] // LANES))
        pv = _dot(p.astype(BF16), v_ref[pl.ds(pl.multiple_of(j * t, t), t), :])
        acc_ref[...] = alpha * acc_ref[...] + pv
        m_ref[...] = m_new

    sa_ref[...] = scores(0)

    def pair(jj, carry):
        j = 2 * jj
        sb_ref[...] = scores(j + 1)
        update(sa_ref, j)
        sa_ref[...] = scores(j + 2)
        update(sb_ref, j + 1)
        return carry

    lax.fori_loop(0, i // 2, pair, 0)

    @pl.when(i % 2 == 0)
    def _():
        update(sa_ref, i, diagonal=True)

    @pl.when(i % 2 == 1)
    def _():
        sb_ref[...] = scores(i)
        update(sa_ref, i - 1)
        update(sb_ref, i, diagonal=True)

    acc = acc_ref[...]
    o_ref[...] = (acc[:, :V_HEAD] / acc[:, V_HEAD:]).astype(BF16)


def _attn_prompt(q, k, v, lp, side=()):
    t = _tile(lp, 1024, LANES)
    nq = lp // t
    steps = MLA_HEADS * nq
    side2d = [a.reshape(-1, a.shape[-1]) for a in side]
    assert all(a.shape[0] % (steps * 16) == 0 for a in side2d)
    side_spec = lambda a: pl.BlockSpec((a.shape[0] // steps, a.shape[1]), lambda h, i: (h * nq + i, 0))
    out = pl.pallas_call(
        functools.partial(_flash_body, t=t, n_side=len(side)),
        grid=(MLA_HEADS, nq),
        in_specs=[
            pl.BlockSpec((None, t, 2 * LANES), lambda h, i: (h, i, 0)),
            pl.BlockSpec((None, lp, 2 * LANES), lambda h, i: (h, 0, 0)),
            pl.BlockSpec((None, lp, V_HEAD + LANES), lambda h, i: (h, 0, 0)),
        ] + [side_spec(a) for a in side2d],
        out_specs=[pl.BlockSpec((t, V_HEAD), lambda h, i: (i, h))] + [side_spec(a) for a in side2d],
        out_shape=[jax.ShapeDtypeStruct((lp, MLA_HEADS * V_HEAD), BF16)]
        + [jax.ShapeDtypeStruct(a.shape, BF16) for a in side2d],
        scratch_shapes=[pltpu.VMEM((t, t), F32), pltpu.VMEM((t, t), F32),
                        pltpu.VMEM((t, LANES), F32), pltpu.VMEM((t, V_HEAD + LANES), F32)],
        compiler_params=_params("arbitrary", "arbitrary"),
        name="attn_prompt",
    )(q, k, v, *side2d)
    return out[0], [o.reshape(a.shape) for o, a in zip(out[1:], side)]


def _attn_sample_body(q_ref, cc_ref, cn_ref, pct_ref, pn_ref, wk_ref, wv_ref, o_ref, c_sc, p_sc, *, past, s, lk):
    c_sc[:past] = cc_ref[...].astype(BF16)
    c_sc[past:past + s] = cn_ref[...].astype(BF16)
    if lk > past + s:
        c_sc[past + s:] = jnp.zeros((lk - past - s, c_sc.shape[1]), BF16)
    c = c_sc[...]
    valid = lax.broadcasted_iota(jnp.int32, (s, lk), 1) < past + s
    ones = jnp.ones((8, QK_NOPE), BF16)
    n_ch = 4 if lk % 64 == 0 else 1
    ch = lk // n_ch
    q_pe = jnp.concatenate([q_ref[hd][:, QK_NOPE:] for hd in range(MLA_HEADS)], axis=0)
    pct = pct_ref[...].astype(BF16)
    pn = pn_ref[...]
    kp_new = jnp.concatenate([(pn + pltpu.roll(pn, QK_ROPE, 1)).astype(BF16),
                              jnp.zeros((lk - past - s, LANES), BF16)], axis=0)
    pe = jnp.concatenate([_dot(q_pe, jnp.concatenate([pct, pct], axis=0)), _dot_nt(q_pe, kp_new)], axis=1)
    inv_l = []
    for pair in range(MLA_HEADS // 2):
        wk = wk_ref[:, pair * 2 * QK_NOPE:(pair + 1) * 2 * QK_NOPE]
        kk = jnp.concatenate([_dot(c[r * ch:(r + 1) * ch], wk) for r in range(n_ch)])
        for hd in (2 * pair, 2 * pair + 1):
            kn = kk[:, (hd % 2) * QK_NOPE:(hd % 2 + 1) * QK_NOPE]
            ms = _dot_nt(ones, (kn * kn).astype(BF16))[0:1] / QK_NOPE
            sc = _dot_nt(q_ref[hd][:, :QK_NOPE], kn.astype(BF16)) * lax.rsqrt(ms + EPS) + pe[hd * s:(hd + 1) * s]
            sc = jnp.where(valid, sc, NEG_BIG)
            p = jnp.exp2(sc - jnp.max(sc, axis=-1, keepdims=True))
            inv_l.append(1.0 / jnp.sum(p, axis=-1, keepdims=True))
            p_sc[hd * s:(hd + 1) * s] = p.astype(BF16)
    ctx = _dot(p_sc[...], c)
    for hd in range(MLA_HEADS):
        ctx_h = (ctx[hd * s:(hd + 1) * s] * inv_l[hd]).astype(BF16)
        o_ref[:, hd * V_HEAD:(hd + 1) * V_HEAD] = _dot(ctx_h, wv_ref[:, hd * V_HEAD:(hd + 1) * V_HEAD]).astype(BF16)


def _attn_sample(q, ckv_cache, ckv, kpe_cache, kpe, wk, wv, e, lp, b, s):
    past, kv_lora = ckv_cache.shape[2], ckv_cache.shape[3]
    lk = past + -(-s // LANES) * LANES
    row0 = lp // s
    return pl.pallas_call(
        functools.partial(_attn_sample_body, past=past, s=s, lk=lk),
        grid=(b,),
        in_specs=[
            pl.BlockSpec((MLA_HEADS, s, 2 * LANES), lambda i: (0, row0 + i, 0)),
            pl.BlockSpec((None, None, past, kv_lora), lambda i: (e, i, 0, 0)),
            pl.BlockSpec((s, kv_lora), lambda i: (row0 + i, 0)),
            pl.BlockSpec((None, None, QK_ROPE, past), lambda i: (e, i, 0, 0)),
            pl.BlockSpec((s, LANES), lambda i: (row0 + i, 0)),
            pl.BlockSpec((None,) + wk.shape[1:], lambda i: (e, 0, 0), pipeline_mode=pl.Buffered(1)),
            pl.BlockSpec((None,) + wv.shape[1:], lambda i: (e, 0, 0), pipeline_mode=pl.Buffered(1)),
        ],
        out_specs=pl.BlockSpec((s, MLA_HEADS * V_HEAD), lambda i: (i, 0)),
        out_shape=jax.ShapeDtypeStruct((b * s, MLA_HEADS * V_HEAD), BF16),
        scratch_shapes=[pltpu.VMEM((lk, kv_lora), BF16), pltpu.VMEM((MLA_HEADS * s, lk), BF16)],
        compiler_params=_params("parallel"),
        name="attn_sample",
    )(q, ckv_cache, ckv, kpe_cache, kpe, wk, wv)


def _pool_segment(ext_ref, pooled_ref, row0, n, pos0, gdim):
    pos = (lax.broadcasted_iota(jnp.int32, (n, 1), 0) + pos0).astype(F32)
    for gi, w in enumerate(POOL_WINDOWS):
        cols = slice(gi * gdim, (gi + 1) * gdim)
        u = ext_ref[HIST_ROWS:HIST_ROWS + n, cols]
        acc = u
        for k in range(1, w):
            acc = acc + ext_ref[HIST_ROWS - k:HIST_ROWS - k + n, cols]
        cnt = jnp.minimum(jnp.float32(w), pos + 1.0)
        pooled_ref[row0:row0 + n, cols] = (acc / cnt - u).astype(BF16)


def _pool_project(pooled_ref, pw_ref, sc_ref, o_ref, gdim):
    for gi in range(len(POOL_WINDOWS)):
        cols = slice(gi * gdim, (gi + 1) * gdim)
        o_ref[:, cols] = (_dot(pooled_ref[:, cols], pw_ref[gi]) * sc_ref[:, cols]).astype(BF16)


def _even_out_body(*refs, n_x, tm, n_prompt_tiles, s, past, pool_dim):
    x_refs, refs = refs[:n_x], refs[n_x:]
    (u_ref, hist_ref, ap_ref, as_ref, pw_ref, sc_ref, w_ref, o_ref, ext_ref, exts_ref, pooled_ref, pool_ref) = refs
    i = pl.program_id(0)
    gdim = pool_dim // len(POOL_WINDOWS)

    def project(a_ref, x_ref):
        _pool_project(pooled_ref, pw_ref, sc_ref, pool_ref, gdim)
        o_ref[...] = x_ref[...] + _dot(pool_ref[...], w_ref[:pool_dim]) + _dot(a_ref[...], w_ref[pool_dim:])

    @pl.when(i == 0)
    def _():
        ext_ref[:HIST_ROWS] = jnp.zeros((HIST_ROWS, pool_dim), F32)

    @pl.when(i < n_prompt_tiles)
    def _():
        ext_ref[HIST_ROWS:] = u_ref[...]
        _pool_segment(ext_ref, pooled_ref, 0, tm, i * tm, gdim)
        ext_ref[:HIST_ROWS] = ext_ref[tm:tm + HIST_ROWS]
        project(ap_ref, x_refs[0])

    @pl.when(i >= n_prompt_tiles)
    def _():
        for bi in range(tm // s):
            ext = exts_ref.at[bi]
            ext[0:1] = jnp.zeros((1, pool_dim), F32)
            ext[1:HIST_ROWS] = hist_ref[bi]
            ext[HIST_ROWS:] = u_ref[bi * s:(bi + 1) * s]
            _pool_segment(ext, pooled_ref, bi * s, s, past, gdim)
        project(as_ref, x_refs[-1])


def _even_out(x, u, state_pool, attn_p, attn_s, pw, scale, wout, e, lp, s, past):
    m, pool_dim = u.shape
    attn_dim = attn_p.shape[1]
    ng = len(POOL_WINDOWS)
    tm = _tile(math.gcd(lp, m - lp), 512, math.lcm(HIST_ROWS, s))
    npt = lp // tm
    xs, x_specs, _ = _row_specs(x, tm)
    d = xs[0].shape[1]
    return pl.pallas_call(
        functools.partial(_even_out_body, n_x=len(xs), tm=tm, n_prompt_tiles=npt, s=s, past=past,
                          pool_dim=pool_dim),
        grid=(m // tm,),
        in_specs=x_specs + [
            pl.BlockSpec((tm, pool_dim), lambda i: (i, 0)),
            pl.BlockSpec((None, tm // s, POOL_HIST, pool_dim), lambda i: (e, jnp.maximum(i - npt, 0), 0, 0)),
            pl.BlockSpec((tm, attn_dim), lambda i: (jnp.minimum(i, npt - 1), 0)),
            pl.BlockSpec((tm, attn_dim), lambda i: (jnp.maximum(i - npt, 0), 0)),
            pl.BlockSpec((None, ng, pool_dim // ng, pool_dim // ng), lambda i: (e, 0, 0, 0)),
            pl.BlockSpec((None, 1, pool_dim), lambda i: (e, 0, 0)),
            pl.BlockSpec((None,) + wout.shape[1:], lambda i: (e, 0, 0), pipeline_mode=pl.Buffered(1)),
        ],
        out_specs=pl.BlockSpec((tm, d), lambda i: (i, 0)),
        out_shape=jax.ShapeDtypeStruct((m, d), F32),
        scratch_shapes=[pltpu.VMEM((HIST_ROWS + tm, pool_dim), F32),
                        pltpu.VMEM((tm // s, HIST_ROWS + s, pool_dim), F32),
                        pltpu.VMEM((tm, pool_dim), BF16), pltpu.VMEM((tm, pool_dim), BF16)],
        compiler_params=_params("arbitrary"),
        name="even_out",
    )(*xs, u, state_pool, attn_p, attn_s, pw, scale, wout)


def _odd_in_body(x_ref, g_ref, w_ref, vn_ref, u_ref, v_ref, vf_ref, *, gate):
    h = _rms(x_ref[...], g_ref[...]).astype(BF16)
    v = _rms(jax.nn.gelu(_dot(h, w_ref[:, gate:])), vn_ref[...])
    v_ref[...] = v.astype(BF16)
    vf_ref[...] = v
    n_chunks = 4 if gate % (4 * LANES) == 0 else 1
    cw = gate // n_chunks
    for c in range(n_chunks):
        u_ref[:, c * cw:(c + 1) * cw] = jax.nn.gelu(_dot(h, w_ref[:, c * cw:(c + 1) * cw])).astype(BF16)


def _odd_in(x, g, layer, win, vn, o, lp):
    m, d = x.shape
    gate = win.shape[-1] // 2
    tm = _tile(math.gcd(lp, m - lp), 512)
    first = lp // tm
    return pl.pallas_call(
        functools.partial(_odd_in_body, gate=gate),
        grid=(m // tm,),
        in_specs=[
            pl.BlockSpec((tm, d), lambda i: (i, 0)),
            pl.BlockSpec((None, 1, d), lambda i: (layer, 0, 0)),
            pl.BlockSpec((None, d, 2 * gate), lambda i: (o, 0, 0), pipeline_mode=pl.Buffered(1)),
            pl.BlockSpec((None, 1, gate), lambda i: (o, 0, 0)),
        ],
        out_specs=[
            pl.BlockSpec((tm, gate), lambda i: (i, 0)),
            pl.BlockSpec((tm, gate), lambda i: (i, 0)),
            pl.BlockSpec((tm, gate), lambda i: (jnp.maximum(i - first, 0), 0)),
        ],
        out_shape=[
            jax.ShapeDtypeStruct((m, gate), BF16),
            jax.ShapeDtypeStruct((m, gate), BF16),
            jax.ShapeDtypeStruct((m - lp, gate), F32),
        ],
        compiler_params=_params("arbitrary"),
        name="odd_in",
    )(x, g, win, vn)


def _odd_out_body(x_ref, u_ref, v_ref, ws_ref, b_ref, w_ref, o_ref, us_ref, *, tm, n_prompt_tiles, s):
    is_prompt = pl.program_id(0) < n_prompt_tiles
    ii = lax.broadcasted_iota(jnp.int32, (GMLP_CHUNK, GMLP_CHUNK), 0)
    jj = lax.broadcasted_iota(jnp.int32, (GMLP_CHUNK, GMLP_CHUNK), 1)
    same_stream = jnp.where((ii // s) == (jj // s), 1, 0) + jnp.where(is_prompt, 1, 0)
    keep = (jj <= ii) & (same_stream > 0)
    gdim = u_ref.shape[1] // GMLP_GROUPS
    for gi in range(GMLP_GROUPS):
        cols = slice(gi * gdim, (gi + 1) * gdim)
        wmat = jnp.where(keep, ws_ref[gi], 0.0).astype(BF16)
        bias = b_ref[:, gi:gi + 1]
        for ci in range(tm // GMLP_CHUNK):
            rows = slice(ci * GMLP_CHUNK, (ci + 1) * GMLP_CHUNK)
            sg = _dot(wmat, v_ref[rows, cols]) + bias
            us_ref[rows, cols] = (u_ref[rows, cols].astype(F32) * sg).astype(BF16)
    o_ref[...] = x_ref[...] + _dot(us_ref[...], w_ref[...])


def _odd_out(x, u, v, ws_sel, b_sel, wout, o, lp, s):
    m, d = x.shape
    gate = u.shape[1]
    tm = _tile(math.gcd(lp, m - lp), 512, GMLP_CHUNK)
    npt = lp // tm
    sel = lambda i: jnp.where(i >= npt, 1, 0)
    return pl.pallas_call(
        functools.partial(_odd_out_body, tm=tm, n_prompt_tiles=npt, s=s),
        grid=(m // tm,),
        in_specs=[
            pl.BlockSpec((tm, d), lambda i: (i, 0)),
            pl.BlockSpec((tm, gate), lambda i: (i, 0)),
            pl.BlockSpec((tm, gate), lambda i: (i, 0)),
            pl.BlockSpec((None, None, GMLP_GROUPS, GMLP_CHUNK, GMLP_CHUNK), lambda i: (sel(i), o, 0, 0, 0)),
            pl.BlockSpec((None, None, GMLP_CHUNK, GMLP_GROUPS), lambda i: (sel(i), o, 0, 0)),
            pl.BlockSpec((None,) + wout.shape[1:], lambda i: (o, 0, 0), pipeline_mode=pl.Buffered(1)),
        ],
        out_specs=pl.BlockSpec((tm, d), lambda i: (i, 0)),
        out_shape=jax.ShapeDtypeStruct((m, d), F32),
        scratch_shapes=[pltpu.VMEM((tm, gate), BF16)],
        compiler_params=_params("parallel"),
        name="odd_out",
    )(x, u, v, ws_sel, b_sel, wout)


def kernel(x_prompt, x_sample, cache_mla_ckv, cache_mla_kpe, state_pool, norm_ffn1, norm_mix, norm_ffn2, ffn1_w_gate, ffn1_w_up, ffn1_w_down, ffn2_w_gate, ffn2_w_up, ffn2_w_down, ev_w_in, ev_q_a_norm, ev_kv_a_norm, ev_w_qb, ev_w_kvb, ev_q_nope_norm, ev_q_pe_norm, ev_k_nope_norm, ev_k_pe_norm, ev_pool_w, ev_pool_scale, ev_w_out, od_w_in, od_v_norm, od_w_s, od_b_s, od_w_out):
    bp, lp, d = x_prompt.shape
    b, s, _ = x_sample.shape
    depth = norm_mix.shape[0]
    n_even, n_odd = ev_w_in.shape[0], od_w_in.shape[0]
    past = cache_mla_ckv.shape[2]
    q_lora, kv_lora = ev_q_a_norm.shape[1], ev_kv_a_norm.shape[1]
    pool_dim = ev_pool_scale.shape[1]
    gate = od_v_norm.shape[1]
    m = lp + b * s
    assert bp == 1 and lp % GMLP_CHUNK == 0 and (b * s) % GMLP_CHUNK == 0 and GMLP_CHUNK % s == 0
    assert past % CHUNK == 0 and s <= CHUNK, "every cached and new key must be visible to every sample query"
    assert s >= POOL_HIST and lp >= POOL_HIST and lp % s == 0
    assert ev_w_in.shape[2] == pool_dim + q_lora + kv_lora + QK_ROPE

    vec = lambda a: a[:, None, :]
    bf = lambda a: a.astype(BF16)
    pad_lanes = lambda a: jnp.pad(a, [(0, 0)] * (a.ndim - 1) + [(0, (-a.shape[-1]) % LANES)])

    w_ffn = ((ffn1_w_gate, ffn1_w_up, ffn1_w_down), (ffn2_w_gate, ffn2_w_up, ffn2_w_down))
    g_ffn = (vec(norm_ffn1), vec(norm_ffn2))
    g_mix = vec(norm_mix)
    wq = ev_w_qb.reshape(n_even, q_lora, MLA_HEADS, QK_HEAD)
    w_qb = bf(jnp.concatenate([wq[..., :QK_NOPE].reshape(n_even, q_lora, -1),
                               wq[..., QK_NOPE:].reshape(n_even, q_lora, -1)], axis=-1))
    w_kvb, pool_w = bf(ev_w_kvb), bf(ev_pool_w)
    wkv = w_kvb.reshape(n_even, kv_lora, MLA_HEADS, QK_NOPE + V_HEAD)
    w_k = wkv[..., :QK_NOPE].reshape(n_even, kv_lora, -1)
    w_v = wkv[..., QK_NOPE:].reshape(n_even, kv_lora, -1)
    qpn = vec(jnp.tile(ev_q_pe_norm, (1, LANES // QK_ROPE)))
    kpn = vec(pad_lanes(ev_k_pe_norm))
    kpe_cache_t = jnp.swapaxes(cache_mla_kpe, 2, 3)
    rep = GMLP_CHUNK // s
    ws_sel = jnp.stack([od_w_s, jnp.tile(od_w_s[:, :, :s, :s], (1, 1, rep, rep))])
    b_sel = jnp.stack([od_b_s, jnp.tile(od_b_s[:, :, :s], (1, 1, rep))]).transpose(0, 1, 3, 2)

    cos, sin, w_in_e = _rope_tables(m, lp, past, s, ev_w_in)
    new_ckv, new_kpe, new_u, new_v = [], [], [], []
    w_cur = tuple(bf(w[0]) for w in w_ffn[0])
    for layer in range(depth):
        if layer == 0:
            x_s, _ = _ffn(x_sample.reshape(b * s, d), g_ffn[0], layer, w_cur)
            x_p, w_cur = _ffn(x_prompt.reshape(lp, d), g_ffn[0], layer, w_cur, w_ffn[1], layer)
            x = (x_p, x_s)
        else:
            x, w_cur = _ffn(x, g_ffn[0], layer, w_cur, w_ffn[1], layer)
        if layer % 2 == 0:
            e = layer // 2
            u, q, ckv, kpe = _even_in(x, g_mix, layer, w_in_e, vec(ev_q_a_norm), vec(ev_kv_a_norm), w_qb,
                                      vec(ev_q_nope_norm), vec(ev_k_nope_norm), qpn, kpn, cos, sin, e,
                                      pool_dim, q_lora, kv_lora)
            k_p, v_p = _kv_prompt(ckv, kpe, w_kvb, e, lp)
            if e == 0:
                attn_p, (w_out_e, w_in_o, w_out_o) = _attn_prompt(q, k_p, v_p, lp, (ev_w_out, od_w_in, od_w_out))
            else:
                attn_p, _ = _attn_prompt(q, k_p, v_p, lp)
            attn_s = _attn_sample(q, cache_mla_ckv, ckv, kpe_cache_t, kpe, w_k, w_v, e, lp, b, s)
            x = _even_out(x, u, state_pool, attn_p, attn_s, pool_w, vec(ev_pool_scale), w_out_e, e, lp, s, past)
            new_ckv.append((ckv[:lp], ckv[lp:]))
            new_kpe.append((kpe[:lp, :QK_ROPE], kpe[lp:, :QK_ROPE]))
            new_u.append((u[lp - POOL_HIST:lp], u[lp:].reshape(b, s, pool_dim)[:, s - POOL_HIST:]))
        else:
            o = layer // 2
            u, v, v_f32 = _odd_in(x, g_mix, layer, w_in_o, vec(od_v_norm), o, lp)
            x = _odd_out(x, u, v, ws_sel, b_sel, w_out_o, o, lp, s)
            new_v.append(v_f32)
        if layer + 1 < depth:
            x, w_cur = _ffn(x, g_ffn[1], layer, w_cur, w_ffn[0], layer + 1)
        else:
            y_p, (y_s,) = _ffn(x, g_ffn[1], layer, w_cur, split_rows=lp)

    stack = lambda pairs, k, shape: jnp.stack([p[k] for p in pairs]).reshape((n_even,) + shape)
    return (
        y_p.reshape(1, lp, d),
        y_s.reshape(b, s, d),
        stack(new_ckv, 0, (1, lp, kv_lora)),
        stack(new_kpe, 0, (1, lp, QK_ROPE)),
        stack(new_u, 0, (1, POOL_HIST, pool_dim)),
        stack(new_ckv, 1, (b, s, kv_lora)),
        stack(new_kpe, 1, (b, s, QK_ROPE)),
        stack(new_u, 1, (b, POOL_HIST, pool_dim)),
        jnp.stack(new_v).reshape(n_odd, b, s, gate),
    )
```

```python
import functools
import math

import jax
import jax.numpy as jnp
from jax import lax
from jax.experimental import pallas as pl
from jax.experimental.pallas import tpu as pltpu

F32 = jnp.float32
BF16 = jnp.bfloat16

EPS = 1e-6
CHUNK = 64
POOL_WINDOWS = (2, 4, 8, 16)
POOL_HIST = max(POOL_WINDOWS) - 1
HIST_ROWS = 16
MLA_HEADS = 8
QK_NOPE = 128
QK_ROPE = 64
V_HEAD = 128
QK_HEAD = QK_NOPE + QK_ROPE
ATTN_SCALE = QK_HEAD ** -0.5
Q_SCALE = ATTN_SCALE * math.log2(math.e)
ROPE_THETA = 10000.0
GMLP_CHUNK = 128
GMLP_GROUPS = 8
LANES = 128
NEG_BIG = -1e30
VMEM_LIMIT = 60 * 1024 * 1024


def _params(*sem):
    return pltpu.CompilerParams(dimension_semantics=sem, vmem_limit_bytes=VMEM_LIMIT)


def _tile(n, pref, mult=8):
    if n <= pref:
        return n
    for t in range(pref - pref % mult, 0, -mult):
        if n % t == 0:
            return t
    raise ValueError(f"no tile for {n}")


def _const_spec(shape):
    nd = len(shape)
    return pl.BlockSpec(shape, lambda *_: (0,) * nd, pipeline_mode=pl.Buffered(1))


def _rms(x, g):
    ms = jnp.mean(x * x, axis=-1, keepdims=True)
    return x * lax.rsqrt(ms + EPS) * g


def _row_specs(x, tm):
    if not isinstance(x, tuple):
        return (x,), [pl.BlockSpec((tm, x.shape[1]), lambda i: (i, 0))], None
    npt = x[0].shape[0] // tm
    d = x[0].shape[1]
    return x, [pl.BlockSpec((tm, d), lambda i: (jnp.minimum(i, npt - 1), 0)),
               pl.BlockSpec((tm, d), lambda i: (jnp.maximum(i - npt, 0), 0))], npt


def _read_rows(x_refs, n_prompt_tiles):
    if len(x_refs) == 1:
        return x_refs[0][...]
    return jnp.where(pl.program_id(0) < n_prompt_tiles, x_refs[0][...], x_refs[1][...])


def _dot(a, b):
    return jnp.dot(a, b, preferred_element_type=F32)


def _dot_nt(a, b):
    return lax.dot_general(a, b, (((1,), (1,)), ((), ())), preferred_element_type=F32)


def _ffn_body(x_ref, g_ref, wg_ref, wu_ref, wd_ref, *rest, convert_next):
    *rest, h_ref = rest
    if convert_next:
        (*nxt, o_ref, cg_ref, cu_ref, cd_ref) = rest

        @pl.when(pl.program_id(0) < convert_next)
        def _():
            for src, dst in zip(nxt, (cg_ref, cu_ref, cd_ref)):
                dst[...] = src[...].astype(BF16)
    else:
        (o_ref,) = rest

    @pl.when(pl.program_id(1) == 0)
    def _():
        x = x_ref[...]
        h_ref[...] = _rms(x, g_ref[...]).astype(BF16)
        o_ref[...] = x

    h = h_ref[...]
    a = _dot(h, wg_ref[...])
    b = _dot(h, wu_ref[...])
    act = (a * jax.nn.sigmoid(a) * b * 0.5).astype(BF16)
    o_ref[...] += _dot(act, wd_ref[...])


def _ffn(x, g, layer, w, w_next=None, next_layer=None, rows=None):
    row0, m = (0, x.shape[0]) if rows is None else rows
    d = x.shape[1]
    f = w[0].shape[-1]
    tm = _tile(math.gcd(row0, m), 1024)
    tf = _tile(f, 512, LANES)
    n_i, t0 = m // tm, row0 // tm
    in_specs = [
        pl.BlockSpec((tm, d), lambda i, j: (t0 + i, 0)),
        pl.BlockSpec((None, 1, d), lambda i, j: (layer, 0, 0)),
        pl.BlockSpec((d, tf), lambda i, j: (0, j)),
        pl.BlockSpec((d, tf), lambda i, j: (0, j)),
        pl.BlockSpec((tf, d), lambda i, j: (j, 0)),
    ]
    out_specs = [pl.BlockSpec((tm, d), lambda i, j: (i, 0))]
    out_shape = [jax.ShapeDtypeStruct((m, d), F32)]
    nr = 0
    if w_next is not None:
        nr = max(r for r in (1, 2, 4, 8, 16) if r <= n_i and d % (r * LANES) == 0)
        db = d // nr
        n_j = f // tf
        blk = lambda i: jnp.minimum(i, nr - 1)
        col = lambda i, j: jnp.where(i < nr, j, n_j - 1)
        in_specs += [
            pl.BlockSpec((None, db, tf), lambda i, j: (next_layer, blk(i), col(i, j))),
            pl.BlockSpec((None, db, tf), lambda i, j: (next_layer, blk(i), col(i, j))),
            pl.BlockSpec((None, tf, db), lambda i, j: (next_layer, col(i, j), blk(i))),
        ]
        out_specs += [
            pl.BlockSpec((db, tf), lambda i, j: (blk(i), col(i, j))),
            pl.BlockSpec((db, tf), lambda i, j: (blk(i), col(i, j))),
            pl.BlockSpec((tf, db), lambda i, j: (col(i, j), blk(i))),
        ]
        out_shape += [jax.ShapeDtypeStruct((d, f), BF16), jax.ShapeDtypeStruct((d, f), BF16),
                      jax.ShapeDtypeStruct((f, d), BF16)]
    out = pl.pallas_call(
        functools.partial(_ffn_body, convert_next=nr),
        grid=(n_i, f // tf),
        in_specs=in_specs,
        out_specs=out_specs,
        out_shape=out_shape,
        scratch_shapes=[pltpu.VMEM((tm, d), BF16)],
        compiler_params=_params("arbitrary", "arbitrary"),
        name="ffn",
    )(x, g, *w, *(w_next or ()))
    return out[0], tuple(out[1:])


def _rope_tab_body(inv_ref, sgn_ref, win_ref, cos_ref, sin_ref, wout_ref, *, tm, lp, past, s, n_blocks):
    r = lax.broadcasted_iota(jnp.int32, (tm, LANES), 0) + pl.program_id(0) * tm
    t = r - lp
    t = (t & (s - 1)) if s & (s - 1) == 0 else lax.rem(t, s)
    pos = jnp.where(r >= lp, past + t, r).astype(F32)
    ang = pos * inv_ref[...]
    cos_ref[...] = jnp.cos(ang)
    sin_ref[...] = jnp.sin(ang) * sgn_ref[...]

    @pl.when(pl.program_id(0) < n_blocks)
    def _():
        n_raw = win_ref.shape[1]
        wout_ref[:, :n_raw] = win_ref[...].astype(BF16)
        wout_ref[:, n_raw:] = jnp.zeros((wout_ref.shape[0], wout_ref.shape[1] - n_raw), BF16)


def _rope_tables(m, lp, past, s, w_in):
    half = QK_ROPE // 2
    inv = ROPE_THETA ** (-jnp.arange(0, QK_ROPE, 2, dtype=F32) / QK_ROPE)
    inv = jnp.tile(inv, LANES // half)[None, :]
    sgn = jnp.tile(jnp.concatenate([-jnp.ones((half,), F32), jnp.ones((half,), F32)]), LANES // QK_ROPE)[None, :]
    tm = _tile(m, 512)
    steps = m // tm
    w2d = w_in.reshape(-1, w_in.shape[-1])
    rows, n_raw = w2d.shape
    n_pad = n_raw + (-n_raw) % LANES
    nb = max(r for r in (1, 2, 4, 8, 16, 32, 64) if r <= steps and rows % (r * 16) == 0)
    blk = lambda i: (jnp.minimum(i, nb - 1), 0)
    cos, sin, w_bf = pl.pallas_call(
        functools.partial(_rope_tab_body, tm=tm, lp=lp, past=past, s=s, n_blocks=nb),
        grid=(steps,),
        in_specs=[_const_spec((1, LANES)), _const_spec((1, LANES)), pl.BlockSpec((rows // nb, n_raw), blk)],
        out_specs=[pl.BlockSpec((tm, LANES), lambda i: (i, 0))] * 2 + [pl.BlockSpec((rows // nb, n_pad), blk)],
        out_shape=[jax.ShapeDtypeStruct((m, LANES), F32)] * 2 + [jax.ShapeDtypeStruct((rows, n_pad), BF16)],
        compiler_params=_params("arbitrary"),
        name="rope_tables",
    )(inv, sgn, w2d)
    return cos, sin, w_bf.reshape(w_in.shape[:-1] + (n_pad,))


def _swap_halves(x):
    lane = lax.broadcasted_iota(jnp.int32, x.shape, 1)
    left = pltpu.roll(x, LANES - QK_ROPE // 2, 1)
    right = pltpu.roll(x, QK_ROPE // 2, 1)
    return jnp.where((lane & (QK_ROPE - 1)) < QK_ROPE // 2, left, right)


def _even_in_body(*refs, n_x, n_prompt_tiles, pool_dim, q_lora, kv_lora):
    x_refs, refs = refs[:n_x], refs[n_x:]
    (g_ref, win_ref, qan_ref, kvan_ref, wqb_ref, qnn_ref, knn_ref, qpn_ref, kpn_ref, cos_ref, sin_ref,
     u_ref, q_ref, ckv_ref, kpe_ref) = refs
    h = _rms(_read_rows(x_refs, n_prompt_tiles), g_ref[...]).astype(BF16)
    o1, o2 = pool_dim + q_lora, pool_dim + q_lora + kv_lora
    cos, sin = cos_ref[...], sin_ref[...]
    lane = lax.broadcasted_iota(jnp.int32, cos.shape, 1)
    lo = lane < QK_ROPE

    def rope(t):
        return t * cos + _swap_halves(t) * sin

    qn = _rms(_dot(h, win_ref[:, pool_dim:o1]), qan_ref[...]).astype(BF16)
    q = _dot(qn, wqb_ref[...])
    zk = _dot(h, win_ref[:, o1:])
    ckv_ref[...] = _rms(zk[:, :kv_lora], kvan_ref[...])
    kp = zk[:, kv_lora:]
    kp = kp * lax.rsqrt(jnp.sum(kp * kp, axis=-1, keepdims=True) / QK_ROPE + EPS) * kpn_ref[...]
    kpe_ref[...] = rope(kp)
    u_ref[...] = _dot(h, win_ref[:, :pool_dim])

    nope_w = MLA_HEADS * QK_NOPE
    k_gain = knn_ref[...] * Q_SCALE
    for hd in range(MLA_HEADS):
        qh = _rms(q[:, hd * QK_NOPE:(hd + 1) * QK_NOPE], qnn_ref[...]) * k_gain
        q_ref[hd, :, :QK_NOPE] = qh.astype(BF16)
    for j in range(MLA_HEADS // 2):
        t = q[:, nope_w + j * LANES:nope_w + (j + 1) * LANES]
        tt = t * t
        s_lo = jnp.sum(jnp.where(lo, tt, 0.0), axis=-1, keepdims=True)
        s_hi = jnp.sum(jnp.where(lo, 0.0, tt), axis=-1, keepdims=True)
        inv = jnp.where(lo, lax.rsqrt(s_lo / QK_ROPE + EPS), lax.rsqrt(s_hi / QK_ROPE + EPS))
        t = rope(t * inv * qpn_ref[...]) * Q_SCALE
        q_ref[2 * j, :, QK_NOPE:] = jnp.where(lo, t, 0.0).astype(BF16)
        q_ref[2 * j + 1, :, QK_NOPE:] = jnp.where(lo, 0.0, t).astype(BF16)


def _even_in(x, g, layer, win, qan, kvan, wqb, qnn, knn, qpn, kpn, cos, sin, e, pool_dim, q_lora, kv_lora):
    rows = tuple(a.shape[0] for a in x) if isinstance(x, tuple) else (x.shape[0],)
    m, d = sum(rows), (x[0] if isinstance(x, tuple) else x).shape[1]
    tm = _tile(math.gcd(*rows), 512)
    xs, x_specs, npt = _row_specs(x, tm)
    n_in = win.shape[-1]
    n_q = wqb.shape[-1]
    row = lambda w: pl.BlockSpec((tm, w), lambda i: (i, 0))
    vec = lambda w, idx: pl.BlockSpec((None, 1, w), lambda i: (idx, 0, 0))
    return pl.pallas_call(
        functools.partial(_even_in_body, n_x=len(xs), n_prompt_tiles=npt, pool_dim=pool_dim, q_lora=q_lora,
                          kv_lora=kv_lora),
        grid=(m // tm,),
        in_specs=x_specs + [
            vec(d, layer),
            pl.BlockSpec((None, d, n_in), lambda i: (e, 0, 0), pipeline_mode=pl.Buffered(1)),
            vec(q_lora, e), vec(kv_lora, e),
            pl.BlockSpec((None, q_lora, n_q), lambda i: (e, 0, 0), pipeline_mode=pl.Buffered(1)),
            vec(QK_NOPE, e), vec(QK_NOPE, e), vec(LANES, e), vec(LANES, e),
            row(LANES), row(LANES),
        ],
        out_specs=[
            row(pool_dim),
            pl.BlockSpec((MLA_HEADS, tm, 2 * LANES), lambda i: (0, i, 0)),
            row(kv_lora), row(LANES),
        ],
        out_shape=[
            jax.ShapeDtypeStruct((m, pool_dim), F32),
            jax.ShapeDtypeStruct((MLA_HEADS, m, 2 * LANES), BF16),
            jax.ShapeDtypeStruct((m, kv_lora), F32),
            jax.ShapeDtypeStruct((m, LANES), F32),
        ],
        compiler_params=_params("arbitrary"),
        name="even_in",
    )(*xs, g, win, qan, kvan, wqb, qnn, knn, qpn, kpn, cos, sin)


def _kv_body(ckv_ref, kpe_ref, w_ref, k_ref, v_ref):
    c = ckv_ref[...].astype(BF16)
    kp = kpe_ref[...]
    kp_both = (kp + pltpu.roll(kp, QK_ROPE, 1)).astype(BF16)
    hw = QK_NOPE + V_HEAD
    for hd in range(MLA_HEADS):
        kv = _dot(c, w_ref[:, hd * hw:(hd + 1) * hw])
        k_ref[hd, :, :QK_NOPE] = _rms(kv[:, :QK_NOPE], 1.0).astype(BF16)
        k_ref[hd, :, QK_NOPE:] = kp_both
        v_ref[hd, :, :V_HEAD] = kv[:, QK_NOPE:].astype(BF16)
        v_ref[hd, :, V_HEAD:] = jnp.ones((c.shape[0], LANES), BF16)


def _kv_prompt(ckv, kpe, wkvb, e, lp):
    kv_lora = ckv.shape[-1]
    tm = _tile(lp, 512)
    return pl.pallas_call(
        _kv_body,
        grid=(lp // tm,),
        in_specs=[
            pl.BlockSpec((tm, kv_lora), lambda i: (i, 0)),
            pl.BlockSpec((tm, LANES), lambda i: (i, 0)),
            pl.BlockSpec((None,) + wkvb.shape[1:], lambda i: (e, 0, 0), pipeline_mode=pl.Buffered(1)),
        ],
        out_specs=[
            pl.BlockSpec((MLA_HEADS, tm, 2 * LANES), lambda i: (0, i, 0)),
            pl.BlockSpec((MLA_HEADS, tm, V_HEAD + LANES), lambda i: (0, i, 0)),
        ],
        out_shape=[
            jax.ShapeDtypeStruct((MLA_HEADS, lp, 2 * LANES), BF16),
            jax.ShapeDtypeStruct((MLA_HEADS, lp, V_HEAD + LANES), BF16),
        ],
        compiler_params=_params("parallel"),
        name="kv_prompt",
    )(ckv, kpe, wkvb)


def _flash_body(q_ref, k_ref, v_ref, *rest, t, n_side):
    side_in, (o_ref, *side_out), (sa_ref, sb_ref, m_ref, acc_ref) = rest[:n_side], rest[n_side:-4], rest[-4:]
    for src, dst in zip(side_in, side_out):
        dst[...] = src[...].astype(BF16)
    i = pl.program_id(1)
    q = q_ref[...]
    m_ref[...] = jnp.full(m_ref.shape, NEG_BIG, F32)
    acc_ref[...] = jnp.zeros(acc_ref.shape, F32)

    def scores(j):
        return _dot_nt(q, k_ref[pl.ds(pl.multiple_of(j * t, t), t), :])

    def update(s_ref, j, diagonal=False):
        s = s_ref[...]
        if diagonal:
            qc = lax.broadcasted_iota(jnp.int32, s.shape, 0) // CHUNK
            kc = lax.broadcasted_iota(jnp.int32, s.shape, 1) // CHUNK
            s = jnp.where(kc <= qc, s, NEG_BIG)
        m_old = m_ref[...]
        m_new = jnp.maximum(m_old, jnp.max(s, axis=-1, keepdims=True))
        p = jnp.exp2(s - jnp.tile(m_new, (1, t // LANES)))
        alpha = jnp.tile(jnp.exp2(m_old - m_new), (1, acc_ref.shape[1] // LANES))
        pv = _dot(p.astype(BF16), v_ref[pl.ds(pl.multiple_of(j * t, t), t), :])
        acc_ref[...] = alpha * acc_ref[...] + pv
        m_ref[...] = m_new

    sa_ref[...] = scores(0)

    def pair(jj, carry):
        j = 2 * jj
        sb_ref[...] = scores(j + 1)
        update(sa_ref, j)
        sa_ref[...] = scores(j + 2)
        update(sb_ref, j + 1)
        return carry

    lax.fori_loop(0, i // 2, pair, 0)

    @pl.when(i % 2 == 0)
    def _():
        update(sa_ref, i, diagonal=True)

    @pl.when(i % 2 == 1)
    def _():
        sb_ref[...] = scores(i)
        update(sa_ref, i - 1)
        update(sb_ref, i, diagonal=True)

    acc = acc_ref[...]
    o_ref[...] = (acc[:, :V_HEAD] / acc[:, V_HEAD:]).astype(BF16)


def _attn_prompt(q, k, v, lp, side=()):
    t = _tile(lp, 1024, LANES)
    nq = lp // t
    steps = MLA_HEADS * nq
    side2d = [a.reshape(-1, a.shape[-1]) for a in side]
    assert all(a.shape[0] % (steps * 16) == 0 for a in side2d)
    side_spec = lambda a: pl.BlockSpec((a.shape[0] // steps, a.shape[1]), lambda h, i: (h * nq + i, 0))
    out = pl.pallas_call(
        functools.partial(_flash_body, t=t, n_side=len(side)),
        grid=(MLA_HEADS, nq),
        in_specs=[
            pl.BlockSpec((None, t, 2 * LANES), lambda h, i: (h, i, 0)),
            pl.BlockSpec((None, lp, 2 * LANES), lambda h, i: (h, 0, 0)),
            pl.BlockSpec((None, lp, V_HEAD + LANES), lambda h, i: (h, 0, 0)),
        ] + [side_spec(a) for a in side2d],
        out_specs=[pl.BlockSpec((t, V_HEAD), lambda h, i: (i, h))] + [side_spec(a) for a in side2d],
        out_shape=[jax.ShapeDtypeStruct((lp, MLA_HEADS * V_HEAD), BF16)]
        + [jax.ShapeDtypeStruct(a.shape, BF16) for a in side2d],
        scratch_shapes=[pltpu.VMEM((t, t), F32), pltpu.VMEM((t, t), F32),
                        pltpu.VMEM((t, LANES), F32), pltpu.VMEM((t, V_HEAD + LANES), F32)],
        compiler_params=_params("arbitrary", "arbitrary"),
        name="attn_prompt",
    )(q, k, v, *side2d)
    return out[0], [o.reshape(a.shape) for o, a in zip(out[1:], side)]


def _attn_sample_body(q_ref, cc_ref, cn_ref, pct_ref, pn_ref, wk_ref, wv_ref, o_ref, c_sc, p_sc, *, past, s, lk):
    c_sc[:past] = cc_ref[...].astype(BF16)
    c_sc[past:past + s] = cn_ref[...].astype(BF16)
    if lk > past + s:
        c_sc[past + s:] = jnp.zeros((lk - past - s, c_sc.shape[1]), BF16)
    c = c_sc[...]
    valid = lax.broadcasted_iota(jnp.int32, (s, lk), 1) < past + s
    ones = jnp.ones((8, QK_NOPE), BF16)
    n_ch = 4 if lk % 64 == 0 else 1
    ch = lk // n_ch
    q_pe = jnp.concatenate([q_ref[hd][:, QK_NOPE:] for hd in range(MLA_HEADS)], axis=0)
    pct = pct_ref[...].astype(BF16)
    pn = pn_ref[...]
    kp_new = jnp.concatenate([(pn + pltpu.roll(pn, QK_ROPE, 1)).astype(BF16),
                              jnp.zeros((lk - past - s, LANES), BF16)], axis=0)
    pe = jnp.concatenate([_dot(q_pe, jnp.concatenate([pct, pct], axis=0)), _dot_nt(q_pe, kp_new)], axis=1)
    inv_l = []
    for pair in range(MLA_HEADS // 2):
        wk = wk_ref[:, pair * 2 * QK_NOPE:(pair + 1) * 2 * QK_NOPE]
        kk = jnp.concatenate([_dot(c[r * ch:(r + 1) * ch], wk) for r in range(n_ch)])
        for hd in (2 * pair, 2 * pair + 1):
            kn = kk[:, (hd % 2) * QK_NOPE:(hd % 2 + 1) * QK_NOPE]
            ms = _dot_nt(ones, (kn * kn).astype(BF16))[0:1] / QK_NOPE
            sc = _dot_nt(q_ref[hd][:, :QK_NOPE], kn.astype(BF16)) * lax.rsqrt(ms + EPS) + pe[hd * s:(hd + 1) * s]
            sc = jnp.where(valid, sc, NEG_BIG)
            p = jnp.exp2(sc - jnp.max(sc, axis=-1, keepdims=True))
            inv_l.append(1.0 / jnp.sum(p, axis=-1, keepdims=True))
            p_sc[hd * s:(hd + 1) * s] = p.astype(BF16)
    ctx = _dot(p_sc[...], c)
    for hd in range(MLA_HEADS):
        ctx_h = (ctx[hd * s:(hd + 1) * s] * inv_l[hd]).astype(BF16)
        o_ref[:, hd * V_HEAD:(hd + 1) * V_HEAD] = _dot(ctx_h, wv_ref[:, hd * V_HEAD:(hd + 1) * V_HEAD]).astype(BF16)


def _attn_sample(q, ckv_cache, ckv, kpe_cache, kpe, wk, wv, e, lp, b, s):
    past, kv_lora = ckv_cache.shape[2], ckv_cache.shape[3]
    lk = past + -(-s // LANES) * LANES
    row0 = lp // s
    return pl.pallas_call(
        functools.partial(_attn_sample_body, past=past, s=s, lk=lk),
        grid=(b,),
        in_specs=[
            pl.BlockSpec((MLA_HEADS, s, 2 * LANES), lambda i: (0, row0 + i, 0)),
            pl.BlockSpec((None, None, past, kv_lora), lambda i: (e, i, 0, 0)),
            pl.BlockSpec((s, kv_lora), lambda i: (row0 + i, 0)),
            pl.BlockSpec((None, None, QK_ROPE, past), lambda i: (e, i, 0, 0)),
            pl.BlockSpec((s, LANES), lambda i: (row0 + i, 0)),
            pl.BlockSpec((None,) + wk.shape[1:], lambda i: (e, 0, 0), pipeline_mode=pl.Buffered(1)),
            pl.BlockSpec((None,) + wv.shape[1:], lambda i: (e, 0, 0), pipeline_mode=pl.Buffered(1)),
        ],
        out_specs=pl.BlockSpec((s, MLA_HEADS * V_HEAD), lambda i: (i, 0)),
        out_shape=jax.ShapeDtypeStruct((b * s, MLA_HEADS * V_HEAD), BF16),
        scratch_shapes=[pltpu.VMEM((lk, kv_lora), BF16), pltpu.VMEM((MLA_HEADS * s, lk), BF16)],
        compiler_params=_params("parallel"),
        name="attn_sample",
    )(q, ckv_cache, ckv, kpe_cache, kpe, wk, wv)


def _pool_segment(ext_ref, pooled_ref, row0, n, pos0, gdim):
    pos = (lax.broadcasted_iota(jnp.int32, (n, 1), 0) + pos0).astype(F32)
    for gi, w in enumerate(POOL_WINDOWS):
        cols = slice(gi * gdim, (gi + 1) * gdim)
        u = ext_ref[HIST_ROWS:HIST_ROWS + n, cols]
        acc = u
        for k in range(1, w):
            acc = acc + ext_ref[HIST_ROWS - k:HIST_ROWS - k + n, cols]
        cnt = jnp.minimum(jnp.float32(w), pos + 1.0)
        pooled_ref[row0:row0 + n, cols] = (acc / cnt - u).astype(BF16)


def _pool_project(pooled_ref, pw_ref, sc_ref, o_ref, gdim):
    for gi in range(len(POOL_WINDOWS)):
        cols = slice(gi * gdim, (gi + 1) * gdim)
        o_ref[:, cols] = (_dot(pooled_ref[:, cols], pw_ref[gi]) * sc_ref[:, cols]).astype(BF16)


def _even_out_body(*refs, n_x, tm, n_prompt_tiles, s, past, pool_dim):
    x_refs, refs = refs[:n_x], refs[n_x:]
    (u_ref, hist_ref, ap_ref, as_ref, pw_ref, sc_ref, w_ref, o_ref, ext_ref, exts_ref, pooled_ref, pool_ref) = refs
    i = pl.program_id(0)
    gdim = pool_dim // len(POOL_WINDOWS)

    def project(a_ref, x_ref):
        _pool_project(pooled_ref, pw_ref, sc_ref, pool_ref, gdim)
        o_ref[...] = x_ref[...] + _dot(pool_ref[...], w_ref[:pool_dim]) + _dot(a_ref[...], w_ref[pool_dim:])

    @pl.when(i == 0)
    def _():
        ext_ref[:HIST_ROWS] = jnp.zeros((HIST_ROWS, pool_dim), F32)

    @pl.when(i < n_prompt_tiles)
    def _():
        ext_ref[HIST_ROWS:] = u_ref[...]
        _pool_segment(ext_ref, pooled_ref, 0, tm, i * tm, gdim)
        ext_ref[:HIST_ROWS] = ext_ref[tm:tm + HIST_ROWS]
        project(ap_ref, x_refs[0])

    @pl.when(i >= n_prompt_tiles)
    def _():
        for bi in range(tm // s):
            ext = exts_ref.at[bi]
            ext[0:1] = jnp.zeros((1, pool_dim), F32)
            ext[1:HIST_ROWS] = hist_ref[bi]
            ext[HIST_ROWS:] = u_ref[bi * s:(bi + 1) * s]
            _pool_segment(ext, pooled_ref, bi * s, s, past, gdim)
        project(as_ref, x_refs[-1])


def _even_out(x, u, state_pool, attn_p, attn_s, pw, scale, wout, e, lp, s, past):
    m, pool_dim = u.shape
    attn_dim = attn_p.shape[1]
    ng = len(POOL_WINDOWS)
    tm = _tile(math.gcd(lp, m - lp), 512, math.lcm(HIST_ROWS, s))
    npt = lp // tm
    xs, x_specs, _ = _row_specs(x, tm)
    d = xs[0].shape[1]
    return pl.pallas_call(
        functools.partial(_even_out_body, n_x=len(xs), tm=tm, n_prompt_tiles=npt, s=s, past=past,
                          pool_dim=pool_dim),
        grid=(m // tm,),
        in_specs=x_specs + [
            pl.BlockSpec((tm, pool_dim), lambda i: (i, 0)),
            pl.BlockSpec((None, tm // s, POOL_HIST, pool_dim), lambda i: (e, jnp.maximum(i - npt, 0), 0, 0)),
            pl.BlockSpec((tm, attn_dim), lambda i: (jnp.minimum(i, npt - 1), 0)),
            pl.BlockSpec((tm, attn_dim), lambda i: (jnp.maximum(i - npt, 0), 0)),
            pl.BlockSpec((None, ng, pool_dim // ng, pool_dim // ng), lambda i: (e, 0, 0, 0)),
            pl.BlockSpec((None, 1, pool_dim), lambda i: (e, 0, 0)),
            pl.BlockSpec((None,) + wout.shape[1:], lambda i: (e, 0, 0), pipeline_mode=pl.Buffered(1)),
        ],
        out_specs=pl.BlockSpec((tm, d), lambda i: (i, 0)),
        out_shape=jax.ShapeDtypeStruct((m, d), F32),
        scratch_shapes=[pltpu.VMEM((HIST_ROWS + tm, pool_dim), F32),
                        pltpu.VMEM((tm // s, HIST_ROWS + s, pool_dim), F32),
                        pltpu.VMEM((tm, pool_dim), BF16), pltpu.VMEM((tm, pool_dim), BF16)],
        compiler_params=_params("arbitrary"),
        name="even_out",
    )(*xs, u, state_pool, attn_p, attn_s, pw, scale, wout)


def _odd_in_body(x_ref, g_ref, w_ref, vn_ref, u_ref, v_ref, vf_ref, *, gate):
    h = _rms(x_ref[...], g_ref[...]).astype(BF16)
    v = _rms(jax.nn.gelu(_dot(h, w_ref[:, gate:])), vn_ref[...])
    v_ref[...] = v.astype(BF16)
    vf_ref[...] = v
    n_chunks = 4 if gate % (4 * LANES) == 0 else 1
    cw = gate // n_chunks
    for c in range(n_chunks):
        u_ref[:, c * cw:(c + 1) * cw] = jax.nn.gelu(_dot(h, w_ref[:, c * cw:(c + 1) * cw])).astype(BF16)


def _odd_in(x, g, layer, win, vn, o, lp):
    m, d = x.shape
    gate = win.shape[-1] // 2
    tm = _tile(math.gcd(lp, m - lp), 512)
    first = lp // tm
    return pl.pallas_call(
        functools.partial(_odd_in_body, gate=gate),
        grid=(m // tm,),
        in_specs=[
            pl.BlockSpec((tm, d), lambda i: (i, 0)),
            pl.BlockSpec((None, 1, d), lambda i: (layer, 0, 0)),
            pl.BlockSpec((None, d, 2 * gate), lambda i: (o, 0, 0), pipeline_mode=pl.Buffered(1)),
            pl.BlockSpec((None, 1, gate), lambda i: (o, 0, 0)),
        ],
        out_specs=[
            pl.BlockSpec((tm, gate), lambda i: (i, 0)),
            pl.BlockSpec((tm, gate), lambda i: (i, 0)),
            pl.BlockSpec((tm, gate), lambda i: (jnp.maximum(i - first, 0), 0)),
        ],
        out_shape=[
            jax.ShapeDtypeStruct((m, gate), BF16),
            jax.ShapeDtypeStruct((m, gate), BF16),
            jax.ShapeDtypeStruct((m - lp, gate), F32),
        ],
        compiler_params=_params("arbitrary"),
        name="odd_in",
    )(x, g, win, vn)


def _odd_out_body(x_ref, u_ref, v_ref, ws_ref, b_ref, w_ref, o_ref, us_ref, *, tm, n_prompt_tiles, s):
    is_prompt = pl.program_id(0) < n_prompt_tiles
    ii = lax.broadcasted_iota(jnp.int32, (GMLP_CHUNK, GMLP_CHUNK), 0)
    jj = lax.broadcasted_iota(jnp.int32, (GMLP_CHUNK, GMLP_CHUNK), 1)
    same_stream = jnp.where((ii // s) == (jj // s), 1, 0) + jnp.where(is_prompt, 1, 0)
    keep = (jj <= ii) & (same_stream > 0)
    gdim = u_ref.shape[1] // GMLP_GROUPS
    for gi in range(GMLP_GROUPS):
        cols = slice(gi * gdim, (gi + 1) * gdim)
        wmat = jnp.where(keep, ws_ref[gi], 0.0).astype(BF16)
        bias = b_ref[:, gi:gi + 1]
        for ci in range(tm // GMLP_CHUNK):
            rows = slice(ci * GMLP_CHUNK, (ci + 1) * GMLP_CHUNK)
            sg = _dot(wmat, v_ref[rows, cols]) + bias
            us_ref[rows, cols] = (u_ref[rows, cols].astype(F32) * sg).astype(BF16)
    o_ref[...] = x_ref[...] + _dot(us_ref[...], w_ref[...])


def _odd_out(x, u, v, ws_sel, b_sel, wout, o, lp, s):
    m, d = x.shape
    gate = u.shape[1]
    tm = _tile(math.gcd(lp, m - lp), 512, GMLP_CHUNK)
    npt = lp // tm
    sel = lambda i: jnp.where(i >= npt, 1, 0)
    return pl.pallas_call(
        functools.partial(_odd_out_body, tm=tm, n_prompt_tiles=npt, s=s),
        grid=(m // tm,),
        in_specs=[
            pl.BlockSpec((tm, d), lambda i: (i, 0)),
            pl.BlockSpec((tm, gate), lambda i: (i, 0)),
            pl.BlockSpec((tm, gate), lambda i: (i, 0)),
            pl.BlockSpec((None, None, GMLP_GROUPS, GMLP_CHUNK, GMLP_CHUNK), lambda i: (sel(i), o, 0, 0, 0)),
            pl.BlockSpec((None, None, GMLP_CHUNK, GMLP_GROUPS), lambda i: (sel(i), o, 0, 0)),
            pl.BlockSpec((None,) + wout.shape[1:], lambda i: (o, 0, 0), pipeline_mode=pl.Buffered(1)),
        ],
        out_specs=pl.BlockSpec((tm, d), lambda i: (i, 0)),
        out_shape=jax.ShapeDtypeStruct((m, d), F32),
        scratch_shapes=[pltpu.VMEM((tm, gate), BF16)],
        compiler_params=_params("parallel"),
        name="odd_out",
    )(x, u, v, ws_sel, b_sel, wout)


def kernel(x_prompt, x_sample, cache_mla_ckv, cache_mla_kpe, state_pool, norm_ffn1, norm_mix, norm_ffn2, ffn1_w_gate, ffn1_w_up, ffn1_w_down, ffn2_w_gate, ffn2_w_up, ffn2_w_down, ev_w_in, ev_q_a_norm, ev_kv_a_norm, ev_w_qb, ev_w_kvb, ev_q_nope_norm, ev_q_pe_norm, ev_k_nope_norm, ev_k_pe_norm, ev_pool_w, ev_pool_scale, ev_w_out, od_w_in, od_v_norm, od_w_s, od_b_s, od_w_out):
    bp, lp, d = x_prompt.shape
    b, s, _ = x_sample.shape
    depth = norm_mix.shape[0]
    n_even, n_odd = ev_w_in.shape[0], od_w_in.shape[0]
    past = cache_mla_ckv.shape[2]
    q_lora, kv_lora = ev_q_a_norm.shape[1], ev_kv_a_norm.shape[1]
    pool_dim = ev_pool_scale.shape[1]
    gate = od_v_norm.shape[1]
    m = lp + b * s
    assert bp == 1 and lp % GMLP_CHUNK == 0 and (b * s) % GMLP_CHUNK == 0 and GMLP_CHUNK % s == 0
    assert past % CHUNK == 0 and s <= CHUNK, "every cached and new key must be visible to every sample query"
    assert s >= POOL_HIST and lp >= POOL_HIST and lp % s == 0
    assert ev_w_in.shape[2] == pool_dim + q_lora + kv_lora + QK_ROPE

    vec = lambda a: a[:, None, :]
    bf = lambda a: a.astype(BF16)
    pad_lanes = lambda a: jnp.pad(a, [(0, 0)] * (a.ndim - 1) + [(0, (-a.shape[-1]) % LANES)])

    w_ffn = ((ffn1_w_gate, ffn1_w_up, ffn1_w_down), (ffn2_w_gate, ffn2_w_up, ffn2_w_down))
    g_ffn = (vec(norm_ffn1), vec(norm_ffn2))
    g_mix = vec(norm_mix)
    wq = ev_w_qb.reshape(n_even, q_lora, MLA_HEADS, QK_HEAD)
    w_qb = bf(jnp.concatenate([wq[..., :QK_NOPE].reshape(n_even, q_lora, -1),
                               wq[..., QK_NOPE:].reshape(n_even, q_lora, -1)], axis=-1))
    w_kvb, pool_w = bf(ev_w_kvb), bf(ev_pool_w)
    wkv = w_kvb.reshape(n_even, kv_lora, MLA_HEADS, QK_NOPE + V_HEAD)
    w_k = wkv[..., :QK_NOPE].reshape(n_even, kv_lora, -1)
    w_v = wkv[..., QK_NOPE:].reshape(n_even, kv_lora, -1)
    qpn = vec(jnp.tile(ev_q_pe_norm, (1, LANES // QK_ROPE)))
    kpn = vec(pad_lanes(ev_k_pe_norm))
    kpe_cache_t = jnp.swapaxes(cache_mla_kpe, 2, 3)
    rep = GMLP_CHUNK // s
    ws_sel = jnp.stack([od_w_s, jnp.tile(od_w_s[:, :, :s, :s], (1, 1, rep, rep))])
    b_sel = jnp.stack([od_b_s, jnp.tile(od_b_s[:, :, :s], (1, 1, rep))]).transpose(0, 1, 3, 2)

    cos, sin, w_in_e = _rope_tables(m, lp, past, s, ev_w_in)
    new_ckv, new_kpe, new_u, new_v = [], [], [], []
    w_cur = tuple(bf(w[0]) for w in w_ffn[0])
    for layer in range(depth):
        if layer == 0:
            x_s, _ = _ffn(x_sample.reshape(b * s, d), g_ffn[0], layer, w_cur)
            x_p, w_cur = _ffn(x_prompt.reshape(lp, d), g_ffn[0], layer, w_cur, w_ffn[1], layer)
            x = (x_p, x_s)
        else:
            x, w_cur = _ffn(x, g_ffn[0], layer, w_cur, w_ffn[1], layer)
        if layer % 2 == 0:
            e = layer // 2
            u, q, ckv, kpe = _even_in(x, g_mix, layer, w_in_e, vec(ev_q_a_norm), vec(ev_kv_a_norm), w_qb,
                                      vec(ev_q_nope_norm), vec(ev_k_nope_norm), qpn, kpn, cos, sin, e,
                                      pool_dim, q_lora, kv_lora)
            k_p, v_p = _kv_prompt(ckv, kpe, w_kvb, e, lp)
            if e == 0:
                attn_p, (w_out_e, w_in_o, w_out_o) = _attn_prompt(q, k_p, v_p, lp, (ev_w_out, od_w_in, od_w_out))
            else:
                attn_p, _ = _attn_prompt(q, k_p, v_p, lp)
            attn_s = _attn_sample(q, cache_mla_ckv, ckv, kpe_cache_t, kpe, w_k, w_v, e, lp, b, s)
            x = _even_out(x, u, state_pool, attn_p, attn_s, pool_w, vec(ev_pool_scale), w_out_e, e, lp, s, past)
            new_ckv.append((ckv[:lp], ckv[lp:]))
            new_kpe.append((kpe[:lp, :QK_ROPE], kpe[lp:, :QK_ROPE]))
            new_u.append((u[lp - POOL_HIST:lp], u[lp:].reshape(b, s, pool_dim)[:, s - POOL_HIST:]))
        else:
            o = layer // 2
            u, v, v_f32 = _odd_in(x, g_mix, layer, w_in_o, vec(od_v_norm), o, lp)
            x = _odd_out(x, u, v, ws_sel, b_sel, w_out_o, o, lp, s)
            new_v.append(v_f32)
        if layer + 1 < depth:
            x, w_cur = _ffn(x, g_ffn[1], layer, w_cur, w_ffn[0], layer + 1)
        else:
            y_p, _ = _ffn(x, g_ffn[1], layer, w_cur, rows=(0, lp))
            y_s, _ = _ffn(x, g_ffn[1], layer, w_cur, rows=(lp, m - lp))

    stack = lambda pairs, k, shape: jnp.stack([p[k] for p in pairs]).reshape((n_even,) + shape)
    return (
        y_p.reshape(1, lp, d),
        y_s.reshape(b, s, d),
        stack(new_ckv, 0, (1, lp, kv_lora)),
        stack(new_kpe, 0, (1, lp, QK_ROPE)),
        stack(new_u, 0, (1, POOL_HIST, pool_dim)),
        stack(new_ckv, 1, (b, s, kv_lora)),
        stack(new_kpe, 1, (b, s, QK_ROPE)),
        stack(new_u, 1, (b, POOL_HIST, pool_dim)),
        jnp.stack(new_v).reshape(n_odd, b, s, gate),
    )
```

```python
import functools
import math

import jax
import jax.numpy as jnp
from jax import lax
from jax.experimental import pallas as pl
from jax.experimental.pallas import tpu as pltpu

F32 = jnp.float32
BF16 = jnp.bfloat16

EPS = 1e-6
CHUNK = 64
POOL_WINDOWS = (2, 4, 8, 16)
POOL_HIST = max(POOL_WINDOWS) - 1
HIST_ROWS = 16
MLA_HEADS = 8
QK_NOPE = 128
QK_ROPE = 64
V_HEAD = 128
QK_HEAD = QK_NOPE + QK_ROPE
ATTN_SCALE = QK_HEAD ** -0.5
Q_SCALE = ATTN_SCALE * math.log2(math.e)
ROPE_THETA = 10000.0
GMLP_CHUNK = 128
GMLP_GROUPS = 8
LANES = 128
NEG_BIG = -1e30
VMEM_LIMIT = 60 * 1024 * 1024


def _params(*sem):
    return pltpu.CompilerParams(dimension_semantics=sem, vmem_limit_bytes=VMEM_LIMIT)


def _tile(n, pref, mult=8):
    if n <= pref:
        return n
    for t in range(pref - pref % mult, 0, -mult):
        if n % t == 0:
            return t
    raise ValueError(f"no tile for {n}")


def _const_spec(shape):
    nd = len(shape)
    return pl.BlockSpec(shape, lambda *_: (0,) * nd, pipeline_mode=pl.Buffered(1))


def _rms(x, g):
    ms = jnp.mean(x * x, axis=-1, keepdims=True)
    return x * lax.rsqrt(ms + EPS) * g


def _row_specs(x, tm):
    if not isinstance(x, tuple):
        return (x,), [pl.BlockSpec((tm, x.shape[1]), lambda i: (i, 0))], None
    npt = x[0].shape[0] // tm
    d = x[0].shape[1]
    return x, [pl.BlockSpec((tm, d), lambda i: (jnp.minimum(i, npt - 1), 0)),
               pl.BlockSpec((tm, d), lambda i: (jnp.maximum(i - npt, 0), 0))], npt


def _read_rows(x_refs, n_prompt_tiles):
    if len(x_refs) == 1:
        return x_refs[0][...]
    return jnp.where(pl.program_id(0) < n_prompt_tiles, x_refs[0][...], x_refs[1][...])


def _dot(a, b):
    return jnp.dot(a, b, preferred_element_type=F32)


def _dot_nt(a, b):
    return lax.dot_general(a, b, (((1,), (1,)), ((), ())), preferred_element_type=F32)


def _ffn_body(x_ref, g_ref, wg_ref, wu_ref, wd_ref, *rest, convert_next):
    *rest, h_ref = rest
    if convert_next:
        (*nxt, o_ref, cg_ref, cu_ref, cd_ref) = rest

        @pl.when(pl.program_id(0) < convert_next)
        def _():
            for src, dst in zip(nxt, (cg_ref, cu_ref, cd_ref)):
                dst[...] = src[...].astype(BF16)
    else:
        (o_ref,) = rest

    @pl.when(pl.program_id(1) == 0)
    def _():
        x = x_ref[...]
        h_ref[...] = _rms(x, g_ref[...]).astype(BF16)
        o_ref[...] = x

    h = h_ref[...]
    a = _dot(h, wg_ref[...])
    b = _dot(h, wu_ref[...])
    act = (a * jax.nn.sigmoid(a) * b * 0.5).astype(BF16)
    o_ref[...] += _dot(act, wd_ref[...])


def _ffn(x, g, layer, w, w_next=None, next_layer=None, rows=None):
    row0, m = (0, x.shape[0]) if rows is None else rows
    d = x.shape[1]
    f = w[0].shape[-1]
    tm = _tile(math.gcd(row0, m), 1024)
    tf = _tile(f, 512, LANES)
    n_i, t0 = m // tm, row0 // tm
    in_specs = [
        pl.BlockSpec((tm, d), lambda i, j: (t0 + i, 0)),
        pl.BlockSpec((None, 1, d), lambda i, j: (layer, 0, 0)),
        pl.BlockSpec((d, tf), lambda i, j: (0, j)),
        pl.BlockSpec((d, tf), lambda i, j: (0, j)),
        pl.BlockSpec((tf, d), lambda i, j: (j, 0)),
    ]
    out_specs = [pl.BlockSpec((tm, d), lambda i, j: (i, 0))]
    out_shape = [jax.ShapeDtypeStruct((m, d), F32)]
    nr = 0
    if w_next is not None:
        nr = max(r for r in (1, 2, 4, 8, 16) if r <= n_i and d % (r * LANES) == 0)
        db = d // nr
        n_j = f // tf
        blk = lambda i: jnp.minimum(i, nr - 1)
        col = lambda i, j: jnp.where(i < nr, j, n_j - 1)
        in_specs += [
            pl.BlockSpec((None, db, tf), lambda i, j: (next_layer, blk(i), col(i, j))),
            pl.BlockSpec((None, db, tf), lambda i, j: (next_layer, blk(i), col(i, j))),
            pl.BlockSpec((None, tf, db), lambda i, j: (next_layer, col(i, j), blk(i))),
        ]
        out_specs += [
            pl.BlockSpec((db, tf), lambda i, j: (blk(i), col(i, j))),
            pl.BlockSpec((db, tf), lambda i, j: (blk(i), col(i, j))),
            pl.BlockSpec((tf, db), lambda i, j: (col(i, j), blk(i))),
        ]
        out_shape += [jax.ShapeDtypeStruct((d, f), BF16), jax.ShapeDtypeStruct((d, f), BF16),
                      jax.ShapeDtypeStruct((f, d), BF16)]
    out = pl.pallas_call(
        functools.partial(_ffn_body, convert_next=nr),
        grid=(n_i, f // tf),
        in_specs=in_specs,
        out_specs=out_specs,
        out_shape=out_shape,
        scratch_shapes=[pltpu.VMEM((tm, d), BF16)],
        compiler_params=_params("arbitrary", "arbitrary"),
        name="ffn",
    )(x, g, *w, *(w_next or ()))
    return out[0], tuple(out[1:])


def _rope_tab_body(inv_ref, sgn_ref, win_ref, *rest, tm, lp, past, s, n_blocks):
    n_side = (len(rest) - 3) // 2
    side_in, (cos_ref, sin_ref, wout_ref), side_out = rest[:n_side], rest[n_side:n_side + 3], rest[n_side + 3:]
    r = lax.broadcasted_iota(jnp.int32, (tm, LANES), 0) + pl.program_id(0) * tm
    t = r - lp
    t = (t & (s - 1)) if s & (s - 1) == 0 else lax.rem(t, s)
    pos = jnp.where(r >= lp, past + t, r).astype(F32)
    ang = pos * inv_ref[...]
    cos_ref[...] = jnp.cos(ang)
    sin_ref[...] = jnp.sin(ang) * sgn_ref[...]

    @pl.when(pl.program_id(0) < n_blocks)
    def _():
        n_raw = win_ref.shape[1]
        wout_ref[:, :n_raw] = win_ref[...].astype(BF16)
        wout_ref[:, n_raw:] = jnp.zeros((wout_ref.shape[0], wout_ref.shape[1] - n_raw), BF16)
        for src, dst in zip(side_in, side_out):
            dst[...] = src[...].astype(BF16)


def _rope_tables(m, lp, past, s, w_in, side=()):
    half = QK_ROPE // 2
    inv = ROPE_THETA ** (-jnp.arange(0, QK_ROPE, 2, dtype=F32) / QK_ROPE)
    inv = jnp.tile(inv, LANES // half)[None, :]
    sgn = jnp.tile(jnp.concatenate([-jnp.ones((half,), F32), jnp.ones((half,), F32)]), LANES // QK_ROPE)[None, :]
    tm = _tile(m, 512)
    steps = m // tm
    w2d = w_in.reshape(-1, w_in.shape[-1])
    rows, n_raw = w2d.shape
    n_pad = n_raw + (-n_raw) % LANES
    all_rows = [rows] + [a.shape[1] for a in side]
    nb = max(r for r in (1, 2, 4, 8, 16, 32, 64) if r <= steps and all(n % (r * 16) == 0 for n in all_rows))
    blk = lambda i: (jnp.minimum(i, nb - 1), 0)
    blk0 = lambda i: (0, jnp.minimum(i, nb - 1), 0)
    cos, sin, w_bf, *side_bf = pl.pallas_call(
        functools.partial(_rope_tab_body, tm=tm, lp=lp, past=past, s=s, n_blocks=nb),
        grid=(steps,),
        in_specs=[_const_spec((1, LANES)), _const_spec((1, LANES)), pl.BlockSpec((rows // nb, n_raw), blk)]
        + [pl.BlockSpec((None, a.shape[1] // nb, a.shape[2]), blk0) for a in side],
        out_specs=[pl.BlockSpec((tm, LANES), lambda i: (i, 0))] * 2 + [pl.BlockSpec((rows // nb, n_pad), blk)]
        + [pl.BlockSpec((a.shape[1] // nb, a.shape[2]), blk) for a in side],
        out_shape=[jax.ShapeDtypeStruct((m, LANES), F32)] * 2 + [jax.ShapeDtypeStruct((rows, n_pad), BF16)]
        + [jax.ShapeDtypeStruct(a.shape[1:], BF16) for a in side],
        compiler_params=_params("arbitrary"),
        name="rope_tables",
    )(inv, sgn, w2d, *side)
    return cos, sin, w_bf.reshape(w_in.shape[:-1] + (n_pad,)), tuple(side_bf)


def _swap_halves(x):
    lane = lax.broadcasted_iota(jnp.int32, x.shape, 1)
    left = pltpu.roll(x, LANES - QK_ROPE // 2, 1)
    right = pltpu.roll(x, QK_ROPE // 2, 1)
    return jnp.where((lane & (QK_ROPE - 1)) < QK_ROPE // 2, left, right)


def _even_in_body(*refs, n_x, n_prompt_tiles, pool_dim, q_lora, kv_lora):
    x_refs, refs = refs[:n_x], refs[n_x:]
    (g_ref, win_ref, qan_ref, kvan_ref, wqb_ref, qnn_ref, knn_ref, qpn_ref, kpn_ref, cos_ref, sin_ref,
     u_ref, q_ref, ckv_ref, kpe_ref) = refs
    h = _rms(_read_rows(x_refs, n_prompt_tiles), g_ref[...]).astype(BF16)
    o1, o2 = pool_dim + q_lora, pool_dim + q_lora + kv_lora
    cos, sin = cos_ref[...], sin_ref[...]
    lane = lax.broadcasted_iota(jnp.int32, cos.shape, 1)
    lo = lane < QK_ROPE

    def rope(t):
        return t * cos + _swap_halves(t) * sin

    qn = _rms(_dot(h, win_ref[:, pool_dim:o1]), qan_ref[...]).astype(BF16)
    q = _dot(qn, wqb_ref[...])
    zk = _dot(h, win_ref[:, o1:])
    ckv_ref[...] = _rms(zk[:, :kv_lora], kvan_ref[...])
    kp = zk[:, kv_lora:]
    kp = kp * lax.rsqrt(jnp.sum(kp * kp, axis=-1, keepdims=True) / QK_ROPE + EPS) * kpn_ref[...]
    kpe_ref[...] = rope(kp)
    u_ref[...] = _dot(h, win_ref[:, :pool_dim])

    nope_w = MLA_HEADS * QK_NOPE
    k_gain = knn_ref[...] * Q_SCALE
    for hd in range(MLA_HEADS):
        qh = _rms(q[:, hd * QK_NOPE:(hd + 1) * QK_NOPE], qnn_ref[...]) * k_gain
        q_ref[hd, :, :QK_NOPE] = qh.astype(BF16)
    for j in range(MLA_HEADS // 2):
        t = q[:, nope_w + j * LANES:nope_w + (j + 1) * LANES]
        tt = t * t
        s_lo = jnp.sum(jnp.where(lo, tt, 0.0), axis=-1, keepdims=True)
        s_hi = jnp.sum(jnp.where(lo, 0.0, tt), axis=-1, keepdims=True)
        inv = jnp.where(lo, lax.rsqrt(s_lo / QK_ROPE + EPS), lax.rsqrt(s_hi / QK_ROPE + EPS))
        t = rope(t * inv * qpn_ref[...]) * Q_SCALE
        q_ref[2 * j, :, QK_NOPE:] = jnp.where(lo, t, 0.0).astype(BF16)
        q_ref[2 * j + 1, :, QK_NOPE:] = jnp.where(lo, 0.0, t).astype(BF16)


def _even_in(x, g, layer, win, qan, kvan, wqb, qnn, knn, qpn, kpn, cos, sin, e, pool_dim, q_lora, kv_lora):
    rows = tuple(a.shape[0] for a in x) if isinstance(x, tuple) else (x.shape[0],)
    m, d = sum(rows), (x[0] if isinstance(x, tuple) else x).shape[1]
    tm = _tile(math.gcd(*rows), 512)
    xs, x_specs, npt = _row_specs(x, tm)
    n_in = win.shape[-1]
    n_q = wqb.shape[-1]
    row = lambda w: pl.BlockSpec((tm, w), lambda i: (i, 0))
    vec = lambda w, idx: pl.BlockSpec((None, 1, w), lambda i: (idx, 0, 0))
    return pl.pallas_call(
        functools.partial(_even_in_body, n_x=len(xs), n_prompt_tiles=npt, pool_dim=pool_dim, q_lora=q_lora,
                          kv_lora=kv_lora),
        grid=(m // tm,),
        in_specs=x_specs + [
            vec(d, layer),
            pl.BlockSpec((None, d, n_in), lambda i: (e, 0, 0), pipeline_mode=pl.Buffered(1)),
            vec(q_lora, e), vec(kv_lora, e),
            pl.BlockSpec((None, q_lora, n_q), lambda i: (e, 0, 0), pipeline_mode=pl.Buffered(1)),
            vec(QK_NOPE, e), vec(QK_NOPE, e), vec(LANES, e), vec(LANES, e),
            row(LANES), row(LANES),
        ],
        out_specs=[
            row(pool_dim),
            pl.BlockSpec((MLA_HEADS, tm, 2 * LANES), lambda i: (0, i, 0)),
            row(kv_lora), row(LANES),
        ],
        out_shape=[
            jax.ShapeDtypeStruct((m, pool_dim), F32),
            jax.ShapeDtypeStruct((MLA_HEADS, m, 2 * LANES), BF16),
            jax.ShapeDtypeStruct((m, kv_lora), F32),
            jax.ShapeDtypeStruct((m, LANES), F32),
        ],
        compiler_params=_params("arbitrary"),
        name="even_in",
    )(*xs, g, win, qan, kvan, wqb, qnn, knn, qpn, kpn, cos, sin)


def _kv_body(ckv_ref, kpe_ref, w_ref, k_ref, v_ref):
    c = ckv_ref[...].astype(BF16)
    kp = kpe_ref[...]
    kp_both = (kp + pltpu.roll(kp, QK_ROPE, 1)).astype(BF16)
    hw = QK_NOPE + V_HEAD
    for hd in range(MLA_HEADS):
        kv = _dot(c, w_ref[:, hd * hw:(hd + 1) * hw])
        k_ref[hd, :, :QK_NOPE] = _rms(kv[:, :QK_NOPE], 1.0).astype(BF16)
        k_ref[hd, :, QK_NOPE:] = kp_both
        v_ref[hd, :, :V_HEAD] = kv[:, QK_NOPE:].astype(BF16)
        v_ref[hd, :, V_HEAD:] = jnp.ones((c.shape[0], LANES), BF16)


def _kv_prompt(ckv, kpe, wkvb, e, lp):
    kv_lora = ckv.shape[-1]
    tm = _tile(lp, 512)
    return pl.pallas_call(
        _kv_body,
        grid=(lp // tm,),
        in_specs=[
            pl.BlockSpec((tm, kv_lora), lambda i: (i, 0)),
            pl.BlockSpec((tm, LANES), lambda i: (i, 0)),
            pl.BlockSpec((None,) + wkvb.shape[1:], lambda i: (e, 0, 0), pipeline_mode=pl.Buffered(1)),
        ],
        out_specs=[
            pl.BlockSpec((MLA_HEADS, tm, 2 * LANES), lambda i: (0, i, 0)),
            pl.BlockSpec((MLA_HEADS, tm, V_HEAD + LANES), lambda i: (0, i, 0)),
        ],
        out_shape=[
            jax.ShapeDtypeStruct((MLA_HEADS, lp, 2 * LANES), BF16),
            jax.ShapeDtypeStruct((MLA_HEADS, lp, V_HEAD + LANES), BF16),
        ],
        compiler_params=_params("parallel"),
        name="kv_prompt",
    )(ckv, kpe, wkvb)


def _flash_body(q_ref, k_ref, v_ref, *rest, t, n_side):
    side_in, (o_ref, *side_out), (sa_ref, sb_ref, m_ref, acc_ref) = rest[:n_side], rest[n_side:-4], rest[-4:]
    for src, dst in zip(side_in, side_out):
        dst[...] = src[...].astype(BF16)
    i = pl.program_id(1)
    q = q_ref[...]
    m_ref[...] = jnp.full(m_ref.shape, NEG_BIG, F32)
    acc_ref[...] = jnp.zeros(acc_ref.shape, F32)

    def scores(j):
        return _dot_nt(q, k_ref[pl.ds(pl.multiple_of(j * t, t), t), :])

    def update(s_ref, j, diagonal=False):
        s = s_ref[...]
        if diagonal:
            qc = lax.broadcasted_iota(jnp.int32, s.shape, 0) // CHUNK
            kc = lax.broadcasted_iota(jnp.int32, s.shape, 1) // CHUNK
            s = jnp.where(kc <= qc, s, NEG_BIG)
        m_old = m_ref[...]
        m_new = jnp.maximum(m_old, jnp.max(s, axis=-1, keepdims=True))
        p = jnp.exp2(s - jnp.tile(m_new, (1, t // LANES)))
        alpha = jnp.tile(jnp.exp2(m_old - m_new), (1, acc_ref.shape[1] // LANES))
        pv = _dot(p.astype(BF16), v_ref[pl.ds(pl.multiple_of(j * t, t), t), :])
        acc_ref[...] = alpha * acc_ref[...] + pv
        m_ref[...] = m_new

    sa_ref[...] = scores(0)

    def pair(jj, carry):
        j = 2 * jj
        sb_ref[...] = scores(j + 1)
        update(sa_ref, j)
        sa_ref[...] = scores(j + 2)
        update(sb_ref, j + 1)
        return carry

    lax.fori_loop(0, i // 2, pair, 0)

    @pl.when(i % 2 == 0)
    def _():
        update(sa_ref, i, diagonal=True)

    @pl.when(i % 2 == 1)
    def _():
        sb_ref[...] = scores(i)
        update(sa_ref, i - 1)
        update(sb_ref, i, diagonal=True)

    acc = acc_ref[...]
    o_ref[...] = (acc[:, :V_HEAD] / acc[:, V_HEAD:]).astype(BF16)


def _attn_prompt(q, k, v, lp, side=()):
    t = _tile(lp, 1024, LANES)
    nq = lp // t
    steps = MLA_HEADS * nq
    side2d = [a.reshape(-1, a.shape[-1]) for a in side]
    assert all(a.shape[0] % (steps * 16) == 0 for a in side2d)
    side_spec = lambda a: pl.BlockSpec((a.shape[0] // steps, a.shape[1]), lambda h, i: (h * nq + i, 0))
    out = pl.pallas_call(
        functools.partial(_flash_body, t=t, n_side=len(side)),
        grid=(MLA_HEADS, nq),
        in_specs=[
            pl.BlockSpec((None, t, 2 * LANES), lambda h, i: (h, i, 0)),
            pl.BlockSpec((None, lp, 2 * LANES), lambda h, i: (h, 0, 0)),
            pl.BlockSpec((None, lp, V_HEAD + LANES), lambda h, i: (h, 0, 0)),
        ] + [side_spec(a) for a in side2d],
        out_specs=[pl.BlockSpec((t, V_HEAD), lambda h, i: (i, h))] + [side_spec(a) for a in side2d],
        out_shape=[jax.ShapeDtypeStruct((lp, MLA_HEADS * V_HEAD), BF16)]
        + [jax.ShapeDtypeStruct(a.shape, BF16) for a in side2d],
        scratch_shapes=[pltpu.VMEM((t, t), F32), pltpu.VMEM((t, t), F32),
                        pltpu.VMEM((t, LANES), F32), pltpu.VMEM((t, V_HEAD + LANES), F32)],
        compiler_params=_params("arbitrary", "arbitrary"),
        name="attn_prompt",
    )(q, k, v, *side2d)
    return out[0], [o.reshape(a.shape) for o, a in zip(out[1:], side)]


def _attn_sample_body(q_ref, cc_ref, cn_ref, pct_ref, pn_ref, wk_ref, wv_ref, o_ref, c_sc, p_sc, *, past, s, lk):
    c_sc[:past] = cc_ref[...].astype(BF16)
    c_sc[past:past + s] = cn_ref[...].astype(BF16)
    if lk > past + s:
        c_sc[past + s:] = jnp.zeros((lk - past - s, c_sc.shape[1]), BF16)
    c = c_sc[...]
    valid = lax.broadcasted_iota(jnp.int32, (s, lk), 1) < past + s
    ones = jnp.ones((8, QK_NOPE), BF16)
    n_ch = 4 if lk % 64 == 0 else 1
    ch = lk // n_ch
    q_pe = jnp.concatenate([q_ref[hd][:, QK_NOPE:] for hd in range(MLA_HEADS)], axis=0)
    pct = pct_ref[...].astype(BF16)
    pn = pn_ref[...]
    kp_new = jnp.concatenate([(pn + pltpu.roll(pn, QK_ROPE, 1)).astype(BF16),
                              jnp.zeros((lk - past - s, LANES), BF16)], axis=0)
    pe = jnp.concatenate([_dot(q_pe, jnp.concatenate([pct, pct], axis=0)), _dot_nt(q_pe, kp_new)], axis=1)
    inv_l = []
    for pair in range(MLA_HEADS // 2):
        wk = wk_ref[:, pair * 2 * QK_NOPE:(pair + 1) * 2 * QK_NOPE]
        kk = jnp.concatenate([_dot(c[r * ch:(r + 1) * ch], wk) for r in range(n_ch)])
        for hd in (2 * pair, 2 * pair + 1):
            kn = kk[:, (hd % 2) * QK_NOPE:(hd % 2 + 1) * QK_NOPE]
            ms = _dot_nt(ones, (kn * kn).astype(BF16))[0:1] / QK_NOPE
            sc = _dot_nt(q_ref[hd][:, :QK_NOPE], kn.astype(BF16)) * lax.rsqrt(ms + EPS) + pe[hd * s:(hd + 1) * s]
            sc = jnp.where(valid, sc, NEG_BIG)
            p = jnp.exp2(sc - jnp.max(sc, axis=-1, keepdims=True))
            inv_l.append(1.0 / jnp.sum(p, axis=-1, keepdims=True))
            p_sc[hd * s:(hd + 1) * s] = p.astype(BF16)
    ctx = _dot(p_sc[...], c)
    for hd in range(MLA_HEADS):
        ctx_h = (ctx[hd * s:(hd + 1) * s] * inv_l[hd]).astype(BF16)
        o_ref[:, hd * V_HEAD:(hd + 1) * V_HEAD] = _dot(ctx_h, wv_ref[:, hd * V_HEAD:(hd + 1) * V_HEAD]).astype(BF16)


def _attn_sample(q, ckv_cache, ckv, kpe_cache, kpe, wk, wv, e, lp, b, s):
    past, kv_lora = ckv_cache.shape[2], ckv_cache.shape[3]
    lk = past + -(-s // LANES) * LANES
    row0 = lp // s
    return pl.pallas_call(
        functools.partial(_attn_sample_body, past=past, s=s, lk=lk),
        grid=(b,),
        in_specs=[
            pl.BlockSpec((MLA_HEADS, s, 2 * LANES), lambda i: (0, row0 + i, 0)),
            pl.BlockSpec((None, None, past, kv_lora), lambda i: (e, i, 0, 0)),
            pl.BlockSpec((s, kv_lora), lambda i: (row0 + i, 0)),
            pl.BlockSpec((None, None, QK_ROPE, past), lambda i: (e, i, 0, 0)),
            pl.BlockSpec((s, LANES), lambda i: (row0 + i, 0)),
            pl.BlockSpec((None,) + wk.shape[1:], lambda i: (e, 0, 0), pipeline_mode=pl.Buffered(1)),
            pl.BlockSpec((None,) + wv.shape[1:], lambda i: (e, 0, 0), pipeline_mode=pl.Buffered(1)),
        ],
        out_specs=pl.BlockSpec((s, MLA_HEADS * V_HEAD), lambda i: (i, 0)),
        out_shape=jax.ShapeDtypeStruct((b * s, MLA_HEADS * V_HEAD), BF16),
        scratch_shapes=[pltpu.VMEM((lk, kv_lora), BF16), pltpu.VMEM((MLA_HEADS * s, lk), BF16)],
        compiler_params=_params("parallel"),
        name="attn_sample",
    )(q, ckv_cache, ckv, kpe_cache, kpe, wk, wv)


def _pool_segment(ext_ref, pooled_ref, row0, n, pos0, gdim):
    pos = (lax.broadcasted_iota(jnp.int32, (n, 1), 0) + pos0).astype(F32)
    for gi, w in enumerate(POOL_WINDOWS):
        cols = slice(gi * gdim, (gi + 1) * gdim)
        u = ext_ref[HIST_ROWS:HIST_ROWS + n, cols]
        acc = u
        for k in range(1, w):
            acc = acc + ext_ref[HIST_ROWS - k:HIST_ROWS - k + n, cols]
        cnt = jnp.minimum(jnp.float32(w), pos + 1.0)
        pooled_ref[row0:row0 + n, cols] = (acc / cnt - u).astype(BF16)


def _pool_project(pooled_ref, pw_ref, sc_ref, o_ref, gdim):
    for gi in range(len(POOL_WINDOWS)):
        cols = slice(gi * gdim, (gi + 1) * gdim)
        o_ref[:, cols] = (_dot(pooled_ref[:, cols], pw_ref[gi]) * sc_ref[:, cols]).astype(BF16)


def _even_out_body(*refs, n_x, tm, n_prompt_tiles, s, past, pool_dim):
    x_refs, refs = refs[:n_x], refs[n_x:]
    (u_ref, hist_ref, ap_ref, as_ref, pw_ref, sc_ref, w_ref, o_ref, ext_ref, exts_ref, pooled_ref, pool_ref) = refs
    i = pl.program_id(0)
    gdim = pool_dim // len(POOL_WINDOWS)

    def project(a_ref, x_ref):
        _pool_project(pooled_ref, pw_ref, sc_ref, pool_ref, gdim)
        o_ref[...] = x_ref[...] + _dot(pool_ref[...], w_ref[:pool_dim]) + _dot(a_ref[...], w_ref[pool_dim:])

    @pl.when(i == 0)
    def _():
        ext_ref[:HIST_ROWS] = jnp.zeros((HIST_ROWS, pool_dim), F32)

    @pl.when(i < n_prompt_tiles)
    def _():
        ext_ref[HIST_ROWS:] = u_ref[...]
        _pool_segment(ext_ref, pooled_ref, 0, tm, i * tm, gdim)
        ext_ref[:HIST_ROWS] = ext_ref[tm:tm + HIST_ROWS]
        project(ap_ref, x_refs[0])

    @pl.when(i >= n_prompt_tiles)
    def _():
        for bi in range(tm // s):
            ext = exts_ref.at[bi]
            ext[0:1] = jnp.zeros((1, pool_dim), F32)
            ext[1:HIST_ROWS] = hist_ref[bi]
            ext[HIST_ROWS:] = u_ref[bi * s:(bi + 1) * s]
            _pool_segment(ext, pooled_ref, bi * s, s, past, gdim)
        project(as_ref, x_refs[-1])


def _even_out(x, u, state_pool, attn_p, attn_s, pw, scale, wout, e, lp, s, past):
    m, pool_dim = u.shape
    attn_dim = attn_p.shape[1]
    ng = len(POOL_WINDOWS)
    tm = _tile(math.gcd(lp, m - lp), 512, math.lcm(HIST_ROWS, s))
    npt = lp // tm
    xs, x_specs, _ = _row_specs(x, tm)
    d = xs[0].shape[1]
    return pl.pallas_call(
        functools.partial(_even_out_body, n_x=len(xs), tm=tm, n_prompt_tiles=npt, s=s, past=past,
                          pool_dim=pool_dim),
        grid=(m // tm,),
        in_specs=x_specs + [
            pl.BlockSpec((tm, pool_dim), lambda i: (i, 0)),
            pl.BlockSpec((None, tm // s, POOL_HIST, pool_dim), lambda i: (e, jnp.maximum(i - npt, 0), 0, 0)),
            pl.BlockSpec((tm, attn_dim), lambda i: (jnp.minimum(i, npt - 1), 0)),
            pl.BlockSpec((tm, attn_dim), lambda i: (jnp.maximum(i - npt, 0), 0)),
            pl.BlockSpec((None, ng, pool_dim // ng, pool_dim // ng), lambda i: (e, 0, 0, 0)),
            pl.BlockSpec((None, 1, pool_dim), lambda i: (e, 0, 0)),
            pl.BlockSpec((None,) + wout.shape[1:], lambda i: (e, 0, 0), pipeline_mode=pl.Buffered(1)),
        ],
        out_specs=pl.BlockSpec((tm, d), lambda i: (i, 0)),
        out_shape=jax.ShapeDtypeStruct((m, d), F32),
        scratch_shapes=[pltpu.VMEM((HIST_ROWS + tm, pool_dim), F32),
                        pltpu.VMEM((tm // s, HIST_ROWS + s, pool_dim), F32),
                        pltpu.VMEM((tm, pool_dim), BF16), pltpu.VMEM((tm, pool_dim), BF16)],
        compiler_params=_params("arbitrary"),
        name="even_out",
    )(*xs, u, state_pool, attn_p, attn_s, pw, scale, wout)


def _odd_in_body(x_ref, g_ref, w_ref, vn_ref, u_ref, v_ref, vf_ref, *, gate):
    h = _rms(x_ref[...], g_ref[...]).astype(BF16)
    v = _rms(jax.nn.gelu(_dot(h, w_ref[:, gate:])), vn_ref[...])
    v_ref[...] = v.astype(BF16)
    vf_ref[...] = v
    n_chunks = 4 if gate % (4 * LANES) == 0 else 1
    cw = gate // n_chunks
    for c in range(n_chunks):
        u_ref[:, c * cw:(c + 1) * cw] = jax.nn.gelu(_dot(h, w_ref[:, c * cw:(c + 1) * cw])).astype(BF16)


def _odd_in(x, g, layer, win, vn, o, lp):
    m, d = x.shape
    gate = win.shape[-1] // 2
    tm = _tile(math.gcd(lp, m - lp), 512)
    first = lp // tm
    return pl.pallas_call(
        functools.partial(_odd_in_body, gate=gate),
        grid=(m // tm,),
        in_specs=[
            pl.BlockSpec((tm, d), lambda i: (i, 0)),
            pl.BlockSpec((None, 1, d), lambda i: (layer, 0, 0)),
            pl.BlockSpec((None, d, 2 * gate), lambda i: (o, 0, 0), pipeline_mode=pl.Buffered(1)),
            pl.BlockSpec((None, 1, gate), lambda i: (o, 0, 0)),
        ],
        out_specs=[
            pl.BlockSpec((tm, gate), lambda i: (i, 0)),
            pl.BlockSpec((tm, gate), lambda i: (i, 0)),
            pl.BlockSpec((tm, gate), lambda i: (jnp.maximum(i - first, 0), 0)),
        ],
        out_shape=[
            jax.ShapeDtypeStruct((m, gate), BF16),
            jax.ShapeDtypeStruct((m, gate), BF16),
            jax.ShapeDtypeStruct((m - lp, gate), F32),
        ],
        compiler_params=_params("arbitrary"),
        name="odd_in",
    )(x, g, win, vn)


def _odd_out_body(x_ref, u_ref, v_ref, ws_ref, b_ref, w_ref, o_ref, us_ref, *, tm, n_prompt_tiles, s):
    is_prompt = pl.program_id(0) < n_prompt_tiles
    ii = lax.broadcasted_iota(jnp.int32, (GMLP_CHUNK, GMLP_CHUNK), 0)
    jj = lax.broadcasted_iota(jnp.int32, (GMLP_CHUNK, GMLP_CHUNK), 1)
    same_stream = jnp.where((ii // s) == (jj // s), 1, 0) + jnp.where(is_prompt, 1, 0)
    keep = (jj <= ii) & (same_stream > 0)
    gdim = u_ref.shape[1] // GMLP_GROUPS
    for gi in range(GMLP_GROUPS):
        cols = slice(gi * gdim, (gi + 1) * gdim)
        wmat = jnp.where(keep, ws_ref[gi], 0.0).astype(BF16)
        bias = b_ref[:, gi:gi + 1]
        for ci in range(tm // GMLP_CHUNK):
            rows = slice(ci * GMLP_CHUNK, (ci + 1) * GMLP_CHUNK)
            sg = _dot(wmat, v_ref[rows, cols]) + bias
            us_ref[rows, cols] = (u_ref[rows, cols].astype(F32) * sg).astype(BF16)
    o_ref[...] = x_ref[...] + _dot(us_ref[...], w_ref[...])


def _odd_out(x, u, v, ws_sel, b_sel, wout, o, lp, s):
    m, d = x.shape
    gate = u.shape[1]
    tm = _tile(math.gcd(lp, m - lp), 512, GMLP_CHUNK)
    npt = lp // tm
    sel = lambda i: jnp.where(i >= npt, 1, 0)
    return pl.pallas_call(
        functools.partial(_odd_out_body, tm=tm, n_prompt_tiles=npt, s=s),
        grid=(m // tm,),
        in_specs=[
            pl.BlockSpec((tm, d), lambda i: (i, 0)),
            pl.BlockSpec((tm, gate), lambda i: (i, 0)),
            pl.BlockSpec((tm, gate), lambda i: (i, 0)),
            pl.BlockSpec((None, None, GMLP_GROUPS, GMLP_CHUNK, GMLP_CHUNK), lambda i: (sel(i), o, 0, 0, 0)),
            pl.BlockSpec((None, None, GMLP_CHUNK, GMLP_GROUPS), lambda i: (sel(i), o, 0, 0)),
            pl.BlockSpec((None,) + wout.shape[1:], lambda i: (o, 0, 0), pipeline_mode=pl.Buffered(1)),
        ],
        out_specs=pl.BlockSpec((tm, d), lambda i: (i, 0)),
        out_shape=jax.ShapeDtypeStruct((m, d), F32),
        scratch_shapes=[pltpu.VMEM((tm, gate), BF16)],
        compiler_params=_params("parallel"),
        name="odd_out",
    )(x, u, v, ws_sel, b_sel, wout)


def kernel(x_prompt, x_sample, cache_mla_ckv, cache_mla_kpe, state_pool, norm_ffn1, norm_mix, norm_ffn2, ffn1_w_gate, ffn1_w_up, ffn1_w_down, ffn2_w_gate, ffn2_w_up, ffn2_w_down, ev_w_in, ev_q_a_norm, ev_kv_a_norm, ev_w_qb, ev_w_kvb, ev_q_nope_norm, ev_q_pe_norm, ev_k_nope_norm, ev_k_pe_norm, ev_pool_w, ev_pool_scale, ev_w_out, od_w_in, od_v_norm, od_w_s, od_b_s, od_w_out):
    bp, lp, d = x_prompt.shape
    b, s, _ = x_sample.shape
    depth = norm_mix.shape[0]
    n_even, n_odd = ev_w_in.shape[0], od_w_in.shape[0]
    past = cache_mla_ckv.shape[2]
    q_lora, kv_lora = ev_q_a_norm.shape[1], ev_kv_a_norm.shape[1]
    pool_dim = ev_pool_scale.shape[1]
    gate = od_v_norm.shape[1]
    m = lp + b * s
    assert bp == 1 and lp % GMLP_CHUNK == 0 and (b * s) % GMLP_CHUNK == 0 and GMLP_CHUNK % s == 0
    assert past % CHUNK == 0 and s <= CHUNK, "every cached and new key must be visible to every sample query"
    assert s >= POOL_HIST and lp >= POOL_HIST and lp % s == 0
    assert ev_w_in.shape[2] == pool_dim + q_lora + kv_lora + QK_ROPE

    vec = lambda a: a[:, None, :]
    bf = lambda a: a.astype(BF16)
    pad_lanes = lambda a: jnp.pad(a, [(0, 0)] * (a.ndim - 1) + [(0, (-a.shape[-1]) % LANES)])

    w_ffn = ((ffn1_w_gate, ffn1_w_up, ffn1_w_down), (ffn2_w_gate, ffn2_w_up, ffn2_w_down))
    g_ffn = (vec(norm_ffn1), vec(norm_ffn2))
    g_mix = vec(norm_mix)
    wq = ev_w_qb.reshape(n_even, q_lora, MLA_HEADS, QK_HEAD)
    w_qb = bf(jnp.concatenate([wq[..., :QK_NOPE].reshape(n_even, q_lora, -1),
                               wq[..., QK_NOPE:].reshape(n_even, q_lora, -1)], axis=-1))
    w_kvb, pool_w = bf(ev_w_kvb), bf(ev_pool_w)
    wkv = w_kvb.reshape(n_even, kv_lora, MLA_HEADS, QK_NOPE + V_HEAD)
    w_k = wkv[..., :QK_NOPE].reshape(n_even, kv_lora, -1)
    w_v = wkv[..., QK_NOPE:].reshape(n_even, kv_lora, -1)
    qpn = vec(jnp.tile(ev_q_pe_norm, (1, LANES // QK_ROPE)))
    kpn = vec(pad_lanes(ev_k_pe_norm))
    kpe_cache_t = jnp.swapaxes(cache_mla_kpe, 2, 3)
    rep = GMLP_CHUNK // s
    ws_sel = jnp.stack([od_w_s, jnp.tile(od_w_s[:, :, :s, :s], (1, 1, rep, rep))])
    b_sel = jnp.stack([od_b_s, jnp.tile(od_b_s[:, :, :s], (1, 1, rep))]).transpose(0, 1, 3, 2)

    cos, sin, w_in_e, w_cur = _rope_tables(m, lp, past, s, ev_w_in, w_ffn[0])
    new_ckv, new_kpe, new_u, new_v = [], [], [], []
    for layer in range(depth):
        if layer == 0:
            x_s, _ = _ffn(x_sample.reshape(b * s, d), g_ffn[0], layer, w_cur)
            x_p, w_cur = _ffn(x_prompt.reshape(lp, d), g_ffn[0], layer, w_cur, w_ffn[1], layer)
            x = (x_p, x_s)
        else:
            x, w_cur = _ffn(x, g_ffn[0], layer, w_cur, w_ffn[1], layer)
        if layer % 2 == 0:
            e = layer // 2
            u, q, ckv, kpe = _even_in(x, g_mix, layer, w_in_e, vec(ev_q_a_norm), vec(ev_kv_a_norm), w_qb,
                                      vec(ev_q_nope_norm), vec(ev_k_nope_norm), qpn, kpn, cos, sin, e,
                                      pool_dim, q_lora, kv_lora)
            k_p, v_p = _kv_prompt(ckv, kpe, w_kvb, e, lp)
            if e == 0:
                attn_p, (w_out_e, w_in_o, w_out_o) = _attn_prompt(q, k_p, v_p, lp, (ev_w_out, od_w_in, od_w_out))
            else:
                attn_p, _ = _attn_prompt(q, k_p, v_p, lp)
            attn_s = _attn_sample(q, cache_mla_ckv, ckv, kpe_cache_t, kpe, w_k, w_v, e, lp, b, s)
            x = _even_out(x, u, state_pool, attn_p, attn_s, pool_w, vec(ev_pool_scale), w_out_e, e, lp, s, past)
            new_ckv.append((ckv[:lp], ckv[lp:]))
            new_kpe.append((kpe[:lp, :QK_ROPE], kpe[lp:, :QK_ROPE]))
            new_u.append((u[lp - POOL_HIST:lp], u[lp:].reshape(b, s, pool_dim)[:, s - POOL_HIST:]))
        else:
            o = layer // 2
            u, v, v_f32 = _odd_in(x, g_mix, layer, w_in_o, vec(od_v_norm), o, lp)
            x = _odd_out(x, u, v, ws_sel, b_sel, w_out_o, o, lp, s)
            new_v.append(v_f32)
        if layer + 1 < depth:
            x, w_cur = _ffn(x, g_ffn[1], layer, w_cur, w_ffn[0], layer + 1)
        else:
            y_p, _ = _ffn(x, g_ffn[1], layer, w_cur, rows=(0, lp))
            y_s, _ = _ffn(x, g_ffn[1], layer, w_cur, rows=(lp, m - lp))

    stack = lambda pairs, k, shape: jnp.stack([p[k] for p in pairs]).reshape((n_even,) + shape)
    return (
        y_p.reshape(1, lp, d),
        y_s.reshape(b, s, d),
        stack(new_ckv, 0, (1, lp, kv_lora)),
        stack(new_kpe, 0, (1, lp, QK_ROPE)),
        stack(new_u, 0, (1, POOL_HIST, pool_dim)),
        stack(new_ckv, 1, (b, s, kv_lora)),
        stack(new_kpe, 1, (b, s, QK_ROPE)),
        stack(new_u, 1, (b, POOL_HIST, pool_dim)),
        jnp.stack(new_v).reshape(n_odd, b, s, gate),
    )
```
